```python
import math
import jax, jax.numpy as jnp
from jax import lax
import numpy as np

D_MODEL = 4096
BATCH = 8
SEQ = 2048
DEPTH = 1

CHUNK = 64
Q_BLOCK = 128
D_MIX = D_MODEL
A_HEADS = 16
A_HEAD_DIM = D_MIX // 2 // A_HEADS
IDX_HEADS = 32
IDX_DIM = 64
TOPK_MAX = 256
T5_BUCKETS = 32
T5_MAX_DIST = 128
B_HEADS = 4
B_HEAD_V = D_MIX // 2 // B_HEADS
B_HEAD_K = B_HEAD_V // 2
GATE_RANK = 16
GATE_TAU = 16.0
D_FF = 11008
CONV_W = 3
ALPHA = (2 * DEPTH) ** 0.25
BETA = (8 * DEPTH) ** -0.25
EPS = 1e-6

A_Q = A_HEADS * A_HEAD_DIM
A_K = A_HEAD_DIM
A_V = A_HEAD_DIM
IDX_Q = IDX_HEADS * IDX_DIM
IDX_K = IDX_DIM
IDX_W = IDX_HEADS
B_Q = B_HEADS * B_HEAD_K
B_K = B_HEADS * B_HEAD_K
B_V = B_HEADS * B_HEAD_V
B_G = GATE_RANK
B_R = B_HEADS * B_HEAD_V
IN_SPLITS = (A_Q, A_K, A_V, IDX_Q, IDX_K, IDX_W, B_Q, B_K, B_V, B_G, B_R)
D_IN = sum(IN_SPLITS)
VALUE_SLOTS = (2, 8)

kernel_name = "hybrid_dsa_gla_convffn_deepnorm_adaln"


def layer_norm(x):
    xf = x.astype(jnp.float32)
    mu = jnp.mean(xf, axis=-1, keepdims=True)
    var = jnp.mean(jnp.square(xf - mu), axis=-1, keepdims=True)
    return ((xf - mu) * lax.rsqrt(var + EPS)).astype(x.dtype)


def layer_norm_affine(x, g, b):
    return layer_norm(x) * g + b


def modulate(x, shift, scale):
    return layer_norm(x) * (1 + scale) + shift


def t5_bucket(rel):
    half = T5_BUCKETS // 2
    max_exact = half // 2
    ret = jnp.where(rel > 0, half, 0)
    n = jnp.abs(rel)
    nf = jnp.maximum(n, 1).astype(jnp.float32)
    large = max_exact + (jnp.log(nf / max_exact) / math.log(T5_MAX_DIST / max_exact)
                         * (half - max_exact)).astype(jnp.int32)
    large = jnp.minimum(large, half - 1)
    return ret + jnp.where(n < max_exact, n, large)


def dsa_mixer(q, k, v, q_idx, k_idx, w_idx, t5_table):
    B, S = q.shape[0], q.shape[1]
    topk = min(TOPK_MAX, S // 4)
    nb = S // Q_BLOCK
    key_chunk = jnp.arange(S, dtype=jnp.int32) // CHUNK
    pos_blocks = jnp.arange(S, dtype=jnp.int32).reshape(nb, Q_BLOCK)
    gather = jax.vmap(lambda a, i: a[i])

    def to_blocks(a):
        return jnp.moveaxis(a.reshape((B, nb, Q_BLOCK) + a.shape[2:]), 1, 0)

    def block(args):
        qb, qib, wb, tb = args
        s_head = jax.nn.relu(jnp.einsum('bqhd,bsd->bqhs', qib, k_idx) * IDX_DIM ** -0.5)
        score = jnp.einsum('bqh,bqhs->bqs', wb, s_head).astype(jnp.float32)
        q_chunk = tb // CHUNK
        admissible = key_chunk[None, :] <= q_chunk[:, None]
        score = jnp.where(admissible[None], score, -jnp.inf)
        _, sel = lax.top_k(score, topk)
        valid = (sel // CHUNK) <= q_chunk[None, :, None]
        k_sel = gather(k, sel)
        v_sel = gather(v, sel)
        bias = t5_table[t5_bucket(sel - tb[None, :, None])]
        logits = (jnp.einsum('bqhd,bqkd->bhqk', qb, k_sel).astype(jnp.float32) * A_HEAD_DIM ** -0.5
                  + jnp.moveaxis(bias, 3, 1).astype(jnp.float32))
        logits = jnp.where(valid[:, None], logits, -1e30)
        p = jax.nn.softmax(logits, axis=-1).astype(v.dtype)
        return jnp.einsum('bhqk,bqkd->bqhd', p, v_sel)

    out = lax.map(block, (to_blocks(q), to_blocks(q_idx), to_blocks(w_idx), pos_blocks))
    return jnp.moveaxis(out, 0, 1).reshape(B, S, A_HEADS * A_HEAD_DIM)


def gla_mixer(q, k, v, g):
    B, S, H, DK = q.shape
    DV = v.shape[-1]
    N = S // CHUNK

    def to_chunks(a):
        return a.reshape(B, N, CHUNK, H, a.shape[-1]).transpose(1, 0, 3, 2, 4).astype(jnp.float32)

    qc, kc, vc, gc = to_chunks(q) * DK ** -0.5, to_chunks(k), to_chunks(v), to_chunks(g)
    b = jnp.cumsum(gc, axis=3)
    b_last = b[:, :, :, -1:, :]
    qe = qc * jnp.exp(b)
    ke = kc * jnp.exp(-b)
    kd = kc * jnp.exp(b_last - b)
    tril = jnp.tril(jnp.ones((CHUNK, CHUNK), jnp.float32))
    a_intra = jnp.einsum('nbhcd,nbhsd->nbhcs', qe, ke) * tril
    o_intra = jnp.einsum('nbhcs,nbhsv->nbhcv', a_intra, vc)
    decay = jnp.exp(b_last[:, :, :, 0, :])

    def step(state, inp):
        qe_n, kd_n, v_n, dec_n = inp
        o = jnp.einsum('bhcd,bhdv->bhcv', qe_n, state)
        state = state * dec_n[..., None] + jnp.einsum('bhcd,bhcv->bhdv', kd_n, v_n)
        return state, o

    s0 = jnp.zeros((B, H, DK, DV), jnp.float32)
    _, o_inter = lax.scan(step, s0, (qe, kd, vc, decay))
    o = (o_intra + o_inter).transpose(1, 0, 3, 2, 4).reshape(B, S, H, DV)
    return o


def causal_dwconv(u, w, b):
    out = lax.conv_general_dilated(
        u, w[:, None, :].astype(u.dtype), window_strides=(1,), padding=[(CONV_W - 1, 0)],
        dimension_numbers=('NWC', 'WIO', 'NWC'), feature_group_count=u.shape[-1])
    return out + b


def setup_inputs(seed: int = 0) -> dict:
    key = jax.random.key(seed)
    ks = jax.random.split(key, 20)

    def nrm(k, shape, scale):
        return scale * jax.random.normal(k, shape, jnp.float32)

    offs = np.concatenate([[0], np.cumsum(IN_SPLITS)])
    col_scale = np.ones((D_IN,), np.float32)
    for slot in VALUE_SLOTS:
        col_scale[offs[slot]:offs[slot + 1]] = BETA
    col_scale = jnp.asarray(col_scale)

    return {
        "x": nrm(ks[0], (BATCH, SEQ, D_MODEL), 1.0),
        "c": nrm(ks[1], (BATCH, D_MODEL), 1.0),
        "t5_table": nrm(ks[2], (T5_BUCKETS, A_HEADS), 0.5),
        "w_ada": nrm(ks[3], (DEPTH, D_MODEL, 6 * D_MODEL), 0.5 * D_MODEL ** -0.5),
        "b_ada": nrm(ks[4], (DEPTH, 6 * D_MODEL), 0.02),
        "w_in": nrm(ks[5], (DEPTH, D_MODEL, D_IN), D_MODEL ** -0.5) * col_scale,
        "w_g2": nrm(ks[6], (DEPTH, GATE_RANK, B_HEADS * B_HEAD_K), GATE_RANK ** -0.5),
        "b_g2": nrm(ks[7], (DEPTH, B_HEADS * B_HEAD_K), 0.1),
        "gla_norm": 1.0 + nrm(ks[8], (DEPTH, B_HEAD_V), 0.02),
        "w_out": nrm(ks[9], (DEPTH, D_MIX, D_MODEL), BETA * D_MIX ** -0.5),
        "ln1_g": 1.0 + nrm(ks[10], (DEPTH, D_MODEL), 0.02),
        "ln1_b": nrm(ks[11], (DEPTH, D_MODEL), 0.02),
        "w_up": nrm(ks[12], (DEPTH, D_MODEL, D_FF), BETA * D_MODEL ** -0.5),
        "w_gate": nrm(ks[13], (DEPTH, D_MODEL, D_FF), BETA * D_MODEL ** -0.5),
        "conv_w": nrm(ks[14], (DEPTH, CONV_W, D_FF), CONV_W ** -0.5),
        "conv_b": nrm(ks[15], (DEPTH, D_FF), 0.02),
        "w_down": nrm(ks[16], (DEPTH, D_FF, D_MODEL), BETA * D_FF ** -0.5),
        "ln2_g": 1.0 + nrm(ks[17], (DEPTH, D_MODEL), 0.02),
        "ln2_b": nrm(ks[18], (DEPTH, D_MODEL), 0.02),
    }


def reference(x, c, t5_table, w_ada, b_ada, w_in, w_g2, b_g2, gla_norm, w_out,
              ln1_g, ln1_b, w_up, w_gate, conv_w, conv_b, w_down, ln2_g, ln2_b):
    B, S, _ = x.shape
    split_points = np.cumsum(IN_SPLITS)[:-1].tolist()
    c_act = jax.nn.silu(c)
    for l in range(DEPTH):
        mod = (jnp.einsum('bd,dm->bm', c_act, w_ada[l]) + b_ada[l])[:, None, :]
        shift1, scale1, gate1, shift2, scale2, gate2 = jnp.split(mod, 6, axis=-1)

        h = modulate(x, shift1, scale1)
        proj = jnp.einsum('bsd,de->bse', h, w_in[l])
        (a_q, a_k, a_v, i_q, i_k, i_w, g_q, g_k, g_v, g_lr, g_r) = jnp.split(proj, split_points, axis=-1)

        o_a = dsa_mixer(a_q.reshape(B, S, A_HEADS, A_HEAD_DIM), a_k, a_v,
                        i_q.reshape(B, S, IDX_HEADS, IDX_DIM), i_k,
                        i_w * IDX_HEADS ** -0.5, t5_table)

        log_gate = jax.nn.log_sigmoid(
            (jnp.einsum('bsr,rk->bsk', g_lr, w_g2[l]) + b_g2[l]).astype(jnp.float32)) / GATE_TAU
        o_b = gla_mixer(g_q.reshape(B, S, B_HEADS, B_HEAD_K), g_k.reshape(B, S, B_HEADS, B_HEAD_K),
                        g_v.reshape(B, S, B_HEADS, B_HEAD_V), log_gate.reshape(B, S, B_HEADS, B_HEAD_K))
        o_b = o_b * lax.rsqrt(jnp.mean(jnp.square(o_b), axis=-1, keepdims=True) + EPS) * gla_norm[l]
        o_b = (o_b.reshape(B, S, B_V) * jax.nn.silu(g_r.astype(jnp.float32))).astype(x.dtype)

        y = jnp.einsum('bse,ed->bsd', jnp.concatenate([o_a, o_b], axis=-1), w_out[l])
        x = layer_norm_affine(ALPHA * x + gate1 * y, ln1_g[l], ln1_b[l])

        h = modulate(x, shift2, scale2)
        u = causal_dwconv(jnp.einsum('bsd,df->bsf', h, w_up[l]), conv_w[l], conv_b[l])
        f = jax.nn.gelu(u) * jnp.einsum('bsd,df->bsf', h, w_gate[l])
        y = jnp.einsum('bsf,fd->bsd', f, w_down[l])
        x = layer_norm_affine(ALPHA * x + gate2 * y, ln2_g[l], ln2_b[l])
    return x
```

```python
import functools
import math

import numpy as np
import jax
import jax.numpy as jnp
from jax import lax
from jax.experimental import pallas as pl
from jax.experimental.pallas import tpu as pltpu

F32 = jnp.float32
BF16 = jnp.bfloat16
I32 = jnp.int32

CHUNK = 64
Q_BLOCK = 128
A_HEADS = 16
A_HEAD_DIM = 128
IDX_HEADS = 32
IDX_DIM = 64
TOPK_MAX = 256
T5_BUCKETS = 32
T5_MAX_DIST = 128
B_HEADS = 4
B_HEAD_V = 512
B_HEAD_K = 256
GATE_RANK = 16
GATE_TAU = 16.0
CONV_W = 3
EPS = 1e-6

A_Q = A_HEADS * A_HEAD_DIM
IDX_Q = IDX_HEADS * IDX_DIM
B_QK = B_HEADS * B_HEAD_K
B_V = B_HEADS * B_HEAD_V
IN_SPLITS = (A_Q, A_HEAD_DIM, A_HEAD_DIM, IDX_Q, IDX_DIM, IDX_HEADS, B_QK, B_QK, B_V, GATE_RANK, B_V)
IN_NAMES = ("a_q", "a_k", "a_v", "i_q", "i_k", "i_w", "g_q", "g_k", "g_v", "g_lr", "g_r")
PROJ_ORDER = ("a_q", "i_q", "g_v", "g_r", "g_q", "g_k", "a_k", "a_v", "i_k", "i_w", "g_lr")
PROJ_TILE = 1536

LANES = 128
VMEM_LIMIT = 56 * 1024 * 1024

KEY_BLOCK = 2 * Q_BLOCK
NEG_MASK = -1e30
INT_MIN = -(2 ** 31)


def _proj_layout():
    old_off = dict(zip(IN_NAMES, np.concatenate([[0], np.cumsum(IN_SPLITS)[:-1]]).tolist()))
    width = dict(zip(IN_NAMES, IN_SPLITS))
    new_off, pos = {}, 0
    for name in PROJ_ORDER:
        new_off[name] = pos
        pos += width[name]
    total = -(-pos // PROJ_TILE) * PROJ_TILE
    return old_off, width, new_off, pos, total


_OLD_OFF, _WIDTH, _NEW_OFF, _PROJ_USED, PROJ_COLS = _proj_layout()
SMALL_OFF = _NEW_OFF["i_k"]
IK_LO = 0
IW_LO = _NEW_OFF["i_w"] - SMALL_OFF
GLR_LO = _NEW_OFF["g_lr"] - SMALL_OFF


def _layer_norm(x):
    mu = jnp.mean(x, axis=-1, keepdims=True)
    xc = x - mu
    var = jnp.mean(xc * xc, axis=-1, keepdims=True)
    return xc * lax.rsqrt(var + EPS)


def _for_row_chunks(nrows, chunk, fn):
    def body(r, carry):
        fn(pl.ds(pl.multiple_of(r * chunk, chunk), chunk))
        return carry
    lax.fori_loop(0, nrows // chunk, body, 0)


ROW_CHUNK = 64


def _cparams(sem):
    return pltpu.CompilerParams(dimension_semantics=sem, vmem_limit_bytes=VMEM_LIMIT)


def _ada_kernel(c_ref, w_ref, b_ref, o_ref):
    c = c_ref[...]
    ca = (c * jax.nn.sigmoid(c)).astype(BF16)
    o_ref[...] = jnp.dot(ca, w_ref[...].astype(BF16), preferred_element_type=F32) + b_ref[...]


def _ada(c, w, b):
    bsz, d = c.shape
    n = w.shape[1]
    tn = 512
    return pl.pallas_call(
        _ada_kernel,
        grid=(n // tn,),
        in_specs=[pl.BlockSpec((bsz, d), lambda j: (0, 0)),
                  pl.BlockSpec((d, tn), lambda j: (0, j)),
                  pl.BlockSpec((1, tn), lambda j: (0, j))],
        out_specs=pl.BlockSpec((bsz, tn), lambda j: (0, j)),
        out_shape=jax.ShapeDtypeStruct((bsz, n), F32),
        compiler_params=_cparams(("arbitrary",)),
        name="ada",
    )(c, w, b.reshape(1, n))


def _inproj_kernel(x_ref, mod_ref, w_ref, o_ref, h_ref):
    @pl.when(pl.program_id(1) == 0)
    def _():
        def rows(rs):
            xn = _layer_norm(x_ref[rs, :])
            h_ref[rs, :] = (xn * (1.0 + mod_ref[1:2, :]) + mod_ref[0:1, :]).astype(BF16)
        _for_row_chunks(x_ref.shape[0], ROW_CHUNK, rows)

    o_ref[...] = jnp.dot(h_ref[...], w_ref[...], preferred_element_type=F32).astype(o_ref.dtype)


def _inproj(x2, mod3, w, seq):
    m, d = x2.shape
    n = w.shape[1]
    tm, tn = 512, PROJ_TILE
    rows_per_seq = seq // tm
    return pl.pallas_call(
        _inproj_kernel,
        grid=(m // tm, n // tn),
        in_specs=[pl.BlockSpec((tm, d), lambda i, j: (i, 0)),
                  pl.BlockSpec((None, 6, d), lambda i, j: (i // rows_per_seq, 0, 0)),
                  pl.BlockSpec((d, tn), lambda i, j: (0, j))],
        out_specs=pl.BlockSpec((tm, tn), lambda i, j: (i, j)),
        out_shape=jax.ShapeDtypeStruct((m, n), BF16),
        scratch_shapes=[pltpu.VMEM((tm, d), BF16)],
        compiler_params=_cparams(("arbitrary", "arbitrary")),
        name="inproj",
    )(x2, mod3, w)


def _t5_bucket(rel):
    half = T5_BUCKETS // 2
    max_exact = half // 2
    ret = jnp.where(rel > 0, half, 0)
    n = jnp.abs(rel)
    nf = jnp.maximum(n, 1).astype(jnp.float32)
    large = max_exact + (jnp.log(nf / max_exact) / math.log(T5_MAX_DIST / max_exact)
                         * (half - max_exact)).astype(jnp.int32)
    large = jnp.minimum(large, half - 1)
    return ret + jnp.where(n < max_exact, n, large)


def _bias_kernel(tab_ref, bkt_ref, o_ref):
    for u in range(3):
        bk = bkt_ref[u]
        for h in range(A_HEADS):
            acc = jnp.zeros((Q_BLOCK, Q_BLOCK), F32)
            for b in range(T5_BUCKETS):
                acc = jnp.where(bk == b, tab_ref[b, h], acc)
            o_ref[u, h] = acc


def _bias_tiles(t5_table):
    i = jnp.arange(Q_BLOCK, dtype=I32)[:, None]
    j = jnp.arange(Q_BLOCK, dtype=I32)[None, :]
    rel = jnp.stack([j - i - 2 * Q_BLOCK, j - i - Q_BLOCK, j - i])
    bkt = _t5_bucket(rel).astype(I32)
    return pl.pallas_call(
        _bias_kernel,
        in_specs=[pl.BlockSpec(memory_space=pltpu.SMEM),
                  pl.BlockSpec((3, Q_BLOCK, Q_BLOCK), lambda: (0, 0, 0))],
        out_specs=pl.BlockSpec((3, A_HEADS, Q_BLOCK, Q_BLOCK), lambda: (0, 0, 0, 0)),
        out_shape=jax.ShapeDtypeStruct((3, A_HEADS, Q_BLOCK, Q_BLOCK), F32),
        name="t5_bias",
    )(t5_table, bkt)


def _dsa_kernel(aq_ref, iq_ref, sm_ref, kbd_ref, kt_ref, v_ref, tiles_ref, o_ref,
                key_ref, nm_ref, wb_ref, t_ref, q_ref, lhs_ref, m_ref, l_ref, acc_ref,
                *, nkb_max, topk):
    qb = pl.program_id(1)
    nkb = lax.shift_right_logical(qb + 2, 1)
    rows = A_HEADS * Q_BLOCK
    half = KEY_BLOCK // 2

    iw = sm_ref[:, IW_LO:IW_LO + IDX_HEADS].astype(F32) * (IDX_DIM ** -0.5 * IDX_HEADS ** -0.5)
    for hh in range(IDX_HEADS):
        wb_ref[hh] = jnp.broadcast_to(iw[:, hh:hh + 1], (Q_BLOCK, LANES))
    for p in range(IDX_HEADS // 2):
        lhs_ref[p * Q_BLOCK:(p + 1) * Q_BLOCK, :] = iq_ref[:, p * LANES:(p + 1) * LANES]
    for h in range(A_HEADS):
        q_ref[h * Q_BLOCK:(h + 1) * Q_BLOCK, :] = aq_ref[:, h * LANES:(h + 1) * LANES]

    row_chunk = lax.shift_right_logical(
        lax.broadcasted_iota(I32, (Q_BLOCK, KEY_BLOCK), 0) + qb * Q_BLOCK, 6)
    col_iota = lax.broadcasted_iota(I32, (Q_BLOCK, KEY_BLOCK), 1)

    def idx_body(kb, carry):
        res = jnp.dot(lhs_ref[...], kbd_ref[kb], preferred_element_type=F32)
        acc = jnp.zeros((Q_BLOCK, KEY_BLOCK), F32)
        for p in range(IDX_HEADS // 2):
            r = res[p * Q_BLOCK:(p + 1) * Q_BLOCK]
            we = wb_ref[2 * p]
            wo = wb_ref[2 * p + 1]
            acc = acc + jnp.maximum(r[:, :KEY_BLOCK], 0.0) * jnp.concatenate([we, we], axis=1)
            acc = acc + jnp.maximum(r[:, KEY_BLOCK:], 0.0) * jnp.concatenate([wo, wo], axis=1)
        bits = pltpu.bitcast(acc, I32)
        skey = bits ^ (lax.shift_right_arithmetic(bits, 31) & 0x7FFFFFFF)
        skey = jnp.where(acc == 0.0, 0, skey)
        adm = lax.shift_right_logical(col_iota + kb * KEY_BLOCK, 6) <= row_chunk
        key_ref[kb] = jnp.where(adm, skey, INT_MIN)
        return carry

    lax.fori_loop(0, nkb, idx_body, 0)

    def search(n):
        def count_ge(cand):
            c = jnp.zeros((Q_BLOCK, LANES), F32)
            for kb in range(n):
                k = key_ref[kb]
                c = c + jnp.where(k[:, :half] >= cand, 1.0, 0.0) + jnp.where(k[:, half:] >= cand, 1.0, 0.0)
            return jnp.broadcast_to(jnp.sum(c, axis=1, keepdims=True), (Q_BLOCK, LANES))

        zero = jnp.zeros((Q_BLOCK, LANES), I32)
        t0 = jnp.where(count_ge(zero) >= topk, zero, INT_MIN)

        def body(i, t):
            cand = t + lax.shift_left(jnp.int32(1), 30 - i)
            return jnp.where(count_ge(cand) >= topk, cand, t)

        t = lax.fori_loop(0, 31, body, t0)
        t_ref[...] = jnp.maximum(t, INT_MIN + 1)

    for n in range(1, nkb_max + 1):
        pl.when(nkb == n)(functools.partial(search, n))

    thr = t_ref[...]
    thr2 = jnp.concatenate([thr, thr], axis=1)

    def cnt_body(kb, c):
        g = jnp.where(key_ref[kb] >= thr2, 1.0, 0.0)
        return c + g[:, :half] + g[:, half:]

    n_ge = jnp.sum(lax.fori_loop(0, nkb, cnt_body, jnp.zeros((Q_BLOCK, LANES), F32)),
                   axis=1, keepdims=True)
    has_ties = jnp.max(n_ge) > topk

    @pl.when(jnp.logical_not(has_ties))
    def _():
        def body(kb, carry):
            nm_ref[kb] = jnp.where(key_ref[kb] >= thr2, 0.0, NEG_MASK)
            return carry
        lax.fori_loop(0, nkb, body, 0)

    @pl.when(has_ties)
    def _():
        def gt_body(kb, c):
            g = jnp.where(key_ref[kb] > thr2, 1.0, 0.0)
            return c + g[:, :half] + g[:, half:]
        n_gt = jnp.sum(lax.fori_loop(0, nkb, gt_body, jnp.zeros((Q_BLOCK, LANES), F32)),
                       axis=1, keepdims=True)
        need = topk - n_gt
        tri = (lax.broadcasted_iota(I32, (half, half), 0)
               <= lax.broadcasted_iota(I32, (half, half), 1)).astype(BF16)

        def body(kb, seen):
            k = key_ref[kb]
            parts = []
            for s in range(2):
                ks = k[:, s * half:(s + 1) * half]
                eq = ks == thr
                eqf = jnp.where(eq, 1.0, 0.0)
                rank = jnp.dot(eqf.astype(BF16), tri, preferred_element_type=F32) + seen
                keep = jnp.logical_or(ks > thr, jnp.logical_and(eq, rank <= need))
                parts.append(jnp.where(keep, 0.0, NEG_MASK))
                seen = seen + jnp.sum(eqf, axis=1, keepdims=True)
            nm_ref[kb] = jnp.concatenate(parts, axis=1)
            return seen
        lax.fori_loop(0, nkb, body, jnp.zeros((Q_BLOCK, 1), F32))

    scale = A_HEAD_DIM ** -0.5

    def logits(kb):
        x = jnp.dot(q_ref[...], kt_ref[kb], preferred_element_type=F32) * scale
        g0 = 2 * kb
        g1 = g0 + 1
        u0 = jnp.where(g0 == qb, 2, jnp.where(g0 == qb - 1, 1, 0))
        u1 = jnp.where(g1 == qb, 2, jnp.where(g1 == qb - 1, 1, 0))
        bias = jnp.concatenate([tiles_ref[u0].reshape(rows, half),
                                tiles_ref[u1].reshape(rows, half)], axis=1)
        x = (x + bias).reshape(A_HEADS, Q_BLOCK, KEY_BLOCK) + nm_ref[kb][None]
        return x.reshape(rows, KEY_BLOCK)

    m_ref[...] = jnp.full((rows, LANES), NEG_MASK, F32)
    l_ref[...] = jnp.zeros((rows, LANES), F32)
    acc_ref[...] = jnp.zeros((rows, A_HEAD_DIM), F32)

    def max_body(kb, carry):
        x = logits(kb)
        m_ref[...] = jnp.maximum(m_ref[...], jnp.maximum(x[:, :half], x[:, half:]))
        return carry

    lax.fori_loop(0, nkb, max_body, 0)
    m_ref[...] = jnp.broadcast_to(jnp.max(m_ref[...], axis=1, keepdims=True), (rows, LANES))

    def pv_body(kb, carry):
        x = logits(kb)
        m = m_ref[...]
        p = jnp.exp(x - jnp.concatenate([m, m], axis=1))
        l_ref[...] += p[:, :half] + p[:, half:]
        vb = v_ref[pl.ds(pl.multiple_of(kb * KEY_BLOCK, KEY_BLOCK), KEY_BLOCK), :]
        acc_ref[...] += jnp.dot(p.astype(BF16), vb, preferred_element_type=F32)
        return carry

    lax.fori_loop(0, nkb, pv_body, 0)

    out = acc_ref[...] / jnp.sum(l_ref[...], axis=1, keepdims=True)
    for h in range(A_HEADS):
        o_ref[:, h * A_HEAD_DIM:(h + 1) * A_HEAD_DIM] = out[h * Q_BLOCK:(h + 1) * Q_BLOCK].astype(o_ref.dtype)


def _dsa(proj, kbd, kt, tiles, bsz, seq):
    nb = seq // Q_BLOCK
    nkb = seq // KEY_BLOCK
    topk = min(TOPK_MAX, seq // 4)
    rows = A_HEADS * Q_BLOCK
    ak_blk = _NEW_OFF["a_v"] // A_HEAD_DIM
    kern = functools.partial(_dsa_kernel, nkb_max=nkb, topk=topk)
    return pl.pallas_call(
        kern,
        grid=(bsz, nb),
        in_specs=[
            pl.BlockSpec((Q_BLOCK, A_Q), lambda b, q: (b * nb + q, _NEW_OFF["a_q"] // A_Q)),
            pl.BlockSpec((Q_BLOCK, IDX_Q), lambda b, q: (b * nb + q, _NEW_OFF["i_q"] // IDX_Q)),
            pl.BlockSpec((Q_BLOCK, LANES), lambda b, q: (b * nb + q, SMALL_OFF // LANES)),
            pl.BlockSpec((None, nkb, 2 * IDX_DIM, 2 * KEY_BLOCK), lambda b, q: (b, 0, 0, 0)),
            pl.BlockSpec((None, nkb, A_HEAD_DIM, KEY_BLOCK), lambda b, q: (b, 0, 0, 0)),
            pl.BlockSpec((seq, A_HEAD_DIM), lambda b, q: (b, ak_blk)),
            pl.BlockSpec((3, A_HEADS, Q_BLOCK, Q_BLOCK), lambda b, q: (0, 0, 0, 0)),
        ],
        out_specs=pl.BlockSpec((Q_BLOCK, A_Q), lambda b, q: (b * nb + q, 0)),
        out_shape=jax.ShapeDtypeStruct((bsz * seq, A_Q), BF16),
        scratch_shapes=[
            pltpu.VMEM((nkb, Q_BLOCK, KEY_BLOCK), I32),
            pltpu.VMEM((nkb, Q_BLOCK, KEY_BLOCK), F32),
            pltpu.VMEM((IDX_HEADS, Q_BLOCK, LANES), F32),
            pltpu.VMEM((Q_BLOCK, LANES), I32),
            pltpu.VMEM((rows, A_HEAD_DIM), BF16),
            pltpu.VMEM((rows, LANES), BF16),
            pltpu.VMEM((rows, LANES), F32),
            pltpu.VMEM((rows, LANES), F32),
            pltpu.VMEM((rows, A_HEAD_DIM), F32),
        ],
        compiler_params=_cparams(("arbitrary", "arbitrary")),
        name="dsa",
    )(proj, proj, proj, kbd, kt, proj, tiles)


def _gla_kernel(q_ref, k_ref, v_ref, r_ref, sm_ref, wg_ref, bg_ref, nrm_ref, o_ref, st_ref, *, nchunk):
    st_ref[...] = jnp.zeros(st_ref.shape, F32)
    ri = lax.broadcasted_iota(I32, (CHUNK, CHUNK), 0)
    ci = lax.broadcasted_iota(I32, (CHUNK, CHUNK), 1)
    tril = ri >= ci
    tril_bf = tril.astype(BF16)
    ones_bf = jnp.ones((CHUNK, LANES), BF16)
    tn = (((0,), (0,)), ((), ()))
    nt = (((1,), (1,)), ((), ()))

    def body(n, carry):
        r0 = pl.multiple_of(n * CHUNK, CHUNK)
        qc = q_ref[pl.ds(r0, CHUNK), :].astype(F32) * (B_HEAD_K ** -0.5)
        kc = k_ref[pl.ds(r0, CHUNK), :].astype(F32)
        vc = v_ref[pl.ds(r0, CHUNK), :]
        glr = sm_ref[pl.ds(r0, CHUNK), :][:, GLR_LO:GLR_LO + GATE_RANK]
        z = jnp.dot(glr, wg_ref[...], preferred_element_type=F32) + bg_ref[...]
        g = (jnp.minimum(z, 0.0) - jnp.log1p(jnp.exp(-jnp.abs(z)))) * (1.0 / GATE_TAU)
        g_hi = g.astype(BF16)
        g_lo = (g - g_hi.astype(F32)).astype(BF16)
        b = (jnp.dot(tril_bf, g_hi, preferred_element_type=F32)
             + jnp.dot(tril_bf, g_lo, preferred_element_type=F32))
        b_last = b[CHUNK - 1:CHUNK, :]
        qe = (qc * jnp.exp(b)).astype(BF16)
        ke = (kc * jnp.exp(-b)).astype(BF16)
        kd = (kc * jnp.exp(b_last - b)).astype(BF16)
        a = lax.dot_general(qe, ke, nt, preferred_element_type=F32)
        a = jnp.where(tril, a, 0.0).astype(BF16)
        o = jnp.dot(a, vc, preferred_element_type=F32)
        st = st_ref[...]
        o = o + jnp.dot(qe, st.astype(BF16), preferred_element_type=F32)
        bl = (lax.dot_general(g_hi, ones_bf, tn, preferred_element_type=F32)
              + lax.dot_general(g_lo, ones_bf, tn, preferred_element_type=F32))
        dec = jnp.exp(bl)
        dec = jnp.concatenate([dec] * (B_HEAD_V // LANES), axis=1)
        st_ref[...] = st * dec + lax.dot_general(kd, vc, tn, preferred_element_type=F32)
        ms = jnp.mean(o * o, axis=-1, keepdims=True)
        on = o * lax.rsqrt(ms + EPS) * nrm_ref[...]
        rr = r_ref[pl.ds(r0, CHUNK), :].astype(F32)
        o_ref[pl.ds(r0, CHUNK), :] = (on * (rr * jax.nn.sigmoid(rr))).astype(o_ref.dtype)
        return carry

    lax.fori_loop(0, nchunk, body, 0)


def _gla(proj, w_g2, b_g2, gla_norm, bsz, seq):
    q_blk = _NEW_OFF["g_q"] // B_HEAD_K
    k_blk = _NEW_OFF["g_k"] // B_HEAD_K
    v_blk = _NEW_OFF["g_v"] // B_HEAD_V
    r_blk = _NEW_OFF["g_r"] // B_HEAD_V
    kern = functools.partial(_gla_kernel, nchunk=seq // CHUNK)
    return pl.pallas_call(
        kern,
        grid=(bsz, B_HEADS),
        in_specs=[
            pl.BlockSpec((seq, B_HEAD_K), lambda b, h: (b, q_blk + h)),
            pl.BlockSpec((seq, B_HEAD_K), lambda b, h: (b, k_blk + h)),
            pl.BlockSpec((seq, B_HEAD_V), lambda b, h: (b, v_blk + h)),
            pl.BlockSpec((seq, B_HEAD_V), lambda b, h: (b, r_blk + h)),
            pl.BlockSpec((seq, LANES), lambda b, h: (b, SMALL_OFF // LANES)),
            pl.BlockSpec((GATE_RANK, B_HEAD_K), lambda b, h: (0, h)),
            pl.BlockSpec((1, B_HEAD_K), lambda b, h: (0, h)),
            pl.BlockSpec((1, B_HEAD_V), lambda b, h: (0, 0)),
        ],
        out_specs=pl.BlockSpec((seq, B_HEAD_V), lambda b, h: (b, h)),
        out_shape=jax.ShapeDtypeStruct((bsz * seq, B_V), BF16),
        scratch_shapes=[pltpu.VMEM((B_HEAD_K, B_HEAD_V), F32)],
        compiler_params=_cparams(("arbitrary", "arbitrary")),
        name="gla",
    )(proj, proj, proj, proj, proj, w_g2.astype(BF16), b_g2.reshape(1, -1), gla_norm.reshape(1, -1))


def _outproj_kernel(oa_ref, ob_ref, w_ref, x_ref, modj_ref, mod_ref, g_ref, b_ref, x1_ref, h2_ref,
                    *, alpha, nj, tn):
    j = pl.program_id(1)
    a = jnp.concatenate([oa_ref[...], ob_ref[...]], axis=1)
    y = jnp.dot(a, w_ref[...], preferred_element_type=F32)
    z = alpha * x_ref[...] + modj_ref[2:3, :] * y
    for jj in range(nj):
        @pl.when(j == jj)
        def _(jj=jj):
            x1_ref[:, jj * tn:(jj + 1) * tn] = z

    @pl.when(j == nj - 1)
    def _():
        def rows(rs):
            x1 = _layer_norm(x1_ref[rs, :]) * g_ref[...] + b_ref[...]
            x1_ref[rs, :] = x1
            h2_ref[rs, :] = (_layer_norm(x1) * (1.0 + mod_ref[4:5, :]) + mod_ref[3:4, :]).astype(h2_ref.dtype)
        _for_row_chunks(x1_ref.shape[0], ROW_CHUNK, rows)


def _outproj(o_a, o_b, w, x2, mod3, ln_g, ln_b, seq, alpha):
    m, d = x2.shape
    tm, tn = 512, 512
    nj = d // tn
    rows_per_seq = seq // tm
    kern = functools.partial(_outproj_kernel, alpha=alpha, nj=nj, tn=tn)
    return pl.pallas_call(
        kern,
        grid=(m // tm, nj),
        in_specs=[
            pl.BlockSpec((tm, A_Q), lambda i, j: (i, 0)),
            pl.BlockSpec((tm, B_V), lambda i, j: (i, 0)),
            pl.BlockSpec((A_Q + B_V, tn), lambda i, j: (0, j)),
            pl.BlockSpec((tm, tn), lambda i, j: (i, j)),
            pl.BlockSpec((None, 6, tn), lambda i, j: (i // rows_per_seq, 0, j)),
            pl.BlockSpec((None, 6, d), lambda i, j: (i // rows_per_seq, 0, 0)),
            pl.BlockSpec((1, d), lambda i, j: (0, 0)),
            pl.BlockSpec((1, d), lambda i, j: (0, 0)),
        ],
        out_specs=[pl.BlockSpec((tm, d), lambda i, j: (i, 0)),
                   pl.BlockSpec((tm, d), lambda i, j: (i, 0))],
        out_shape=[jax.ShapeDtypeStruct((m, d), F32), jax.ShapeDtypeStruct((m, d), BF16)],
        compiler_params=_cparams(("arbitrary", "arbitrary")),
        name="outproj",
    )(o_a, o_b, w, x2, mod3, mod3, ln_g.reshape(1, d), ln_b.reshape(1, d))


def _ffn_kernel(h_ref, wu_ref, wg_ref, cw_ref, cb_ref, wd_ref, x1_ref, mod_ref, g_ref, b_ref, o_ref,
                carry_ref, *, alpha, nj, rows_per_seq):
    i = pl.program_id(0)
    j = pl.program_id(1)
    h = h_ref[...]
    tm = h.shape[0]
    u = jnp.dot(h, wu_ref[...], preferred_element_type=F32)
    gt = jnp.dot(h, wg_ref[...], preferred_element_type=F32)

    first = (i % rows_per_seq) == 0
    prev = jnp.where(first, 0.0, carry_ref[j])
    carry_ref[j] = u[tm - 8:tm, :]
    rid = lax.broadcasted_iota(I32, u.shape, 0)
    p1 = prev[7:8, :]
    p2 = prev[6:7, :]
    u1 = jnp.where(rid == 0, p1, pltpu.roll(u, 1, axis=0))
    u2 = jnp.where(rid == 0, p2, jnp.where(rid == 1, p1, pltpu.roll(u, 2, axis=0)))
    cv = cw_ref[0:1, :] * u2 + cw_ref[1:2, :] * u1 + cw_ref[2:3, :] * u + cb_ref[...]
    cdf = 0.5 * (1.0 + jnp.tanh(math.sqrt(2.0 / math.pi) * (cv + 0.044715 * (cv ** 3))))
    act = (cv * cdf * gt).astype(BF16)
    contrib = jnp.dot(act, wd_ref[...], preferred_element_type=F32)

    @pl.when(j == 0)
    def _():
        o_ref[...] = contrib

    @pl.when(j > 0)
    def _():
        o_ref[...] += contrib

    @pl.when(j == nj - 1)
    def _():
        def rows(rs):
            z = alpha * x1_ref[rs, :] + mod_ref[5:6, :] * o_ref[rs, :]
            o_ref[rs, :] = _layer_norm(z) * g_ref[...] + b_ref[...]
        _for_row_chunks(o_ref.shape[0], ROW_CHUNK, rows)


def _ffn(h2, w_up, w_gate, conv_w, conv_b, w_down, x1, mod3, ln_g, ln_b, seq, alpha):
    m, d = h2.shape
    f = w_up.shape[1]
    tm, tf = 512, 256
    nj = f // tf
    rows_per_seq = seq // tm
    kern = functools.partial(_ffn_kernel, alpha=alpha, nj=nj, rows_per_seq=rows_per_seq)
    once = pl.Buffered(1)
    return pl.pallas_call(
        kern,
        grid=(m // tm, nj),
        in_specs=[
            pl.BlockSpec((tm, d), lambda i, j: (i, 0), pipeline_mode=once),
            pl.BlockSpec((d, tf), lambda i, j: (0, j)),
            pl.BlockSpec((d, tf), lambda i, j: (0, j)),
            pl.BlockSpec((CONV_W, tf), lambda i, j: (0, j)),
            pl.BlockSpec((1, tf), lambda i, j: (0, j)),
            pl.BlockSpec((tf, d), lambda i, j: (j, 0)),
            pl.BlockSpec((tm, d), lambda i, j: (i, 0), pipeline_mode=once),
            pl.BlockSpec((None, 6, d), lambda i, j: (i // rows_per_seq, 0, 0)),
            pl.BlockSpec((1, d), lambda i, j: (0, 0)),
            pl.BlockSpec((1, d), lambda i, j: (0, 0)),
        ],
        out_specs=pl.BlockSpec((tm, d), lambda i, j: (i, 0)),
        out_shape=jax.ShapeDtypeStruct((m, d), F32),
        scratch_shapes=[pltpu.VMEM((nj, 8, tf), F32)],
        compiler_params=_cparams(("arbitrary", "arbitrary")),
        name="ffn",
    )(h2, w_up, w_gate, conv_w, conv_b.reshape(1, f), w_down, x1, mod3, ln_g.reshape(1, d), ln_b.reshape(1, d))


def _regroup_w_in(w):
    d = w.shape[0]
    parts = [w[:, _OLD_OFF[n]:_OLD_OFF[n] + _WIDTH[n]] for n in PROJ_ORDER]
    parts.append(jnp.zeros((d, PROJ_COLS - _PROJ_USED), w.dtype))
    return jnp.concatenate(parts, axis=1).astype(BF16)


def kernel(x, c, t5_table, w_ada, b_ada, w_in, w_g2, b_g2, gla_norm, w_out, ln1_g, ln1_b, w_up, w_gate,
           conv_w, conv_b, w_down, ln2_g, ln2_b):
    bsz, seq, d = x.shape
    depth = w_ada.shape[0]
    alpha = (2 * depth) ** 0.25
    nkb = seq // KEY_BLOCK
    x2 = x.reshape(bsz * seq, d)
    tiles = _bias_tiles(t5_table)
    for l in range(depth):
        mod3 = _ada(c, w_ada[l], b_ada[l]).reshape(bsz, 6, d)
        proj = _inproj(x2, mod3, _regroup_w_in(w_in[l]), seq)

        p3 = proj.reshape(bsz, nkb, KEY_BLOCK, PROJ_COLS)
        ko, io = _NEW_OFF["a_k"], _NEW_OFF["i_k"]
        kt = jnp.swapaxes(p3[..., ko:ko + A_HEAD_DIM], 2, 3)
        kit = jnp.swapaxes(p3[..., io:io + IDX_DIM], 2, 3)
        zz = jnp.zeros_like(kit)
        kbd = jnp.concatenate([jnp.concatenate([kit, zz], axis=3),
                               jnp.concatenate([zz, kit], axis=3)], axis=2)

        o_a = _dsa(proj, kbd, kt, tiles, bsz, seq)
        o_b = _gla(proj, w_g2[l], b_g2[l], gla_norm[l], bsz, seq)
        x1, h2 = _outproj(o_a, o_b, w_out[l].astype(BF16), x2, mod3, ln1_g[l], ln1_b[l], seq, alpha)
        x2 = _ffn(h2, w_up[l].astype(BF16), w_gate[l].astype(BF16), conv_w[l], conv_b[l],
                  w_down[l].astype(BF16), x1, mod3, ln2_g[l], ln2_b[l], seq, alpha)
    return x2.reshape(bsz, seq, d)
```

```python
import functools
import math

import numpy as np
import jax
import jax.numpy as jnp
from jax import lax
from jax.experimental import pallas as pl
from jax.experimental.pallas import tpu as pltpu

F32 = jnp.float32
BF16 = jnp.bfloat16
I32 = jnp.int32

CHUNK = 64
Q_BLOCK = 128
A_HEADS = 16
A_HEAD_DIM = 128
IDX_HEADS = 32
IDX_DIM = 64
TOPK_MAX = 256
T5_BUCKETS = 32
T5_MAX_DIST = 128
B_HEADS = 4
B_HEAD_V = 512
B_HEAD_K = 256
GATE_RANK = 16
GATE_TAU = 16.0
CONV_W = 3
EPS = 1e-6

A_Q = A_HEADS * A_HEAD_DIM
IDX_Q = IDX_HEADS * IDX_DIM
B_QK = B_HEADS * B_HEAD_K
B_V = B_HEADS * B_HEAD_V
IN_SPLITS = (A_Q, A_HEAD_DIM, A_HEAD_DIM, IDX_Q, IDX_DIM, IDX_HEADS, B_QK, B_QK, B_V, GATE_RANK, B_V)
IN_NAMES = ("a_q", "a_k", "a_v", "i_q", "i_k", "i_w", "g_q", "g_k", "g_v", "g_lr", "g_r")
PROJ_ORDER = ("a_q", "i_q", "g_v", "g_r", "g_q", "g_k", "a_k", "a_v", "i_k", "i_w", "g_lr")
PROJ_TILE = 1536

LANES = 128
VMEM_LIMIT = 56 * 1024 * 1024
VMEM_LIMIT_FFN = 60 * 1024 * 1024

KEY_BLOCK = 2 * Q_BLOCK
NEG_MASK = -1e30
INT_MIN = -(2 ** 31)


def _proj_layout():
    old_off = dict(zip(IN_NAMES, np.concatenate([[0], np.cumsum(IN_SPLITS)[:-1]]).tolist()))
    width = dict(zip(IN_NAMES, IN_SPLITS))
    new_off, pos = {}, 0
    for name in PROJ_ORDER:
        new_off[name] = pos
        pos += width[name]
    total = -(-pos // PROJ_TILE) * PROJ_TILE
    return old_off, width, new_off, pos, total


_OLD_OFF, _WIDTH, _NEW_OFF, _PROJ_USED, PROJ_COLS = _proj_layout()
SMALL_OFF = _NEW_OFF["i_k"]
IK_LO = 0
IW_LO = _NEW_OFF["i_w"] - SMALL_OFF
GLR_LO = _NEW_OFF["g_lr"] - SMALL_OFF


def _layer_norm(x):
    mu = jnp.mean(x, axis=-1, keepdims=True)
    xc = x - mu
    var = jnp.mean(xc * xc, axis=-1, keepdims=True)
    return xc * lax.rsqrt(var + EPS)


def _for_row_chunks(nrows, chunk, fn):
    def body(r, carry):
        fn(pl.ds(pl.multiple_of(r * chunk, chunk), chunk))
        return carry
    lax.fori_loop(0, nrows // chunk, body, 0)


ROW_CHUNK = 64
FFN_TILE = 512
GLA_SEQ_TILE = 512
FFN_DOWN_CHUNK = 512


def _cparams(sem, vmem=VMEM_LIMIT):
    return pltpu.CompilerParams(dimension_semantics=sem, vmem_limit_bytes=vmem)


def _ada_kernel(c_ref, w_ref, b_ref, o_ref):
    c = c_ref[...]
    ca = (c * jax.nn.sigmoid(c)).astype(BF16)
    o_ref[...] = jnp.dot(ca, w_ref[...].astype(BF16), preferred_element_type=F32) + b_ref[...]


def _ada(c, w, b):
    bsz, d = c.shape
    n = w.shape[1]
    tn = 512
    return pl.pallas_call(
        _ada_kernel,
        grid=(n // tn,),
        in_specs=[pl.BlockSpec((bsz, d), lambda j: (0, 0)),
                  pl.BlockSpec((d, tn), lambda j: (0, j)),
                  pl.BlockSpec((1, tn), lambda j: (0, j))],
        out_specs=pl.BlockSpec((bsz, tn), lambda j: (0, j)),
        out_shape=jax.ShapeDtypeStruct((bsz, n), F32),
        compiler_params=_cparams(("arbitrary",)),
        name="ada",
    )(c, w, b.reshape(1, n))


def _inproj_kernel(x_ref, mod_ref, w_ref, o_ref, h_ref):
    @pl.when(pl.program_id(1) == 0)
    def _():
        def rows(rs):
            xn = _layer_norm(x_ref[rs, :])
            h_ref[rs, :] = (xn * (1.0 + mod_ref[1:2, :]) + mod_ref[0:1, :]).astype(BF16)
        _for_row_chunks(x_ref.shape[0], ROW_CHUNK, rows)

    o_ref[...] = jnp.dot(h_ref[...], w_ref[...], preferred_element_type=F32).astype(o_ref.dtype)


def _inproj(x2, mod3, w, seq):
    m, d = x2.shape
    n = w.shape[1]
    tm, tn = 512, PROJ_TILE
    rows_per_seq = seq // tm
    return pl.pallas_call(
        _inproj_kernel,
        grid=(m // tm, n // tn),
        in_specs=[pl.BlockSpec((tm, d), lambda i, j: (i, 0)),
                  pl.BlockSpec((None, 6, d), lambda i, j: (i // rows_per_seq, 0, 0)),
                  pl.BlockSpec((d, tn), lambda i, j: (0, j))],
        out_specs=pl.BlockSpec((tm, tn), lambda i, j: (i, j)),
        out_shape=jax.ShapeDtypeStruct((m, n), BF16),
        scratch_shapes=[pltpu.VMEM((tm, d), BF16)],
        compiler_params=_cparams(("arbitrary", "arbitrary")),
        name="inproj",
    )(x2, mod3, w)


def _t5_bucket(rel):
    half = T5_BUCKETS // 2
    max_exact = half // 2
    ret = jnp.where(rel > 0, half, 0)
    n = jnp.abs(rel)
    nf = jnp.maximum(n, 1).astype(jnp.float32)
    large = max_exact + (jnp.log(nf / max_exact) / math.log(T5_MAX_DIST / max_exact)
                         * (half - max_exact)).astype(jnp.int32)
    large = jnp.minimum(large, half - 1)
    return ret + jnp.where(n < max_exact, n, large)


def _bias_kernel(tab_ref, bkt_ref, o_ref):
    for u in range(3):
        bk = bkt_ref[u]
        for h in range(A_HEADS):
            acc = jnp.zeros((Q_BLOCK, Q_BLOCK), F32)
            for b in range(T5_BUCKETS):
                acc = jnp.where(bk == b, tab_ref[b, h], acc)
            o_ref[u, h] = acc


def _bias_tiles(t5_table):
    i = jnp.arange(Q_BLOCK, dtype=I32)[:, None]
    j = jnp.arange(Q_BLOCK, dtype=I32)[None, :]
    rel = jnp.stack([j - i - 2 * Q_BLOCK, j - i - Q_BLOCK, j - i])
    bkt = _t5_bucket(rel).astype(I32)
    return pl.pallas_call(
        _bias_kernel,
        in_specs=[pl.BlockSpec(memory_space=pltpu.SMEM),
                  pl.BlockSpec((3, Q_BLOCK, Q_BLOCK), lambda: (0, 0, 0))],
        out_specs=pl.BlockSpec((3, A_HEADS, Q_BLOCK, Q_BLOCK), lambda: (0, 0, 0, 0)),
        out_shape=jax.ShapeDtypeStruct((3, A_HEADS, Q_BLOCK, Q_BLOCK), F32),
        name="t5_bias",
    )(t5_table, bkt)


def _dsa_kernel(aq_ref, iq_ref, sm_ref, kbd_ref, kt_ref, v_ref, tiles_ref, o_ref,
                key_ref, nm_ref, wb_ref, t_ref, q_ref, lhs_ref, m_ref, l_ref, acc_ref,
                *, nkb_max, topk):
    qb = pl.program_id(1)
    nkb = lax.shift_right_logical(qb + 2, 1)
    rows = A_HEADS * Q_BLOCK
    half = KEY_BLOCK // 2

    iw = sm_ref[:, IW_LO:IW_LO + IDX_HEADS].astype(F32) * (IDX_DIM ** -0.5 * IDX_HEADS ** -0.5)
    for hh in range(IDX_HEADS):
        wb_ref[hh] = jnp.broadcast_to(iw[:, hh:hh + 1], (Q_BLOCK, LANES))
    for p in range(IDX_HEADS // 2):
        lhs_ref[p * Q_BLOCK:(p + 1) * Q_BLOCK, :] = iq_ref[:, p * LANES:(p + 1) * LANES]
    for h in range(A_HEADS):
        q_ref[h * Q_BLOCK:(h + 1) * Q_BLOCK, :] = aq_ref[:, h * LANES:(h + 1) * LANES]

    row_chunk = lax.shift_right_logical(
        lax.broadcasted_iota(I32, (Q_BLOCK, KEY_BLOCK), 0) + qb * Q_BLOCK, 6)
    col_iota = lax.broadcasted_iota(I32, (Q_BLOCK, KEY_BLOCK), 1)

    def idx_body(kb, carry):
        res = jnp.dot(lhs_ref[...], kbd_ref[kb], preferred_element_type=F32)
        acc = jnp.zeros((Q_BLOCK, KEY_BLOCK), F32)
        for p in range(IDX_HEADS // 2):
            r = res[p * Q_BLOCK:(p + 1) * Q_BLOCK]
            we = wb_ref[2 * p]
            wo = wb_ref[2 * p + 1]
            acc = acc + jnp.maximum(r[:, :KEY_BLOCK], 0.0) * jnp.concatenate([we, we], axis=1)
            acc = acc + jnp.maximum(r[:, KEY_BLOCK:], 0.0) * jnp.concatenate([wo, wo], axis=1)
        bits = pltpu.bitcast(acc, I32)
        skey = bits ^ (lax.shift_right_arithmetic(bits, 31) & 0x7FFFFFFF)
        skey = jnp.where(acc == 0.0, 0, skey)
        adm = lax.shift_right_logical(col_iota + kb * KEY_BLOCK, 6) <= row_chunk
        key_ref[kb] = jnp.where(adm, skey, INT_MIN)
        return carry

    lax.fori_loop(0, nkb, idx_body, 0)

    def search(n):
        def count_ge(cand):
            c = jnp.zeros((Q_BLOCK, LANES), F32)
            for kb in range(n):
                k = key_ref[kb]
                c = c + jnp.where(k[:, :half] >= cand, 1.0, 0.0) + jnp.where(k[:, half:] >= cand, 1.0, 0.0)
            return jnp.broadcast_to(jnp.sum(c, axis=1, keepdims=True), (Q_BLOCK, LANES))

        zero = jnp.zeros((Q_BLOCK, LANES), I32)
        t0 = jnp.where(count_ge(zero) >= topk, zero, INT_MIN)

        def body(i, t):
            cand = t + lax.shift_left(jnp.int32(1), 30 - i)
            return jnp.where(count_ge(cand) >= topk, cand, t)

        t = lax.fori_loop(0, 31, body, t0)
        t_ref[...] = jnp.maximum(t, INT_MIN + 1)

    for n in range(1, nkb_max + 1):
        pl.when(nkb == n)(functools.partial(search, n))

    thr = t_ref[...]
    thr2 = jnp.concatenate([thr, thr], axis=1)

    def cnt_body(kb, c):
        g = jnp.where(key_ref[kb] >= thr2, 1.0, 0.0)
        return c + g[:, :half] + g[:, half:]

    n_ge = jnp.sum(lax.fori_loop(0, nkb, cnt_body, jnp.zeros((Q_BLOCK, LANES), F32)),
                   axis=1, keepdims=True)
    has_ties = jnp.max(n_ge) > topk

    @pl.when(jnp.logical_not(has_ties))
    def _():
        def body(kb, carry):
            nm_ref[kb] = jnp.where(key_ref[kb] >= thr2, 0.0, NEG_MASK)
            return carry
        lax.fori_loop(0, nkb, body, 0)

    @pl.when(has_ties)
    def _():
        def gt_body(kb, c):
            g = jnp.where(key_ref[kb] > thr2, 1.0, 0.0)
            return c + g[:, :half] + g[:, half:]
        n_gt = jnp.sum(lax.fori_loop(0, nkb, gt_body, jnp.zeros((Q_BLOCK, LANES), F32)),
                       axis=1, keepdims=True)
        need = topk - n_gt
        tri = (lax.broadcasted_iota(I32, (half, half), 0)
               <= lax.broadcasted_iota(I32, (half, half), 1)).astype(BF16)

        def body(kb, seen):
            k = key_ref[kb]
            parts = []
            for s in range(2):
                ks = k[:, s * half:(s + 1) * half]
                eq = ks == thr
                eqf = jnp.where(eq, 1.0, 0.0)
                rank = jnp.dot(eqf.astype(BF16), tri, preferred_element_type=F32) + seen
                keep = jnp.logical_or(ks > thr, jnp.logical_and(eq, rank <= need))
                parts.append(jnp.where(keep, 0.0, NEG_MASK))
                seen = seen + jnp.sum(eqf, axis=1, keepdims=True)
            nm_ref[kb] = jnp.concatenate(parts, axis=1)
            return seen
        lax.fori_loop(0, nkb, body, jnp.zeros((Q_BLOCK, 1), F32))

    scale = A_HEAD_DIM ** -0.5

    def logits(kb):
        x = jnp.dot(q_ref[...], kt_ref[kb], preferred_element_type=F32) * scale
        g0 = 2 * kb
        g1 = g0 + 1
        u0 = jnp.where(g0 == qb, 2, jnp.where(g0 == qb - 1, 1, 0))
        u1 = jnp.where(g1 == qb, 2, jnp.where(g1 == qb - 1, 1, 0))
        bias = jnp.concatenate([tiles_ref[u0].reshape(rows, half),
                                tiles_ref[u1].reshape(rows, half)], axis=1)
        x = (x + bias).reshape(A_HEADS, Q_BLOCK, KEY_BLOCK) + nm_ref[kb][None]
        return x.reshape(rows, KEY_BLOCK)

    m_ref[...] = jnp.full((rows, LANES), NEG_MASK, F32)
    l_ref[...] = jnp.zeros((rows, LANES), F32)
    acc_ref[...] = jnp.zeros((rows, A_HEAD_DIM), F32)

    def max_body(kb, carry):
        x = logits(kb)
        m_ref[...] = jnp.maximum(m_ref[...], jnp.maximum(x[:, :half], x[:, half:]))
        return carry

    lax.fori_loop(0, nkb, max_body, 0)
    m_ref[...] = jnp.broadcast_to(jnp.max(m_ref[...], axis=1, keepdims=True), (rows, LANES))

    def pv_body(kb, carry):
        x = logits(kb)
        m = m_ref[...]
        p = jnp.exp(x - jnp.concatenate([m, m], axis=1))
        l_ref[...] += p[:, :half] + p[:, half:]
        vb = v_ref[pl.ds(pl.multiple_of(kb * KEY_BLOCK, KEY_BLOCK), KEY_BLOCK), :]
        acc_ref[...] += jnp.dot(p.astype(BF16), vb, preferred_element_type=F32)
        return carry

    lax.fori_loop(0, nkb, pv_body, 0)

    out = acc_ref[...] / jnp.sum(l_ref[...], axis=1, keepdims=True)
    for h in range(A_HEADS):
        o_ref[:, h * A_HEAD_DIM:(h + 1) * A_HEAD_DIM] = out[h * Q_BLOCK:(h + 1) * Q_BLOCK].astype(o_ref.dtype)


def _dsa(proj, kbd, kt, tiles, bsz, seq):
    nb = seq // Q_BLOCK
    nkb = seq // KEY_BLOCK
    topk = min(TOPK_MAX, seq // 4)
    rows = A_HEADS * Q_BLOCK
    ak_blk = _NEW_OFF["a_v"] // A_HEAD_DIM
    kern = functools.partial(_dsa_kernel, nkb_max=nkb, topk=topk)
    return pl.pallas_call(
        kern,
        grid=(bsz, nb),
        in_specs=[
            pl.BlockSpec((Q_BLOCK, A_Q), lambda b, q: (b * nb + q, _NEW_OFF["a_q"] // A_Q)),
            pl.BlockSpec((Q_BLOCK, IDX_Q), lambda b, q: (b * nb + q, _NEW_OFF["i_q"] // IDX_Q)),
            pl.BlockSpec((Q_BLOCK, LANES), lambda b, q: (b * nb + q, SMALL_OFF // LANES)),
            pl.BlockSpec((None, nkb, 2 * IDX_DIM, 2 * KEY_BLOCK), lambda b, q: (b, 0, 0, 0)),
            pl.BlockSpec((None, nkb, A_HEAD_DIM, KEY_BLOCK), lambda b, q: (b, 0, 0, 0)),
            pl.BlockSpec((seq, A_HEAD_DIM), lambda b, q: (b, ak_blk)),
            pl.BlockSpec((3, A_HEADS, Q_BLOCK, Q_BLOCK), lambda b, q: (0, 0, 0, 0)),
        ],
        out_specs=pl.BlockSpec((Q_BLOCK, A_Q), lambda b, q: (b * nb + q, 0)),
        out_shape=jax.ShapeDtypeStruct((bsz * seq, A_Q), BF16),
        scratch_shapes=[
            pltpu.VMEM((nkb, Q_BLOCK, KEY_BLOCK), I32),
            pltpu.VMEM((nkb, Q_BLOCK, KEY_BLOCK), F32),
            pltpu.VMEM((IDX_HEADS, Q_BLOCK, LANES), F32),
            pltpu.VMEM((Q_BLOCK, LANES), I32),
            pltpu.VMEM((rows, A_HEAD_DIM), BF16),
            pltpu.VMEM((rows, LANES), BF16),
            pltpu.VMEM((rows, LANES), F32),
            pltpu.VMEM((rows, LANES), F32),
            pltpu.VMEM((rows, A_HEAD_DIM), F32),
        ],
        compiler_params=_cparams(("arbitrary", "arbitrary")),
        name="dsa",
    )(proj, proj, proj, kbd, kt, proj, tiles)


def _gla_kernel(q_ref, k_ref, v_ref, r_ref, sm_ref, wg_ref, bg_ref, nrm_ref, tri_ref, sel_ref,
                o_ref, st_ref, oacc_ref, *, nchunk):
    @pl.when(pl.program_id(1) == 0)
    def _():
        st_ref[...] = jnp.zeros(st_ref.shape, F32)

    ts = nchunk * CHUNK
    ri = lax.broadcasted_iota(I32, (ts, ts), 0)
    ci = lax.broadcasted_iota(I32, (ts, ts), 1)
    causal = jnp.logical_and(lax.shift_right_logical(ri, 6) == lax.shift_right_logical(ci, 6), ci <= ri)
    tn = (((0,), (0,)), ((), ()))
    nt = (((1,), (1,)), ((), ()))
    tri = tri_ref[...]
    sel = sel_ref[...]
    glr = sm_ref[:, GLR_LO:GLR_LO + GATE_RANK]

    heads = range(B_HEADS)
    ksl = [slice(h * B_HEAD_K, (h + 1) * B_HEAD_K) for h in heads]
    vsl = [slice(h * B_HEAD_V, (h + 1) * B_HEAD_V) for h in heads]
    qe, kd, dcol = [], [], []
    for h in heads:
        ks = ksl[h]
        z = jnp.dot(glr, wg_ref[:, ks], preferred_element_type=F32) + bg_ref[:, ks]
        g = (jnp.minimum(z, 0.0) - jnp.log(1.0 + jnp.exp(-jnp.abs(z)))) * (1.0 / GATE_TAU)
        g_hi = g.astype(BF16)
        g_lo = (g - g_hi.astype(F32)).astype(BF16)
        b = jnp.dot(tri, g_hi, preferred_element_type=F32) + jnp.dot(tri, g_lo, preferred_element_type=F32)
        bl = jnp.concatenate(
            [jnp.broadcast_to(b[(c + 1) * CHUNK - 1:(c + 1) * CHUNK, :], (CHUNK, B_HEAD_K)) for c in range(nchunk)],
            axis=0)
        qc = q_ref[:, ks].astype(F32) * (B_HEAD_K ** -0.5)
        kc = k_ref[:, ks].astype(F32)
        qe_h = (qc * jnp.exp(b)).astype(BF16)
        ke_h = (kc * jnp.exp(-b)).astype(BF16)
        qe.append(qe_h)
        kd.append((kc * jnp.exp(bl - b)).astype(BF16))
        a = lax.dot_general(qe_h, ke_h, nt, preferred_element_type=F32)
        a = jnp.where(causal, a, 0.0).astype(BF16)
        oacc_ref[:, vsl[h]] = jnp.dot(a, v_ref[:, vsl[h]], preferred_element_type=F32)
        dcol.append(jnp.exp(lax.dot_general(g_hi, sel, tn, preferred_element_type=F32)
                            + lax.dot_general(g_lo, sel, tn, preferred_element_type=F32)))

    st = [st_ref[h] for h in heads]
    for c in range(nchunk):
        rs = slice(c * CHUNK, (c + 1) * CHUNK)
        for h in heads:
            oacc_ref[rs, vsl[h]] += jnp.dot(qe[h][rs], st[h].astype(BF16), preferred_element_type=F32)
            dec = jnp.broadcast_to(dcol[h][:, c:c + 1], (B_HEAD_K, B_HEAD_V))
            st[h] = st[h] * dec + lax.dot_general(kd[h][rs], v_ref[rs, vsl[h]], tn, preferred_element_type=F32)
    for h in heads:
        st_ref[h] = st[h]
        o = oacc_ref[:, vsl[h]]
        ms = jnp.mean(o * o, axis=-1, keepdims=True)
        on = o * lax.rsqrt(ms + EPS) * nrm_ref[...]
        rr = r_ref[:, vsl[h]].astype(F32)
        o_ref[:, vsl[h]] = (on * (rr * jax.nn.sigmoid(rr))).astype(o_ref.dtype)


def _gla(proj, w_g2, b_g2, gla_norm, bsz, seq):
    q_blk = _NEW_OFF["g_q"] // B_QK
    k_blk = _NEW_OFF["g_k"] // B_QK
    v_blk = _NEW_OFF["g_v"] // B_V
    r_blk = _NEW_OFF["g_r"] // B_V
    ts = min(seq, GLA_SEQ_TILE)
    nt = seq // ts
    nchunk = ts // CHUNK
    kern = functools.partial(_gla_kernel, nchunk=nchunk)
    rc = np.arange(ts)[:, None] // CHUNK
    cc = np.arange(ts)[None, :] // CHUNK
    tri = jnp.asarray((rc == cc) & (np.arange(ts)[None, :] <= np.arange(ts)[:, None]), BF16)
    sel = jnp.asarray(rc == np.arange(LANES)[None, :], BF16)
    const = lambda shape: pl.BlockSpec(shape, lambda b, t: (0, 0))
    return pl.pallas_call(
        kern,
        grid=(bsz, nt),
        in_specs=[
            pl.BlockSpec((ts, B_QK), lambda b, t: (b * nt + t, q_blk)),
            pl.BlockSpec((ts, B_QK), lambda b, t: (b * nt + t, k_blk)),
            pl.BlockSpec((ts, B_V), lambda b, t: (b * nt + t, v_blk)),
            pl.BlockSpec((ts, B_V), lambda b, t: (b * nt + t, r_blk)),
            pl.BlockSpec((ts, LANES), lambda b, t: (b * nt + t, SMALL_OFF // LANES)),
            pl.BlockSpec((GATE_RANK, B_QK), lambda b, t: (0, 0)),
            pl.BlockSpec((1, B_QK), lambda b, t: (0, 0)),
            pl.BlockSpec((1, B_HEAD_V), lambda b, t: (0, 0)),
            const((ts, ts)), const((ts, LANES)),
        ],
        out_specs=pl.BlockSpec((ts, B_V), lambda b, t: (b * nt + t, 0)),
        out_shape=jax.ShapeDtypeStruct((bsz * seq, B_V), BF16),
        scratch_shapes=[pltpu.VMEM((B_HEADS, B_HEAD_K, B_HEAD_V), F32),
                        pltpu.VMEM((ts, B_V), F32)],
        compiler_params=_cparams(("arbitrary", "arbitrary")),
        name="gla",
    )(proj, proj, proj, proj, proj, w_g2.astype(BF16), b_g2.reshape(1, -1), gla_norm.reshape(1, -1),
      tri, sel)


def _outproj_kernel(oa_ref, ob_ref, w_ref, x_ref, modj_ref, mod_ref, g_ref, b_ref, x1_ref, h2_ref,
                    *, alpha, nj, tn):
    j = pl.program_id(1)
    a = jnp.concatenate([oa_ref[...], ob_ref[...]], axis=1)
    y = jnp.dot(a, w_ref[...], preferred_element_type=F32)
    z = alpha * x_ref[...] + modj_ref[2:3, :] * y
    for jj in range(nj):
        @pl.when(j == jj)
        def _(jj=jj):
            x1_ref[:, jj * tn:(jj + 1) * tn] = z

    @pl.when(j == nj - 1)
    def _():
        def rows(rs):
            x1 = _layer_norm(x1_ref[rs, :]) * g_ref[...] + b_ref[...]
            x1_ref[rs, :] = x1
            h2_ref[rs, :] = (_layer_norm(x1) * (1.0 + mod_ref[4:5, :]) + mod_ref[3:4, :]).astype(h2_ref.dtype)
        _for_row_chunks(x1_ref.shape[0], ROW_CHUNK, rows)


def _outproj(o_a, o_b, w, x2, mod3, ln_g, ln_b, seq, alpha):
    m, d = x2.shape
    tm, tn = 512, 1024
    nj = d // tn
    rows_per_seq = seq // tm
    kern = functools.partial(_outproj_kernel, alpha=alpha, nj=nj, tn=tn)
    return pl.pallas_call(
        kern,
        grid=(m // tm, nj),
        in_specs=[
            pl.BlockSpec((tm, A_Q), lambda i, j: (i, 0)),
            pl.BlockSpec((tm, B_V), lambda i, j: (i, 0)),
            pl.BlockSpec((A_Q + B_V, tn), lambda i, j: (0, j)),
            pl.BlockSpec((tm, tn), lambda i, j: (i, j)),
            pl.BlockSpec((None, 6, tn), lambda i, j: (i // rows_per_seq, 0, j)),
            pl.BlockSpec((None, 6, d), lambda i, j: (i // rows_per_seq, 0, 0)),
            pl.BlockSpec((1, d), lambda i, j: (0, 0)),
            pl.BlockSpec((1, d), lambda i, j: (0, 0)),
        ],
        out_specs=[pl.BlockSpec((tm, d), lambda i, j: (i, 0)),
                   pl.BlockSpec((tm, d), lambda i, j: (i, 0))],
        out_shape=[jax.ShapeDtypeStruct((m, d), F32), jax.ShapeDtypeStruct((m, d), BF16)],
        compiler_params=_cparams(("arbitrary", "arbitrary"), VMEM_LIMIT_FFN),
        name="outproj",
    )(o_a, o_b, w, x2, mod3, mod3, ln_g.reshape(1, d), ln_b.reshape(1, d))


def _ffn_kernel(h_ref, wu_ref, wg_ref, cw_ref, cb_ref, wd_ref, x1_ref, mod_ref, g_ref, b_ref, o_ref,
                carry_ref, *, alpha, nj, rows_per_seq):
    i = pl.program_id(0)
    j = pl.program_id(1)
    h = h_ref[...]
    tm = h.shape[0]
    u = jnp.dot(h, wu_ref[...], preferred_element_type=F32)
    gt = jnp.dot(h, wg_ref[...], preferred_element_type=F32)

    first = (i % rows_per_seq) == 0
    prev = jnp.where(first, 0.0, carry_ref[j])
    carry_ref[j] = u[tm - 8:tm, :]
    rid = lax.broadcasted_iota(I32, u.shape, 0)
    p1 = prev[7:8, :]
    p2 = prev[6:7, :]
    u1 = jnp.where(rid == 0, p1, pltpu.roll(u, 1, axis=0))
    u2 = jnp.where(rid == 0, p2, jnp.where(rid == 1, p1, pltpu.roll(u, 2, axis=0)))
    cv = cw_ref[0:1, :] * u2 + cw_ref[1:2, :] * u1 + cw_ref[2:3, :] * u + cb_ref[...]
    cdf = 0.5 * (1.0 + jnp.tanh(math.sqrt(2.0 / math.pi) * (cv + 0.044715 * (cv ** 3))))
    act = (cv * cdf * gt).astype(BF16)

    @pl.when(j == 0)
    def _():
        o_ref[...] = jnp.zeros(o_ref.shape, F32)

    for c0 in range(0, o_ref.shape[1], FFN_DOWN_CHUNK):
        cs = slice(c0, c0 + FFN_DOWN_CHUNK)
        o_ref[:, cs] += jnp.dot(act, wd_ref[:, cs], preferred_element_type=F32)

    @pl.when(j == nj - 1)
    def _():
        def rows(rs):
            z = alpha * x1_ref[rs, :] + mod_ref[5:6, :] * o_ref[rs, :]
            o_ref[rs, :] = _layer_norm(z) * g_ref[...] + b_ref[...]
        _for_row_chunks(o_ref.shape[0], ROW_CHUNK, rows)


def _ffn(h2, w_up, w_gate, conv_w, conv_b, w_down, x1, mod3, ln_g, ln_b, seq, alpha):
    m, d = h2.shape
    tm, tf = 512, FFN_TILE
    pad = -w_up.shape[1] % tf
    w_up, w_gate, conv_w = (jnp.pad(a, ((0, 0), (0, pad))) for a in (w_up, w_gate, conv_w))
    conv_b = jnp.pad(conv_b, (0, pad))
    w_down = jnp.pad(w_down, ((0, pad), (0, 0)))
    f = w_up.shape[1]
    nj = f // tf
    rows_per_seq = seq // tm
    kern = functools.partial(_ffn_kernel, alpha=alpha, nj=nj, rows_per_seq=rows_per_seq)
    once = pl.Buffered(1)
    return pl.pallas_call(
        kern,
        grid=(m // tm, nj),
        in_specs=[
            pl.BlockSpec((tm, d), lambda i, j: (i, 0), pipeline_mode=once),
            pl.BlockSpec((d, tf), lambda i, j: (0, j)),
            pl.BlockSpec((d, tf), lambda i, j: (0, j)),
            pl.BlockSpec((CONV_W, tf), lambda i, j: (0, j)),
            pl.BlockSpec((1, tf), lambda i, j: (0, j)),
            pl.BlockSpec((tf, d), lambda i, j: (j, 0)),
            pl.BlockSpec((tm, d), lambda i, j: (i, 0), pipeline_mode=once),
            pl.BlockSpec((None, 6, d), lambda i, j: (i // rows_per_seq, 0, 0)),
            pl.BlockSpec((1, d), lambda i, j: (0, 0)),
            pl.BlockSpec((1, d), lambda i, j: (0, 0)),
        ],
        out_specs=pl.BlockSpec((tm, d), lambda i, j: (i, 0)),
        out_shape=jax.ShapeDtypeStruct((m, d), F32),
        scratch_shapes=[pltpu.VMEM((nj, 8, tf), F32)],
        compiler_params=_cparams(("arbitrary", "arbitrary"), VMEM_LIMIT_FFN),
        name="ffn",
    )(h2, w_up, w_gate, conv_w, conv_b.reshape(1, f), w_down, x1, mod3, ln_g.reshape(1, d), ln_b.reshape(1, d))


def _regroup_w_in(w):
    d = w.shape[0]
    parts = [w[:, _OLD_OFF[n]:_OLD_OFF[n] + _WIDTH[n]] for n in PROJ_ORDER]
    parts.append(jnp.zeros((d, PROJ_COLS - _PROJ_USED), w.dtype))
    return jnp.concatenate(parts, axis=1).astype(BF16)


def kernel(x, c, t5_table, w_ada, b_ada, w_in, w_g2, b_g2, gla_norm, w_out, ln1_g, ln1_b, w_up, w_gate,
           conv_w, conv_b, w_down, ln2_g, ln2_b):
    bsz, seq, d = x.shape
    depth = w_ada.shape[0]
    alpha = (2 * depth) ** 0.25
    nkb = seq // KEY_BLOCK
    x2 = x.reshape(bsz * seq, d)
    tiles = _bias_tiles(t5_table)
    for l in range(depth):
        mod3 = _ada(c, w_ada[l], b_ada[l]).reshape(bsz, 6, d)
        proj = _inproj(x2, mod3, _regroup_w_in(w_in[l]), seq)

        p3 = proj.reshape(bsz, nkb, KEY_BLOCK, PROJ_COLS)
        ko, io = _NEW_OFF["a_k"], _NEW_OFF["i_k"]
        kt = jnp.swapaxes(p3[..., ko:ko + A_HEAD_DIM], 2, 3)
        kit = jnp.swapaxes(p3[..., io:io + IDX_DIM], 2, 3)
        zz = jnp.zeros_like(kit)
        kbd = jnp.concatenate([jnp.concatenate([kit, zz], axis=3),
                               jnp.concatenate([zz, kit], axis=3)], axis=2)

        o_a = _dsa(proj, kbd, kt, tiles, bsz, seq)
        o_b = _gla(proj, w_g2[l], b_g2[l], gla_norm[l], bsz, seq)
        x1, h2 = _outproj(o_a, o_b, w_out[l].astype(BF16), x2, mod3, ln1_g[l], ln1_b[l], seq, alpha)
        x2 = _ffn(h2, w_up[l].astype(BF16), w_gate[l].astype(BF16), conv_w[l], conv_b[l],
                  w_down[l].astype(BF16), x1, mod3, ln2_g[l], ln2_b[l], seq, alpha)
    return x2.reshape(bsz, seq, d)
```

```python
import functools
import math

import numpy as np
import jax
import jax.numpy as jnp
from jax import lax
from jax.experimental import pallas as pl
from jax.experimental.pallas import tpu as pltpu

F32 = jnp.float32
BF16 = jnp.bfloat16
I32 = jnp.int32

CHUNK = 64
Q_BLOCK = 128
A_HEADS = 16
A_HEAD_DIM = 128
IDX_HEADS = 32
IDX_DIM = 64
TOPK_MAX = 256
T5_BUCKETS = 32
T5_MAX_DIST = 128
B_HEADS = 4
B_HEAD_V = 512
B_HEAD_K = 256
GATE_RANK = 16
GATE_TAU = 16.0
CONV_W = 3
EPS = 1e-6

A_Q = A_HEADS * A_HEAD_DIM
IDX_Q = IDX_HEADS * IDX_DIM
B_QK = B_HEADS * B_HEAD_K
B_V = B_HEADS * B_HEAD_V
IN_SPLITS = (A_Q, A_HEAD_DIM, A_HEAD_DIM, IDX_Q, IDX_DIM, IDX_HEADS, B_QK, B_QK, B_V, GATE_RANK, B_V)
IN_NAMES = ("a_q", "a_k", "a_v", "i_q", "i_k", "i_w", "g_q", "g_k", "g_v", "g_lr", "g_r")
PROJ_ORDER = ("a_q", "i_q", "g_v", "g_r", "g_q", "g_k", "a_k", "a_v", "i_k", "i_w", "g_lr")
PROJ_TILE = 1536

LANES = 128
VMEM_LIMIT = 56 * 1024 * 1024
VMEM_LIMIT_FFN = 60 * 1024 * 1024

KEY_BLOCK = 2 * Q_BLOCK
NEG_MASK = -1e30
LOG2E = math.log2(math.e)
INT_MIN = -(2 ** 31)


def _proj_layout():
    old_off = dict(zip(IN_NAMES, np.concatenate([[0], np.cumsum(IN_SPLITS)[:-1]]).tolist()))
    width = dict(zip(IN_NAMES, IN_SPLITS))
    new_off, pos = {}, 0
    for name in PROJ_ORDER:
        new_off[name] = pos
        pos += width[name]
    total = -(-pos // PROJ_TILE) * PROJ_TILE
    return old_off, width, new_off, pos, total


_OLD_OFF, _WIDTH, _NEW_OFF, _PROJ_USED, PROJ_COLS = _proj_layout()
SMALL_OFF = _NEW_OFF["i_k"]
IK_LO = 0
IW_LO = _NEW_OFF["i_w"] - SMALL_OFF
GLR_LO = _NEW_OFF["g_lr"] - SMALL_OFF


def _layer_norm(x):
    mu = jnp.mean(x, axis=-1, keepdims=True)
    xc = x - mu
    var = jnp.mean(xc * xc, axis=-1, keepdims=True)
    return xc * lax.rsqrt(var + EPS)


def _for_row_chunks(nrows, chunk, fn):
    def body(r, carry):
        fn(pl.ds(pl.multiple_of(r * chunk, chunk), chunk))
        return carry
    lax.fori_loop(0, nrows // chunk, body, 0)


ROW_CHUNK = 64
FFN_TILE = 512
GLA_SEQ_TILE = 512
FFN_DOWN_CHUNK = 512


def _cparams(sem, vmem=VMEM_LIMIT):
    return pltpu.CompilerParams(dimension_semantics=sem, vmem_limit_bytes=vmem)


def _ada_kernel(c_ref, w_ref, b_ref, o_ref):
    c = c_ref[...]
    ca = (c * jax.nn.sigmoid(c)).astype(BF16)
    o_ref[...] = jnp.dot(ca, w_ref[...].astype(BF16), preferred_element_type=F32) + b_ref[...]


def _ada(c, w, b):
    bsz, d = c.shape
    n = w.shape[1]
    tn = 512
    return pl.pallas_call(
        _ada_kernel,
        grid=(n // tn,),
        in_specs=[pl.BlockSpec((bsz, d), lambda j: (0, 0)),
                  pl.BlockSpec((d, tn), lambda j: (0, j)),
                  pl.BlockSpec((1, tn), lambda j: (0, j))],
        out_specs=pl.BlockSpec((bsz, tn), lambda j: (0, j)),
        out_shape=jax.ShapeDtypeStruct((bsz, n), F32),
        compiler_params=_cparams(("arbitrary",)),
        name="ada",
    )(c, w, b.reshape(1, n))


def _inproj_kernel(x_ref, mod_ref, w_ref, o_ref, h_ref):
    @pl.when(pl.program_id(1) == 0)
    def _():
        def rows(rs):
            xn = _layer_norm(x_ref[rs, :])
            h_ref[rs, :] = (xn * (1.0 + mod_ref[1:2, :]) + mod_ref[0:1, :]).astype(BF16)
        _for_row_chunks(x_ref.shape[0], ROW_CHUNK, rows)

    o_ref[...] = jnp.dot(h_ref[...], w_ref[...], preferred_element_type=F32).astype(o_ref.dtype)


def _inproj(x2, mod3, w, seq):
    m, d = x2.shape
    n = w.shape[1]
    tm, tn = 512, PROJ_TILE
    rows_per_seq = seq // tm
    return pl.pallas_call(
        _inproj_kernel,
        grid=(m // tm, n // tn),
        in_specs=[pl.BlockSpec((tm, d), lambda i, j: (i, 0)),
                  pl.BlockSpec((None, 6, d), lambda i, j: (i // rows_per_seq, 0, 0)),
                  pl.BlockSpec((d, tn), lambda i, j: (0, j))],
        out_specs=pl.BlockSpec((tm, tn), lambda i, j: (i, j)),
        out_shape=jax.ShapeDtypeStruct((m, n), BF16),
        scratch_shapes=[pltpu.VMEM((tm, d), BF16)],
        compiler_params=_cparams(("arbitrary", "arbitrary")),
        name="inproj",
    )(x2, mod3, w)


def _t5_bucket(rel):
    half = T5_BUCKETS // 2
    max_exact = half // 2
    ret = jnp.where(rel > 0, half, 0)
    n = jnp.abs(rel)
    nf = jnp.maximum(n, 1).astype(jnp.float32)
    large = max_exact + (jnp.log(nf / max_exact) / math.log(T5_MAX_DIST / max_exact)
                         * (half - max_exact)).astype(jnp.int32)
    large = jnp.minimum(large, half - 1)
    return ret + jnp.where(n < max_exact, n, large)


def _bias_kernel(tab_ref, bkt_ref, o_ref):
    for u in range(3):
        bk = bkt_ref[u]
        for h in range(A_HEADS):
            acc = jnp.zeros((Q_BLOCK, Q_BLOCK), F32)
            for b in range(T5_BUCKETS):
                acc = jnp.where(bk == b, tab_ref[b, h], acc)
            o_ref[u, h] = acc * LOG2E


def _bias_tiles(t5_table):
    i = jnp.arange(Q_BLOCK, dtype=I32)[:, None]
    j = jnp.arange(Q_BLOCK, dtype=I32)[None, :]
    rel = jnp.stack([j - i - 2 * Q_BLOCK, j - i - Q_BLOCK, j - i])
    bkt = _t5_bucket(rel).astype(I32)
    return pl.pallas_call(
        _bias_kernel,
        in_specs=[pl.BlockSpec(memory_space=pltpu.SMEM),
                  pl.BlockSpec((3, Q_BLOCK, Q_BLOCK), lambda: (0, 0, 0))],
        out_specs=pl.BlockSpec((3, A_HEADS, Q_BLOCK, Q_BLOCK), lambda: (0, 0, 0, 0)),
        out_shape=jax.ShapeDtypeStruct((3, A_HEADS, Q_BLOCK, Q_BLOCK), F32),
        name="t5_bias",
    )(t5_table, bkt)


def _dsa_kernel(aq_ref, iq_ref, sm_ref, kbd_ref, kt_ref, v_ref, tiles_ref, o_ref,
                key_ref, nm_ref, wb_ref, t_ref, q_ref, lhs_ref, m_ref, l_ref, acc_ref, x_ref,
                *, nkb_max, topk):
    qb = pl.program_id(1)
    nkb = lax.shift_right_logical(qb + 2, 1)
    rows = A_HEADS * Q_BLOCK
    half = KEY_BLOCK // 2

    iw = sm_ref[:, IW_LO:IW_LO + IDX_HEADS].astype(F32) * (IDX_DIM ** -0.5 * IDX_HEADS ** -0.5)
    for hh in range(IDX_HEADS):
        wb_ref[hh] = jnp.broadcast_to(iw[:, hh:hh + 1], (Q_BLOCK, LANES))
    for p in range(IDX_HEADS // 2):
        lhs_ref[p * Q_BLOCK:(p + 1) * Q_BLOCK, :] = iq_ref[:, p * LANES:(p + 1) * LANES]
    for h in range(A_HEADS):
        q_ref[h * Q_BLOCK:(h + 1) * Q_BLOCK, :] = aq_ref[:, h * LANES:(h + 1) * LANES]

    row_chunk = lax.shift_right_logical(
        lax.broadcasted_iota(I32, (Q_BLOCK, KEY_BLOCK), 0) + qb * Q_BLOCK, 6)
    col_iota = lax.broadcasted_iota(I32, (Q_BLOCK, KEY_BLOCK), 1)

    def idx_body(kb, carry):
        res = jnp.dot(lhs_ref[...], kbd_ref[kb], preferred_element_type=F32)
        acc = jnp.zeros((Q_BLOCK, KEY_BLOCK), F32)
        for p in range(IDX_HEADS // 2):
            r = res[p * Q_BLOCK:(p + 1) * Q_BLOCK]
            we = wb_ref[2 * p]
            wo = wb_ref[2 * p + 1]
            acc = acc + jnp.maximum(r[:, :KEY_BLOCK], 0.0) * jnp.concatenate([we, we], axis=1)
            acc = acc + jnp.maximum(r[:, KEY_BLOCK:], 0.0) * jnp.concatenate([wo, wo], axis=1)
        bits = pltpu.bitcast(acc, I32)
        skey = bits ^ (lax.shift_right_arithmetic(bits, 31) & 0x7FFFFFFF)
        skey = jnp.where(acc == 0.0, 0, skey)
        adm = lax.shift_right_logical(col_iota + kb * KEY_BLOCK, 6) <= row_chunk
        key_ref[kb] = jnp.where(adm, skey, INT_MIN)
        return carry

    lax.fori_loop(0, nkb, idx_body, 0)

    def search(n):
        def count_ge(cand):
            c = jnp.zeros((Q_BLOCK, LANES), F32)
            for kb in range(n):
                k = key_ref[kb]
                c = c + jnp.where(k[:, :half] >= cand, 1.0, 0.0) + jnp.where(k[:, half:] >= cand, 1.0, 0.0)
            return jnp.broadcast_to(jnp.sum(c, axis=1, keepdims=True), (Q_BLOCK, LANES))

        zero = jnp.zeros((Q_BLOCK, LANES), I32)
        t0 = jnp.where(count_ge(zero) >= topk, zero, INT_MIN)

        def body(i, t):
            cand = t + lax.shift_left(jnp.int32(1), 30 - i)
            return jnp.where(count_ge(cand) >= topk, cand, t)

        t = lax.fori_loop(0, 31, body, t0)
        t_ref[...] = jnp.maximum(t, INT_MIN + 1)

    for n in range(1, nkb_max + 1):
        pl.when(nkb == n)(functools.partial(search, n))

    thr = t_ref[...]
    thr2 = jnp.concatenate([thr, thr], axis=1)

    def cnt_body(kb, c):
        g = jnp.where(key_ref[kb] >= thr2, 1.0, 0.0)
        return c + g[:, :half] + g[:, half:]

    n_ge = jnp.sum(lax.fori_loop(0, nkb, cnt_body, jnp.zeros((Q_BLOCK, LANES), F32)),
                   axis=1, keepdims=True)
    has_ties = jnp.max(n_ge) > topk

    @pl.when(jnp.logical_not(has_ties))
    def _():
        def body(kb, carry):
            nm_ref[kb] = jnp.where(key_ref[kb] >= thr2, 0.0, NEG_MASK)
            return carry
        lax.fori_loop(0, nkb, body, 0)

    @pl.when(has_ties)
    def _():
        def gt_body(kb, c):
            g = jnp.where(key_ref[kb] > thr2, 1.0, 0.0)
            return c + g[:, :half] + g[:, half:]
        n_gt = jnp.sum(lax.fori_loop(0, nkb, gt_body, jnp.zeros((Q_BLOCK, LANES), F32)),
                       axis=1, keepdims=True)
        need = topk - n_gt
        tri = (lax.broadcasted_iota(I32, (half, half), 0)
               <= lax.broadcasted_iota(I32, (half, half), 1)).astype(BF16)

        def body(kb, seen):
            k = key_ref[kb]
            parts = []
            for s in range(2):
                ks = k[:, s * half:(s + 1) * half]
                eq = ks == thr
                eqf = jnp.where(eq, 1.0, 0.0)
                rank = jnp.dot(eqf.astype(BF16), tri, preferred_element_type=F32) + seen
                keep = jnp.logical_or(ks > thr, jnp.logical_and(eq, rank <= need))
                parts.append(jnp.where(keep, 0.0, NEG_MASK))
                seen = seen + jnp.sum(eqf, axis=1, keepdims=True)
            nm_ref[kb] = jnp.concatenate(parts, axis=1)
            return seen
        lax.fori_loop(0, nkb, body, jnp.zeros((Q_BLOCK, 1), F32))

    scale = A_HEAD_DIM ** -0.5 * LOG2E

    m_ref[...] = jnp.full((rows, LANES), NEG_MASK, F32)
    l_ref[...] = jnp.zeros((rows, LANES), F32)
    acc_ref[...] = jnp.zeros((rows, A_HEAD_DIM), F32)

    def max_body(kb, carry):
        x = jnp.dot(q_ref[...], kt_ref[kb], preferred_element_type=F32) * scale
        g0 = 2 * kb
        g1 = g0 + 1
        u0 = jnp.where(g0 == qb, 2, jnp.where(g0 == qb - 1, 1, 0))
        u1 = jnp.where(g1 == qb, 2, jnp.where(g1 == qb - 1, 1, 0))
        bias = jnp.concatenate([tiles_ref[u0].reshape(rows, half),
                                tiles_ref[u1].reshape(rows, half)], axis=1)
        x = ((x + bias).reshape(A_HEADS, Q_BLOCK, KEY_BLOCK) + nm_ref[kb][None]).reshape(rows, KEY_BLOCK)
        x_ref[kb] = x
        m_ref[...] = jnp.maximum(m_ref[...], jnp.maximum(x[:, :half], x[:, half:]))
        return carry

    lax.fori_loop(0, nkb, max_body, 0)
    m_ref[...] = jnp.broadcast_to(jnp.max(m_ref[...], axis=1, keepdims=True), (rows, LANES))

    def pv_body(kb, carry):
        m = m_ref[...]
        p = jnp.exp2(x_ref[kb] - jnp.concatenate([m, m], axis=1))
        l_ref[...] += p[:, :half] + p[:, half:]
        vb = v_ref[pl.ds(pl.multiple_of(kb * KEY_BLOCK, KEY_BLOCK), KEY_BLOCK), :]
        acc_ref[...] += jnp.dot(p.astype(BF16), vb, preferred_element_type=F32)
        return carry

    lax.fori_loop(0, nkb, pv_body, 0)

    out = acc_ref[...] / jnp.sum(l_ref[...], axis=1, keepdims=True)
    for h in range(A_HEADS):
        o_ref[:, h * A_HEAD_DIM:(h + 1) * A_HEAD_DIM] = out[h * Q_BLOCK:(h + 1) * Q_BLOCK].astype(o_ref.dtype)


def _dsa(proj, kbd, kt, tiles, bsz, seq):
    nb = seq // Q_BLOCK
    nkb = seq // KEY_BLOCK
    topk = min(TOPK_MAX, seq // 4)
    rows = A_HEADS * Q_BLOCK
    ak_blk = _NEW_OFF["a_v"] // A_HEAD_DIM
    kern = functools.partial(_dsa_kernel, nkb_max=nkb, topk=topk)
    return pl.pallas_call(
        kern,
        grid=(bsz, nb),
        in_specs=[
            pl.BlockSpec((Q_BLOCK, A_Q), lambda b, q: (b * nb + q, _NEW_OFF["a_q"] // A_Q)),
            pl.BlockSpec((Q_BLOCK, IDX_Q), lambda b, q: (b * nb + q, _NEW_OFF["i_q"] // IDX_Q)),
            pl.BlockSpec((Q_BLOCK, LANES), lambda b, q: (b * nb + q, SMALL_OFF // LANES)),
            pl.BlockSpec((None, nkb, 2 * IDX_DIM, 2 * KEY_BLOCK), lambda b, q: (b, 0, 0, 0)),
            pl.BlockSpec((None, nkb, A_HEAD_DIM, KEY_BLOCK), lambda b, q: (b, 0, 0, 0)),
            pl.BlockSpec((seq, A_HEAD_DIM), lambda b, q: (b, ak_blk)),
            pl.BlockSpec((3, A_HEADS, Q_BLOCK, Q_BLOCK), lambda b, q: (0, 0, 0, 0)),
        ],
        out_specs=pl.BlockSpec((Q_BLOCK, A_Q), lambda b, q: (b * nb + q, 0)),
        out_shape=jax.ShapeDtypeStruct((bsz * seq, A_Q), BF16),
        scratch_shapes=[
            pltpu.VMEM((nkb, Q_BLOCK, KEY_BLOCK), I32),
            pltpu.VMEM((nkb, Q_BLOCK, KEY_BLOCK), F32),
            pltpu.VMEM((IDX_HEADS, Q_BLOCK, LANES), F32),
            pltpu.VMEM((Q_BLOCK, LANES), I32),
            pltpu.VMEM((rows, A_HEAD_DIM), BF16),
            pltpu.VMEM((rows, LANES), BF16),
            pltpu.VMEM((rows, LANES), F32),
            pltpu.VMEM((rows, LANES), F32),
            pltpu.VMEM((rows, A_HEAD_DIM), F32),
            pltpu.VMEM((nkb, rows, KEY_BLOCK), F32),
        ],
        compiler_params=_cparams(("arbitrary", "arbitrary")),
        name="dsa",
    )(proj, proj, proj, kbd, kt, proj, tiles)


def _gla_kernel(q_ref, k_ref, v_ref, r_ref, sm_ref, wg_ref, bg_ref, nrm_ref, tri_ref, sel_ref,
                o_ref, st_ref, oacc_ref, *, nchunk):
    @pl.when(pl.program_id(1) == 0)
    def _():
        st_ref[...] = jnp.zeros(st_ref.shape, F32)

    ts = nchunk * CHUNK
    ri = lax.broadcasted_iota(I32, (ts, ts), 0)
    ci = lax.broadcasted_iota(I32, (ts, ts), 1)
    causal = jnp.logical_and(lax.shift_right_logical(ri, 6) == lax.shift_right_logical(ci, 6), ci <= ri)
    tn = (((0,), (0,)), ((), ()))
    nt = (((1,), (1,)), ((), ()))
    tri = tri_ref[...]
    sel = sel_ref[...]
    glr = sm_ref[:, GLR_LO:GLR_LO + GATE_RANK]

    heads = range(B_HEADS)
    ksl = [slice(h * B_HEAD_K, (h + 1) * B_HEAD_K) for h in heads]
    vsl = [slice(h * B_HEAD_V, (h + 1) * B_HEAD_V) for h in heads]
    qe, kd, dcol = [], [], []
    for h in heads:
        ks = ksl[h]
        z = jnp.dot(glr, wg_ref[:, ks], preferred_element_type=F32) + bg_ref[:, ks]
        g = (jnp.minimum(z, 0.0) - jnp.log(1.0 + jnp.exp(-jnp.abs(z)))) * (1.0 / GATE_TAU)
        g_hi = g.astype(BF16)
        g_lo = (g - g_hi.astype(F32)).astype(BF16)
        b = jnp.dot(tri, g_hi, preferred_element_type=F32) + jnp.dot(tri, g_lo, preferred_element_type=F32)
        bl = jnp.concatenate(
            [jnp.broadcast_to(b[(c + 1) * CHUNK - 1:(c + 1) * CHUNK, :], (CHUNK, B_HEAD_K)) for c in range(nchunk)],
            axis=0)
        qc = q_ref[:, ks].astype(F32) * (B_HEAD_K ** -0.5)
        kc = k_ref[:, ks].astype(F32)
        qe_h = (qc * jnp.exp(b)).astype(BF16)
        ke_h = (kc * jnp.exp(-b)).astype(BF16)
        qe.append(qe_h)
        kd.append((kc * jnp.exp(bl - b)).astype(BF16))
        a = lax.dot_general(qe_h, ke_h, nt, preferred_element_type=F32)
        a = jnp.where(causal, a, 0.0).astype(BF16)
        oacc_ref[:, vsl[h]] = jnp.dot(a, v_ref[:, vsl[h]], preferred_element_type=F32)
        dcol.append(jnp.exp(lax.dot_general(g_hi, sel, tn, preferred_element_type=F32)
                            + lax.dot_general(g_lo, sel, tn, preferred_element_type=F32)))

    st = [st_ref[h] for h in heads]
    for c in range(nchunk):
        rs = slice(c * CHUNK, (c + 1) * CHUNK)
        for h in heads:
            oacc_ref[rs, vsl[h]] += jnp.dot(qe[h][rs], st[h].astype(BF16), preferred_element_type=F32)
            dec = jnp.broadcast_to(dcol[h][:, c:c + 1], (B_HEAD_K, B_HEAD_V))
            st[h] = st[h] * dec + lax.dot_general(kd[h][rs], v_ref[rs, vsl[h]], tn, preferred_element_type=F32)
    for h in heads:
        st_ref[h] = st[h]
        o = oacc_ref[:, vsl[h]]
        ms = jnp.mean(o * o, axis=-1, keepdims=True)
        on = o * lax.rsqrt(ms + EPS) * nrm_ref[...]
        rr = r_ref[:, vsl[h]].astype(F32)
        o_ref[:, vsl[h]] = (on * (rr * jax.nn.sigmoid(rr))).astype(o_ref.dtype)


def _gla(proj, w_g2, b_g2, gla_norm, bsz, seq):
    q_blk = _NEW_OFF["g_q"] // B_QK
    k_blk = _NEW_OFF["g_k"] // B_QK
    v_blk = _NEW_OFF["g_v"] // B_V
    r_blk = _NEW_OFF["g_r"] // B_V
    ts = min(seq, GLA_SEQ_TILE)
    nt = seq // ts
    nchunk = ts // CHUNK
    kern = functools.partial(_gla_kernel, nchunk=nchunk)
    rc = np.arange(ts)[:, None] // CHUNK
    cc = np.arange(ts)[None, :] // CHUNK
    tri = jnp.asarray((rc == cc) & (np.arange(ts)[None, :] <= np.arange(ts)[:, None]), BF16)
    sel = jnp.asarray(rc == np.arange(LANES)[None, :], BF16)
    const = lambda shape: pl.BlockSpec(shape, lambda b, t: (0, 0))
    return pl.pallas_call(
        kern,
        grid=(bsz, nt),
        in_specs=[
            pl.BlockSpec((ts, B_QK), lambda b, t: (b * nt + t, q_blk)),
            pl.BlockSpec((ts, B_QK), lambda b, t: (b * nt + t, k_blk)),
            pl.BlockSpec((ts, B_V), lambda b, t: (b * nt + t, v_blk)),
            pl.BlockSpec((ts, B_V), lambda b, t: (b * nt + t, r_blk)),
            pl.BlockSpec((ts, LANES), lambda b, t: (b * nt + t, SMALL_OFF // LANES)),
            pl.BlockSpec((GATE_RANK, B_QK), lambda b, t: (0, 0)),
            pl.BlockSpec((1, B_QK), lambda b, t: (0, 0)),
            pl.BlockSpec((1, B_HEAD_V), lambda b, t: (0, 0)),
            const((ts, ts)), const((ts, LANES)),
        ],
        out_specs=pl.BlockSpec((ts, B_V), lambda b, t: (b * nt + t, 0)),
        out_shape=jax.ShapeDtypeStruct((bsz * seq, B_V), BF16),
        scratch_shapes=[pltpu.VMEM((B_HEADS, B_HEAD_K, B_HEAD_V), F32),
                        pltpu.VMEM((ts, B_V), F32)],
        compiler_params=_cparams(("arbitrary", "arbitrary")),
        name="gla",
    )(proj, proj, proj, proj, proj, w_g2.astype(BF16), b_g2.reshape(1, -1), gla_norm.reshape(1, -1),
      tri, sel)


def _outproj_kernel(oa_ref, ob_ref, w_ref, x_ref, modj_ref, mod_ref, g_ref, b_ref, x1_ref, h2_ref,
                    *, alpha, nj, tn):
    j = pl.program_id(1)
    a = jnp.concatenate([oa_ref[...], ob_ref[...]], axis=1)
    y = jnp.dot(a, w_ref[...], preferred_element_type=F32)
    z = alpha * x_ref[...] + modj_ref[2:3, :] * y
    for jj in range(nj):
        @pl.when(j == jj)
        def _(jj=jj):
            x1_ref[:, jj * tn:(jj + 1) * tn] = z

    @pl.when(j == nj - 1)
    def _():
        def rows(rs):
            x1 = _layer_norm(x1_ref[rs, :]) * g_ref[...] + b_ref[...]
            x1_ref[rs, :] = x1
            h2_ref[rs, :] = (_layer_norm(x1) * (1.0 + mod_ref[4:5, :]) + mod_ref[3:4, :]).astype(h2_ref.dtype)
        _for_row_chunks(x1_ref.shape[0], ROW_CHUNK, rows)


def _outproj(o_a, o_b, w, x2, mod3, ln_g, ln_b, seq, alpha):
    m, d = x2.shape
    tm, tn = 512, 1024
    nj = d // tn
    rows_per_seq = seq // tm
    kern = functools.partial(_outproj_kernel, alpha=alpha, nj=nj, tn=tn)
    return pl.pallas_call(
        kern,
        grid=(m // tm, nj),
        in_specs=[
            pl.BlockSpec((tm, A_Q), lambda i, j: (i, 0)),
            pl.BlockSpec((tm, B_V), lambda i, j: (i, 0)),
            pl.BlockSpec((A_Q + B_V, tn), lambda i, j: (0, j)),
            pl.BlockSpec((tm, tn), lambda i, j: (i, j)),
            pl.BlockSpec((None, 6, tn), lambda i, j: (i // rows_per_seq, 0, j)),
            pl.BlockSpec((None, 6, d), lambda i, j: (i // rows_per_seq, 0, 0)),
            pl.BlockSpec((1, d), lambda i, j: (0, 0)),
            pl.BlockSpec((1, d), lambda i, j: (0, 0)),
        ],
        out_specs=[pl.BlockSpec((tm, d), lambda i, j: (i, 0)),
                   pl.BlockSpec((tm, d), lambda i, j: (i, 0))],
        out_shape=[jax.ShapeDtypeStruct((m, d), F32), jax.ShapeDtypeStruct((m, d), BF16)],
        compiler_params=_cparams(("arbitrary", "arbitrary"), VMEM_LIMIT_FFN),
        name="outproj",
    )(o_a, o_b, w, x2, mod3, mod3, ln_g.reshape(1, d), ln_b.reshape(1, d))


def _ffn_kernel(h_ref, wu_ref, wg_ref, cw_ref, cb_ref, wd_ref, x1_ref, mod_ref, g_ref, b_ref, o_ref,
                carry_ref, act_ref, *, alpha, nj, rows_per_seq):
    i = pl.program_id(0)
    j = pl.program_id(1)

    def up_gate():
        h = h_ref[...]
        tm = h.shape[0]
        u = jnp.dot(h, wu_ref[...], preferred_element_type=F32)
        gt = jnp.dot(h, wg_ref[...], preferred_element_type=F32)
        first = (i % rows_per_seq) == 0
        prev = jnp.where(first, 0.0, carry_ref[j])
        carry_ref[j] = u[tm - 8:tm, :]
        rid = lax.broadcasted_iota(I32, u.shape, 0)
        p1 = prev[7:8, :]
        p2 = prev[6:7, :]
        u1 = jnp.where(rid == 0, p1, pltpu.roll(u, 1, axis=0))
        u2 = jnp.where(rid == 0, p2, jnp.where(rid == 1, p1, pltpu.roll(u, 2, axis=0)))
        cv = cw_ref[0:1, :] * u2 + cw_ref[1:2, :] * u1 + cw_ref[2:3, :] * u + cb_ref[...]
        cdf = 0.5 * (1.0 + jnp.tanh(math.sqrt(2.0 / math.pi) * (cv + 0.044715 * (cv ** 3))))
        return (cv * cdf * gt).astype(BF16)

    def down(act):
        for c0 in range(0, o_ref.shape[1], FFN_DOWN_CHUNK):
            cs = slice(c0, c0 + FFN_DOWN_CHUNK)
            o_ref[:, cs] += jnp.dot(act, wd_ref[:, cs], preferred_element_type=F32)

    @pl.when(j == 0)
    def _():
        o_ref[...] = jnp.zeros(o_ref.shape, F32)
        act_ref[...] = up_gate()

    @pl.when(jnp.logical_and(j > 0, j < nj))
    def _():
        act_prev = act_ref[...]
        act_ref[...] = up_gate()
        down(act_prev)

    @pl.when(j == nj)
    def _():
        down(act_ref[...])

        def rows(rs):
            z = alpha * x1_ref[rs, :] + mod_ref[5:6, :] * o_ref[rs, :]
            o_ref[rs, :] = _layer_norm(z) * g_ref[...] + b_ref[...]
        _for_row_chunks(o_ref.shape[0], ROW_CHUNK, rows)


def _ffn(h2, w_up, w_gate, conv_w, conv_b, w_down, x1, mod3, ln_g, ln_b, seq, alpha):
    m, d = h2.shape
    tm, tf = 512, FFN_TILE
    pad = -w_up.shape[1] % tf
    w_up, w_gate = (jnp.pad(a, ((0, 0), (0, pad))).astype(BF16) for a in (w_up, w_gate))
    conv_w = jnp.pad(conv_w, ((0, 0), (0, pad)))
    conv_b = jnp.pad(conv_b, (0, pad))
    w_down = jnp.pad(w_down, ((0, pad), (0, 0))).astype(BF16)
    f = w_up.shape[1]
    nj = f // tf
    rows_per_seq = seq // tm
    kern = functools.partial(_ffn_kernel, alpha=alpha, nj=nj, rows_per_seq=rows_per_seq)
    once = pl.Buffered(1)
    up_tile = lambda i, j: (0, jnp.minimum(j, nj - 1))
    return pl.pallas_call(
        kern,
        grid=(m // tm, nj + 1),
        in_specs=[
            pl.BlockSpec((tm, d), lambda i, j: (i, 0), pipeline_mode=once),
            pl.BlockSpec((d, tf), up_tile),
            pl.BlockSpec((d, tf), up_tile),
            pl.BlockSpec((CONV_W, tf), up_tile),
            pl.BlockSpec((1, tf), up_tile),
            pl.BlockSpec((tf, d), lambda i, j: (jnp.maximum(j - 1, 0), 0)),
            pl.BlockSpec((tm, d), lambda i, j: (i, 0), pipeline_mode=once),
            pl.BlockSpec((None, 6, d), lambda i, j: (i // rows_per_seq, 0, 0)),
            pl.BlockSpec((1, d), lambda i, j: (0, 0)),
            pl.BlockSpec((1, d), lambda i, j: (0, 0)),
        ],
        out_specs=pl.BlockSpec((tm, d), lambda i, j: (i, 0)),
        out_shape=jax.ShapeDtypeStruct((m, d), F32),
        scratch_shapes=[pltpu.VMEM((nj, 8, tf), F32),
                        pltpu.VMEM((tm, tf), BF16)],
        compiler_params=_cparams(("arbitrary", "arbitrary"), VMEM_LIMIT_FFN),
        name="ffn",
    )(h2, w_up, w_gate, conv_w, conv_b.reshape(1, f), w_down, x1, mod3, ln_g.reshape(1, d), ln_b.reshape(1, d))


def _regroup_kernel(w_ref, o_ref):
    for n in PROJ_ORDER:
        o_ref[:, _NEW_OFF[n]:_NEW_OFF[n] + _WIDTH[n]] = w_ref[:, _OLD_OFF[n]:_OLD_OFF[n] + _WIDTH[n]].astype(BF16)
    o_ref[:, _PROJ_USED:] = jnp.zeros((o_ref.shape[0], PROJ_COLS - _PROJ_USED), BF16)


def _regroup_w_in(w):
    d, n = w.shape
    tk = 256
    return pl.pallas_call(
        _regroup_kernel,
        grid=(d // tk,),
        in_specs=[pl.BlockSpec((tk, n), lambda i: (i, 0))],
        out_specs=pl.BlockSpec((tk, PROJ_COLS), lambda i: (i, 0)),
        out_shape=jax.ShapeDtypeStruct((d, PROJ_COLS), BF16),
        compiler_params=_cparams(("arbitrary",)),
        name="regroup_w_in",
    )(w)


def kernel(x, c, t5_table, w_ada, b_ada, w_in, w_g2, b_g2, gla_norm, w_out, ln1_g, ln1_b, w_up, w_gate,
           conv_w, conv_b, w_down, ln2_g, ln2_b):
    bsz, seq, d = x.shape
    depth = w_ada.shape[0]
    alpha = (2 * depth) ** 0.25
    nkb = seq // KEY_BLOCK
    x2 = x.reshape(bsz * seq, d)
    tiles = _bias_tiles(t5_table)
    for l in range(depth):
        mod3 = _ada(c, w_ada[l], b_ada[l]).reshape(bsz, 6, d)
        proj = _inproj(x2, mod3, _regroup_w_in(w_in[l]), seq)

        p3 = proj.reshape(bsz, nkb, KEY_BLOCK, PROJ_COLS)
        ko, io = _NEW_OFF["a_k"], _NEW_OFF["i_k"]
        kt = jnp.swapaxes(p3[..., ko:ko + A_HEAD_DIM], 2, 3)
        kit = jnp.swapaxes(p3[..., io:io + IDX_DIM], 2, 3)
        zz = jnp.zeros_like(kit)
        kbd = jnp.concatenate([jnp.concatenate([kit, zz], axis=3),
                               jnp.concatenate([zz, kit], axis=3)], axis=2)

        o_a = _dsa(proj, kbd, kt, tiles, bsz, seq)
        o_b = _gla(proj, w_g2[l], b_g2[l], gla_norm[l], bsz, seq)
        x1, h2 = _outproj(o_a, o_b, w_out[l].astype(BF16), x2, mod3, ln1_g[l], ln1_b[l], seq, alpha)
        x2 = _ffn(h2, w_up[l], w_gate[l], conv_w[l], conv_b[l], w_down[l], x1, mod3, ln2_g[l], ln2_b[l],
                  seq, alpha)
    return x2.reshape(bsz, seq, d)
```

```python
import functools
import math

import numpy as np
import jax
import jax.numpy as jnp
from jax import lax
from jax.experimental import pallas as pl
from jax.experimental.pallas import tpu as pltpu

F32 = jnp.float32
BF16 = jnp.bfloat16
I32 = jnp.int32

CHUNK = 64
Q_BLOCK = 128
A_HEADS = 16
A_HEAD_DIM = 128
IDX_HEADS = 32
IDX_DIM = 64
TOPK_MAX = 256
T5_BUCKETS = 32
T5_MAX_DIST = 128
B_HEADS = 4
B_HEAD_V = 512
B_HEAD_K = 256
GATE_RANK = 16
GATE_TAU = 16.0
CONV_W = 3
EPS = 1e-6

A_Q = A_HEADS * A_HEAD_DIM
IDX_Q = IDX_HEADS * IDX_DIM
B_QK = B_HEADS * B_HEAD_K
B_V = B_HEADS * B_HEAD_V
IN_SPLITS = (A_Q, A_HEAD_DIM, A_HEAD_DIM, IDX_Q, IDX_DIM, IDX_HEADS, B_QK, B_QK, B_V, GATE_RANK, B_V)
IN_NAMES = ("a_q", "a_k", "a_v", "i_q", "i_k", "i_w", "g_q", "g_k", "g_v", "g_lr", "g_r")
PROJ_ORDER = ("a_q", "i_q", "g_v", "g_r", "g_q", "g_k", "a_k", "a_v", "i_k", "i_w", "g_lr")
PROJ_TILE = 1536

LANES = 128
VMEM_LIMIT = 56 * 1024 * 1024
VMEM_LIMIT_FFN = 60 * 1024 * 1024

KEY_BLOCK = 2 * Q_BLOCK
NEG_MASK = -1e30
LOG2E = math.log2(math.e)
INT_MIN = -(2 ** 31)


def _proj_layout():
    old_off = dict(zip(IN_NAMES, np.concatenate([[0], np.cumsum(IN_SPLITS)[:-1]]).tolist()))
    width = dict(zip(IN_NAMES, IN_SPLITS))
    new_off, pos = {}, 0
    for name in PROJ_ORDER:
        new_off[name] = pos
        pos += width[name]
    total = -(-pos // PROJ_TILE) * PROJ_TILE
    return old_off, width, new_off, pos, total


_OLD_OFF, _WIDTH, _NEW_OFF, _PROJ_USED, PROJ_COLS = _proj_layout()
SMALL_OFF = _NEW_OFF["i_k"]
IK_LO = 0
IW_LO = _NEW_OFF["i_w"] - SMALL_OFF
GLR_LO = _NEW_OFF["g_lr"] - SMALL_OFF


def _layer_norm(x):
    mu = jnp.mean(x, axis=-1, keepdims=True)
    xc = x - mu
    var = jnp.mean(xc * xc, axis=-1, keepdims=True)
    return xc * lax.rsqrt(var + EPS)


def _for_row_chunks(nrows, chunk, fn):
    def body(r, carry):
        fn(pl.ds(pl.multiple_of(r * chunk, chunk), chunk))
        return carry
    lax.fori_loop(0, nrows // chunk, body, 0)


ROW_CHUNK = 64
FFN_TILE = 512
GLA_SEQ_TILE = 512
FFN_DOWN_CHUNK = 512


def _cparams(sem, vmem=VMEM_LIMIT):
    return pltpu.CompilerParams(dimension_semantics=sem, vmem_limit_bytes=vmem)


def _ada_kernel(c_ref, w_ref, b_ref, o_ref):
    c = c_ref[...]
    ca = (c * jax.nn.sigmoid(c)).astype(BF16)
    o_ref[...] = jnp.dot(ca, w_ref[...].astype(BF16), preferred_element_type=F32) + b_ref[...]


def _ada(c, w, b):
    bsz, d = c.shape
    n = w.shape[1]
    tn = 512
    return pl.pallas_call(
        _ada_kernel,
        grid=(n // tn,),
        in_specs=[pl.BlockSpec((bsz, d), lambda j: (0, 0)),
                  pl.BlockSpec((d, tn), lambda j: (0, j)),
                  pl.BlockSpec((1, tn), lambda j: (0, j))],
        out_specs=pl.BlockSpec((bsz, tn), lambda j: (0, j)),
        out_shape=jax.ShapeDtypeStruct((bsz, n), F32),
        compiler_params=_cparams(("arbitrary",)),
        name="ada",
    )(c, w, b.reshape(1, n))


def _inproj_kernel(x_ref, mod_ref, w_ref, o_ref, h_ref):
    @pl.when(pl.program_id(1) == 0)
    def _():
        def rows(rs):
            xn = _layer_norm(x_ref[rs, :])
            h_ref[rs, :] = (xn * (1.0 + mod_ref[1:2, :]) + mod_ref[0:1, :]).astype(BF16)
        _for_row_chunks(x_ref.shape[0], ROW_CHUNK, rows)

    o_ref[...] = jnp.dot(h_ref[...], w_ref[...], preferred_element_type=F32).astype(o_ref.dtype)


def _inproj(x2, mod3, w, seq):
    m, d = x2.shape
    n = w.shape[1]
    tm, tn = 512, PROJ_TILE
    rows_per_seq = seq // tm
    return pl.pallas_call(
        _inproj_kernel,
        grid=(m // tm, n // tn),
        in_specs=[pl.BlockSpec((tm, d), lambda i, j: (i, 0)),
                  pl.BlockSpec((None, 6, d), lambda i, j: (i // rows_per_seq, 0, 0)),
                  pl.BlockSpec((d, tn), lambda i, j: (0, j))],
        out_specs=pl.BlockSpec((tm, tn), lambda i, j: (i, j)),
        out_shape=jax.ShapeDtypeStruct((m, n), BF16),
        scratch_shapes=[pltpu.VMEM((tm, d), BF16)],
        compiler_params=_cparams(("arbitrary", "arbitrary")),
        name="inproj",
    )(x2, mod3, w)


def _t5_bucket(rel):
    half = T5_BUCKETS // 2
    max_exact = half // 2
    ret = jnp.where(rel > 0, half, 0)
    n = jnp.abs(rel)
    nf = jnp.maximum(n, 1).astype(jnp.float32)
    large = max_exact + (jnp.log(nf / max_exact) / math.log(T5_MAX_DIST / max_exact)
                         * (half - max_exact)).astype(jnp.int32)
    large = jnp.minimum(large, half - 1)
    return ret + jnp.where(n < max_exact, n, large)


def _bias_kernel(tab_ref, bkt_ref, o_ref):
    for u in range(3):
        bk = bkt_ref[u]
        for h in range(A_HEADS):
            acc = jnp.zeros((Q_BLOCK, Q_BLOCK), F32)
            for b in range(T5_BUCKETS):
                acc = jnp.where(bk == b, tab_ref[b, h], acc)
            o_ref[u, h] = acc * LOG2E


def _bias_tiles(t5_table):
    i = jnp.arange(Q_BLOCK, dtype=I32)[:, None]
    j = jnp.arange(Q_BLOCK, dtype=I32)[None, :]
    rel = jnp.stack([j - i - 2 * Q_BLOCK, j - i - Q_BLOCK, j - i])
    bkt = _t5_bucket(rel).astype(I32)
    return pl.pallas_call(
        _bias_kernel,
        in_specs=[pl.BlockSpec(memory_space=pltpu.SMEM),
                  pl.BlockSpec((3, Q_BLOCK, Q_BLOCK), lambda: (0, 0, 0))],
        out_specs=pl.BlockSpec((3, A_HEADS, Q_BLOCK, Q_BLOCK), lambda: (0, 0, 0, 0)),
        out_shape=jax.ShapeDtypeStruct((3, A_HEADS, Q_BLOCK, Q_BLOCK), F32),
        name="t5_bias",
    )(t5_table, bkt)


def _dsa_kernel(aq_ref, iq_ref, sm_ref, kbd_ref, kt_ref, v_ref, tiles_ref, exp_ref, o_ref,
                key_ref, nm_ref, wb_ref, t_ref, q_ref, lhs_ref, m_ref, l_ref, acc_ref, x_ref,
                *, nkb_max, topk):
    qb = pl.program_id(1)
    nkb = lax.shift_right_logical(qb + 2, 1)
    rows = A_HEADS * Q_BLOCK
    half = KEY_BLOCK // 2

    iw = jnp.dot(sm_ref[:, IW_LO:IW_LO + IDX_HEADS], exp_ref[...], preferred_element_type=F32)
    for hh in range(IDX_HEADS):
        wb_ref[hh] = iw[:, hh * LANES:(hh + 1) * LANES] * (IDX_DIM ** -0.5 * IDX_HEADS ** -0.5)
    for p in range(IDX_HEADS // 2):
        lhs_ref[p * Q_BLOCK:(p + 1) * Q_BLOCK, :] = iq_ref[:, p * LANES:(p + 1) * LANES]
    for h in range(A_HEADS):
        q_ref[h * Q_BLOCK:(h + 1) * Q_BLOCK, :] = aq_ref[:, h * LANES:(h + 1) * LANES]

    row_chunk = lax.shift_right_logical(
        lax.broadcasted_iota(I32, (Q_BLOCK, KEY_BLOCK), 0) + qb * Q_BLOCK, 6)
    col_iota = lax.broadcasted_iota(I32, (Q_BLOCK, KEY_BLOCK), 1)

    def idx_body(kb, carry):
        res = jnp.dot(lhs_ref[...], kbd_ref[kb], preferred_element_type=F32)
        acc = jnp.zeros((Q_BLOCK, KEY_BLOCK), F32)
        for p in range(IDX_HEADS // 2):
            r = res[p * Q_BLOCK:(p + 1) * Q_BLOCK]
            we = wb_ref[2 * p]
            wo = wb_ref[2 * p + 1]
            acc = acc + jnp.maximum(r[:, :KEY_BLOCK], 0.0) * jnp.concatenate([we, we], axis=1)
            acc = acc + jnp.maximum(r[:, KEY_BLOCK:], 0.0) * jnp.concatenate([wo, wo], axis=1)
        bits = pltpu.bitcast(acc, I32)
        skey = bits ^ (lax.shift_right_arithmetic(bits, 31) & 0x7FFFFFFF)
        skey = jnp.where(acc == 0.0, 0, skey)
        adm = lax.shift_right_logical(col_iota + kb * KEY_BLOCK, 6) <= row_chunk
        key_ref[kb] = jnp.where(adm, skey, INT_MIN)
        return carry

    lax.fori_loop(0, nkb, idx_body, 0)

    def search(n):
        def count_ge(cand):
            c = jnp.zeros((Q_BLOCK, LANES), F32)
            for kb in range(n):
                k = key_ref[kb]
                c = c + jnp.where(k[:, :half] >= cand, 1.0, 0.0) + jnp.where(k[:, half:] >= cand, 1.0, 0.0)
            return jnp.broadcast_to(jnp.sum(c, axis=1, keepdims=True), (Q_BLOCK, LANES))

        zero = jnp.zeros((Q_BLOCK, LANES), I32)
        t0 = jnp.where(count_ge(zero) >= topk, zero, INT_MIN)

        def body(i, t):
            cand = t + lax.shift_left(jnp.int32(1), 30 - i)
            return jnp.where(count_ge(cand) >= topk, cand, t)

        t = lax.fori_loop(0, 31, body, t0)
        t_ref[...] = jnp.maximum(t, INT_MIN + 1)

    all_selected = (qb + 1) * Q_BLOCK <= topk

    @pl.when(all_selected)
    def _():
        t_ref[...] = jnp.full((Q_BLOCK, LANES), INT_MIN + 1, I32)

    for n in range(1, nkb_max + 1):
        pl.when(jnp.logical_and(nkb == n, jnp.logical_not(all_selected)))(functools.partial(search, n))

    thr = t_ref[...]
    thr2 = jnp.concatenate([thr, thr], axis=1)

    def cnt_body(kb, c):
        g = jnp.where(key_ref[kb] >= thr2, 1.0, 0.0)
        return c + g[:, :half] + g[:, half:]

    n_ge = jnp.sum(lax.fori_loop(0, nkb, cnt_body, jnp.zeros((Q_BLOCK, LANES), F32)),
                   axis=1, keepdims=True)
    has_ties = jnp.max(n_ge) > topk

    @pl.when(jnp.logical_not(has_ties))
    def _():
        def body(kb, carry):
            nm_ref[kb] = jnp.where(key_ref[kb] >= thr2, 0.0, NEG_MASK)
            return carry
        lax.fori_loop(0, nkb, body, 0)

    @pl.when(has_ties)
    def _():
        def gt_body(kb, c):
            g = jnp.where(key_ref[kb] > thr2, 1.0, 0.0)
            return c + g[:, :half] + g[:, half:]
        n_gt = jnp.sum(lax.fori_loop(0, nkb, gt_body, jnp.zeros((Q_BLOCK, LANES), F32)),
                       axis=1, keepdims=True)
        need = topk - n_gt
        tri = (lax.broadcasted_iota(I32, (half, half), 0)
               <= lax.broadcasted_iota(I32, (half, half), 1)).astype(BF16)

        def body(kb, seen):
            k = key_ref[kb]
            parts = []
            for s in range(2):
                ks = k[:, s * half:(s + 1) * half]
                eq = ks == thr
                eqf = jnp.where(eq, 1.0, 0.0)
                rank = jnp.dot(eqf.astype(BF16), tri, preferred_element_type=F32) + seen
                keep = jnp.logical_or(ks > thr, jnp.logical_and(eq, rank <= need))
                parts.append(jnp.where(keep, 0.0, NEG_MASK))
                seen = seen + jnp.sum(eqf, axis=1, keepdims=True)
            nm_ref[kb] = jnp.concatenate(parts, axis=1)
            return seen
        lax.fori_loop(0, nkb, body, jnp.zeros((Q_BLOCK, 1), F32))

    scale = A_HEAD_DIM ** -0.5 * LOG2E

    m_ref[...] = jnp.full((rows, LANES), NEG_MASK, F32)
    l_ref[...] = jnp.zeros((rows, LANES), F32)
    acc_ref[...] = jnp.zeros((rows, A_HEAD_DIM), F32)

    def max_body(kb, carry):
        x = jnp.dot(q_ref[...], kt_ref[kb], preferred_element_type=F32) * scale
        g0 = 2 * kb
        g1 = g0 + 1
        u0 = jnp.where(g0 == qb, 2, jnp.where(g0 == qb - 1, 1, 0))
        u1 = jnp.where(g1 == qb, 2, jnp.where(g1 == qb - 1, 1, 0))
        bias = jnp.concatenate([tiles_ref[u0].reshape(rows, half),
                                tiles_ref[u1].reshape(rows, half)], axis=1)
        x = ((x + bias).reshape(A_HEADS, Q_BLOCK, KEY_BLOCK) + nm_ref[kb][None]).reshape(rows, KEY_BLOCK)
        x_ref[kb] = x
        m_ref[...] = jnp.maximum(m_ref[...], jnp.maximum(x[:, :half], x[:, half:]))
        return carry

    lax.fori_loop(0, nkb, max_body, 0)
    m_ref[...] = jnp.broadcast_to(jnp.max(m_ref[...], axis=1, keepdims=True), (rows, LANES))

    def pv_body(kb, carry):
        m = m_ref[...]
        p = jnp.exp2(x_ref[kb] - jnp.concatenate([m, m], axis=1))
        l_ref[...] += p[:, :half] + p[:, half:]
        vb = v_ref[pl.ds(pl.multiple_of(kb * KEY_BLOCK, KEY_BLOCK), KEY_BLOCK), :]
        acc_ref[...] += jnp.dot(p.astype(BF16), vb, preferred_element_type=F32)
        return carry

    lax.fori_loop(0, nkb, pv_body, 0)

    out = acc_ref[...] / jnp.sum(l_ref[...], axis=1, keepdims=True)
    for h in range(A_HEADS):
        o_ref[:, h * A_HEAD_DIM:(h + 1) * A_HEAD_DIM] = out[h * Q_BLOCK:(h + 1) * Q_BLOCK].astype(o_ref.dtype)


def _dsa(proj, kbd, kt, tiles, bsz, seq):
    nb = seq // Q_BLOCK
    nkb = seq // KEY_BLOCK
    topk = min(TOPK_MAX, seq // 4)
    rows = A_HEADS * Q_BLOCK
    av_blk = _NEW_OFF["a_v"] // A_HEAD_DIM
    kern = functools.partial(_dsa_kernel, nkb_max=nkb, topk=topk)
    expand = jnp.asarray(np.arange(IDX_HEADS)[:, None] == np.arange(IDX_HEADS * LANES)[None, :] // LANES, BF16)
    return pl.pallas_call(
        kern,
        grid=(bsz, nb),
        in_specs=[
            pl.BlockSpec((Q_BLOCK, A_Q), lambda b, q: (b * nb + q, _NEW_OFF["a_q"] // A_Q)),
            pl.BlockSpec((Q_BLOCK, IDX_Q), lambda b, q: (b * nb + q, _NEW_OFF["i_q"] // IDX_Q)),
            pl.BlockSpec((Q_BLOCK, LANES), lambda b, q: (b * nb + q, SMALL_OFF // LANES)),
            pl.BlockSpec((None, nkb, 2 * IDX_DIM, 2 * KEY_BLOCK), lambda b, q: (b, 0, 0, 0)),
            pl.BlockSpec((None, nkb, A_HEAD_DIM, KEY_BLOCK), lambda b, q: (b, 0, 0, 0)),
            pl.BlockSpec((seq, A_HEAD_DIM), lambda b, q: (b, av_blk)),
            pl.BlockSpec((3, A_HEADS, Q_BLOCK, Q_BLOCK), lambda b, q: (0, 0, 0, 0)),
            pl.BlockSpec((IDX_HEADS, IDX_HEADS * LANES), lambda b, q: (0, 0)),
        ],
        out_specs=pl.BlockSpec((Q_BLOCK, A_Q), lambda b, q: (b * nb + q, 0)),
        out_shape=jax.ShapeDtypeStruct((bsz * seq, A_Q), BF16),
        scratch_shapes=[
            pltpu.VMEM((nkb, Q_BLOCK, KEY_BLOCK), I32),
            pltpu.VMEM((nkb, Q_BLOCK, KEY_BLOCK), F32),
            pltpu.VMEM((IDX_HEADS, Q_BLOCK, LANES), F32),
            pltpu.VMEM((Q_BLOCK, LANES), I32),
            pltpu.VMEM((rows, A_HEAD_DIM), BF16),
            pltpu.VMEM((rows, LANES), BF16),
            pltpu.VMEM((rows, LANES), F32),
            pltpu.VMEM((rows, LANES), F32),
            pltpu.VMEM((rows, A_HEAD_DIM), F32),
            pltpu.VMEM((nkb, rows, KEY_BLOCK), F32),
        ],
        compiler_params=_cparams(("arbitrary", "arbitrary")),
        name="dsa",
    )(proj, proj, proj, kbd, kt, proj, tiles, expand)


def _gla_kernel(q_ref, k_ref, v_ref, r_ref, sm_ref, wg_ref, bg_ref, nrm_ref, tri_ref, sel_ref,
                o_ref, st_ref, oacc_ref, *, nchunk):
    @pl.when(pl.program_id(1) == 0)
    def _():
        st_ref[...] = jnp.zeros(st_ref.shape, F32)

    ts = nchunk * CHUNK
    ri = lax.broadcasted_iota(I32, (ts, ts), 0)
    ci = lax.broadcasted_iota(I32, (ts, ts), 1)
    causal = jnp.logical_and(lax.shift_right_logical(ri, 6) == lax.shift_right_logical(ci, 6), ci <= ri)
    tn = (((0,), (0,)), ((), ()))
    nt = (((1,), (1,)), ((), ()))
    tri = tri_ref[...]
    sel = sel_ref[...]
    glr = sm_ref[:, GLR_LO:GLR_LO + GATE_RANK]

    heads = range(B_HEADS)
    ksl = [slice(h * B_HEAD_K, (h + 1) * B_HEAD_K) for h in heads]
    vsl = [slice(h * B_HEAD_V, (h + 1) * B_HEAD_V) for h in heads]
    qe, ke, kd, dcol = [], [], [], []

    def gates(h):
        ks = ksl[h]
        z = jnp.dot(glr, wg_ref[:, ks], preferred_element_type=F32) + bg_ref[:, ks]
        g = (jnp.minimum(z, 0.0) - jnp.log(1.0 + jnp.exp(-jnp.abs(z)))) * (1.0 / GATE_TAU)
        g_hi = g.astype(BF16)
        g_lo = (g - g_hi.astype(F32)).astype(BF16)
        b = jnp.dot(tri, g_hi, preferred_element_type=F32) + jnp.dot(tri, g_lo, preferred_element_type=F32)
        bl = jnp.concatenate(
            [jnp.broadcast_to(b[(c + 1) * CHUNK - 1:(c + 1) * CHUNK, :], (CHUNK, B_HEAD_K)) for c in range(nchunk)],
            axis=0)
        qc = q_ref[:, ks].astype(F32) * (B_HEAD_K ** -0.5)
        kc = k_ref[:, ks].astype(F32)
        qe.append((qc * jnp.exp(b)).astype(BF16))
        ke.append((kc * jnp.exp(-b)).astype(BF16))
        kd.append((kc * jnp.exp(bl - b)).astype(BF16))
        dcol.append(jnp.exp(lax.dot_general(g_hi, sel, tn, preferred_element_type=F32)
                            + lax.dot_general(g_lo, sel, tn, preferred_element_type=F32)))

    def intra(h):
        a = lax.dot_general(qe[h], ke[h], nt, preferred_element_type=F32)
        a = jnp.where(causal, a, 0.0).astype(BF16)
        oacc_ref[:, vsl[h]] = jnp.dot(a, v_ref[:, vsl[h]], preferred_element_type=F32)

    gates(0)
    for h in heads:
        if h + 1 < B_HEADS:
            gates(h + 1)
        intra(h)

    st = [st_ref[h] for h in heads]
    for c in range(nchunk):
        rs = slice(c * CHUNK, (c + 1) * CHUNK)
        for h in heads:
            oacc_ref[rs, vsl[h]] += jnp.dot(qe[h][rs], st[h].astype(BF16), preferred_element_type=F32)
            dec = jnp.broadcast_to(dcol[h][:, c:c + 1], (B_HEAD_K, B_HEAD_V))
            st[h] = st[h] * dec + lax.dot_general(kd[h][rs], v_ref[rs, vsl[h]], tn, preferred_element_type=F32)
    for h in heads:
        st_ref[h] = st[h]
        o = oacc_ref[:, vsl[h]]
        ms = jnp.mean(o * o, axis=-1, keepdims=True)
        on = o * lax.rsqrt(ms + EPS) * nrm_ref[...]
        rr = r_ref[:, vsl[h]].astype(F32)
        o_ref[:, vsl[h]] = (on * (rr * jax.nn.sigmoid(rr))).astype(o_ref.dtype)


def _gla(proj, w_g2, b_g2, gla_norm, bsz, seq):
    q_blk = _NEW_OFF["g_q"] // B_QK
    k_blk = _NEW_OFF["g_k"] // B_QK
    v_blk = _NEW_OFF["g_v"] // B_V
    r_blk = _NEW_OFF["g_r"] // B_V
    ts = min(seq, GLA_SEQ_TILE)
    nt = seq // ts
    nchunk = ts // CHUNK
    kern = functools.partial(_gla_kernel, nchunk=nchunk)
    rc = np.arange(ts)[:, None] // CHUNK
    cc = np.arange(ts)[None, :] // CHUNK
    tri = jnp.asarray((rc == cc) & (np.arange(ts)[None, :] <= np.arange(ts)[:, None]), BF16)
    sel = jnp.asarray(rc == np.arange(LANES)[None, :], BF16)
    const = lambda shape: pl.BlockSpec(shape, lambda b, t: (0, 0))
    return pl.pallas_call(
        kern,
        grid=(bsz, nt),
        in_specs=[
            pl.BlockSpec((ts, B_QK), lambda b, t: (b * nt + t, q_blk)),
            pl.BlockSpec((ts, B_QK), lambda b, t: (b * nt + t, k_blk)),
            pl.BlockSpec((ts, B_V), lambda b, t: (b * nt + t, v_blk)),
            pl.BlockSpec((ts, B_V), lambda b, t: (b * nt + t, r_blk)),
            pl.BlockSpec((ts, LANES), lambda b, t: (b * nt + t, SMALL_OFF // LANES)),
            pl.BlockSpec((GATE_RANK, B_QK), lambda b, t: (0, 0)),
            pl.BlockSpec((1, B_QK), lambda b, t: (0, 0)),
            pl.BlockSpec((1, B_HEAD_V), lambda b, t: (0, 0)),
            const((ts, ts)), const((ts, LANES)),
        ],
        out_specs=pl.BlockSpec((ts, B_V), lambda b, t: (b * nt + t, 0)),
        out_shape=jax.ShapeDtypeStruct((bsz * seq, B_V), BF16),
        scratch_shapes=[pltpu.VMEM((B_HEADS, B_HEAD_K, B_HEAD_V), F32),
                        pltpu.VMEM((ts, B_V), F32)],
        compiler_params=_cparams(("arbitrary", "arbitrary")),
        name="gla",
    )(proj, proj, proj, proj, proj, w_g2.astype(BF16), b_g2.reshape(1, -1), gla_norm.reshape(1, -1),
      tri, sel)


def _outproj_kernel(oa_ref, ob_ref, w_ref, x_ref, modj_ref, mod_ref, g_ref, b_ref, x1_ref, h2_ref,
                    *, alpha, nj, tn):
    j = pl.program_id(1)
    a = jnp.concatenate([oa_ref[...], ob_ref[...]], axis=1)
    y = jnp.dot(a, w_ref[...], preferred_element_type=F32)
    z = alpha * x_ref[...] + modj_ref[2:3, :] * y
    for jj in range(nj):
        @pl.when(j == jj)
        def _(jj=jj):
            x1_ref[:, jj * tn:(jj + 1) * tn] = z

    @pl.when(j == nj - 1)
    def _():
        def rows(rs):
            x1 = _layer_norm(x1_ref[rs, :]) * g_ref[...] + b_ref[...]
            x1_ref[rs, :] = x1
            h2_ref[rs, :] = (_layer_norm(x1) * (1.0 + mod_ref[4:5, :]) + mod_ref[3:4, :]).astype(h2_ref.dtype)
        _for_row_chunks(x1_ref.shape[0], ROW_CHUNK, rows)


def _outproj(o_a, o_b, w, x2, mod3, ln_g, ln_b, seq, alpha):
    m, d = x2.shape
    tm, tn = 512, 1024
    nj = d // tn
    rows_per_seq = seq // tm
    kern = functools.partial(_outproj_kernel, alpha=alpha, nj=nj, tn=tn)
    return pl.pallas_call(
        kern,
        grid=(m // tm, nj),
        in_specs=[
            pl.BlockSpec((tm, A_Q), lambda i, j: (i, 0)),
            pl.BlockSpec((tm, B_V), lambda i, j: (i, 0)),
            pl.BlockSpec((A_Q + B_V, tn), lambda i, j: (0, j)),
            pl.BlockSpec((tm, tn), lambda i, j: (i, j)),
            pl.BlockSpec((None, 6, tn), lambda i, j: (i // rows_per_seq, 0, j)),
            pl.BlockSpec((None, 6, d), lambda i, j: (i // rows_per_seq, 0, 0)),
            pl.BlockSpec((1, d), lambda i, j: (0, 0)),
            pl.BlockSpec((1, d), lambda i, j: (0, 0)),
        ],
        out_specs=[pl.BlockSpec((tm, d), lambda i, j: (i, 0)),
                   pl.BlockSpec((tm, d), lambda i, j: (i, 0))],
        out_shape=[jax.ShapeDtypeStruct((m, d), F32), jax.ShapeDtypeStruct((m, d), BF16)],
        compiler_params=_cparams(("arbitrary", "arbitrary"), VMEM_LIMIT_FFN),
        name="outproj",
    )(o_a, o_b, w, x2, mod3, mod3, ln_g.reshape(1, d), ln_b.reshape(1, d))


def _ffn_kernel(h_ref, wu_ref, wg_ref, cw_ref, cb_ref, wd_ref, x1_ref, mod_ref, g_ref, b_ref, o_ref,
                carry_ref, act_ref, *, alpha, nj, rows_per_seq):
    i = pl.program_id(0)
    j = pl.program_id(1)

    def up_gate():
        h = h_ref[...]
        tm = h.shape[0]
        u = jnp.dot(h, wu_ref[...], preferred_element_type=F32)
        gt = jnp.dot(h, wg_ref[...], preferred_element_type=F32)
        first = (i % rows_per_seq) == 0
        prev = jnp.where(first, 0.0, carry_ref[j])
        carry_ref[j] = u[tm - 8:tm, :]
        rid = lax.broadcasted_iota(I32, u.shape, 0)
        p1 = prev[7:8, :]
        p2 = prev[6:7, :]
        u1 = jnp.where(rid == 0, p1, pltpu.roll(u, 1, axis=0))
        u2 = jnp.where(rid == 0, p2, jnp.where(rid == 1, p1, pltpu.roll(u, 2, axis=0)))
        cv = cw_ref[0:1, :] * u2 + cw_ref[1:2, :] * u1 + cw_ref[2:3, :] * u + cb_ref[...]
        cdf = 0.5 * (1.0 + jnp.tanh(math.sqrt(2.0 / math.pi) * (cv + 0.044715 * (cv ** 3))))
        return (cv * cdf * gt).astype(BF16)

    def down(act):
        for c0 in range(0, o_ref.shape[1], FFN_DOWN_CHUNK):
            cs = slice(c0, c0 + FFN_DOWN_CHUNK)
            o_ref[:, cs] += jnp.dot(act, wd_ref[:, cs], preferred_element_type=F32)

    @pl.when(j == 0)
    def _():
        o_ref[...] = jnp.zeros(o_ref.shape, F32)
        act_ref[...] = up_gate()

    @pl.when(jnp.logical_and(j > 0, j < nj))
    def _():
        act_prev = act_ref[...]
        act_ref[...] = up_gate()
        down(act_prev)

    @pl.when(j == nj)
    def _():
        down(act_ref[...])

        def rows(rs):
            z = alpha * x1_ref[rs, :] + mod_ref[5:6, :] * o_ref[rs, :]
            o_ref[rs, :] = _layer_norm(z) * g_ref[...] + b_ref[...]
        _for_row_chunks(o_ref.shape[0], ROW_CHUNK, rows)


def _ffn(h2, w_up, w_gate, conv_w, conv_b, w_down, l, x1, mod3, ln_g, ln_b, seq, alpha):
    m, d = h2.shape
    tm, tf = 512, FFN_TILE
    f = -(-w_up.shape[2] // tf) * tf
    pad = f - w_up.shape[2]
    w_up, w_gate = (_cast_pad(a, l, d, f) for a in (w_up, w_gate))
    w_down = _cast_pad(w_down, l, f, d)
    conv_w = jnp.pad(conv_w, ((0, 0), (0, pad)))
    conv_b = jnp.pad(conv_b, (0, pad))
    nj = f // tf
    rows_per_seq = seq // tm
    kern = functools.partial(_ffn_kernel, alpha=alpha, nj=nj, rows_per_seq=rows_per_seq)
    once = pl.Buffered(1)
    up_tile = lambda i, j: (0, jnp.minimum(j, nj - 1))
    return pl.pallas_call(
        kern,
        grid=(m // tm, nj + 1),
        in_specs=[
            pl.BlockSpec((tm, d), lambda i, j: (i, 0), pipeline_mode=once),
            pl.BlockSpec((d, tf), up_tile),
            pl.BlockSpec((d, tf), up_tile),
            pl.BlockSpec((CONV_W, tf), up_tile),
            pl.BlockSpec((1, tf), up_tile),
            pl.BlockSpec((tf, d), lambda i, j: (jnp.maximum(j - 1, 0), 0)),
            pl.BlockSpec((tm, d), lambda i, j: (i, 0), pipeline_mode=once),
            pl.BlockSpec((None, 6, d), lambda i, j: (i // rows_per_seq, 0, 0)),
            pl.BlockSpec((1, d), lambda i, j: (0, 0)),
            pl.BlockSpec((1, d), lambda i, j: (0, 0)),
        ],
        out_specs=pl.BlockSpec((tm, d), lambda i, j: (i, 0)),
        out_shape=jax.ShapeDtypeStruct((m, d), F32),
        scratch_shapes=[pltpu.VMEM((nj, 8, tf), F32),
                        pltpu.VMEM((tm, tf), BF16)],
        compiler_params=_cparams(("arbitrary", "arbitrary"), VMEM_LIMIT_FFN),
        name="ffn",
    )(h2, w_up, w_gate, conv_w, conv_b.reshape(1, f), w_down, x1, mod3, ln_g.reshape(1, d), ln_b.reshape(1, d))


def _regroup_kernel(w_ref, o_ref):
    for n in PROJ_ORDER:
        o_ref[:, _NEW_OFF[n]:_NEW_OFF[n] + _WIDTH[n]] = w_ref[:, _OLD_OFF[n]:_OLD_OFF[n] + _WIDTH[n]].astype(BF16)
    o_ref[:, _PROJ_USED:] = jnp.zeros((o_ref.shape[0], PROJ_COLS - _PROJ_USED), BF16)


def _cast_pad_kernel(w_ref, o_ref, *, rows_valid):
    tk, n = w_ref.shape
    rid = lax.broadcasted_iota(I32, (tk, n), 0) + pl.program_id(0) * tk
    o_ref[:, :n] = jnp.where(rid < rows_valid, w_ref[...], 0.0).astype(BF16)
    if o_ref.shape[1] > n:
        o_ref[:, n:] = jnp.zeros((tk, o_ref.shape[1] - n), BF16)


def _cast_pad(w, l, rows, cols):
    _, r, n = w.shape
    tk = 256
    last = (r - 1) // tk
    return pl.pallas_call(
        functools.partial(_cast_pad_kernel, rows_valid=r),
        grid=(rows // tk,),
        in_specs=[pl.BlockSpec((None, tk, n), lambda i: (l, jnp.minimum(i, last), 0))],
        out_specs=pl.BlockSpec((tk, cols), lambda i: (i, 0)),
        out_shape=jax.ShapeDtypeStruct((rows, cols), BF16),
        compiler_params=_cparams(("arbitrary",)),
        name="cast_pad",
    )(w)


def _regroup_w_in(w, l):
    _, d, n = w.shape
    tk = 256
    return pl.pallas_call(
        _regroup_kernel,
        grid=(d // tk,),
        in_specs=[pl.BlockSpec((None, tk, n), lambda i: (l, i, 0))],
        out_specs=pl.BlockSpec((tk, PROJ_COLS), lambda i: (i, 0)),
        out_shape=jax.ShapeDtypeStruct((d, PROJ_COLS), BF16),
        compiler_params=_cparams(("arbitrary",)),
        name="regroup_w_in",
    )(w)


def kernel(x, c, t5_table, w_ada, b_ada, w_in, w_g2, b_g2, gla_norm, w_out, ln1_g, ln1_b, w_up, w_gate,
           conv_w, conv_b, w_down, ln2_g, ln2_b):
    bsz, seq, d = x.shape
    depth = w_ada.shape[0]
    alpha = (2 * depth) ** 0.25
    nkb = seq // KEY_BLOCK
    x2 = x.reshape(bsz * seq, d)
    tiles = _bias_tiles(t5_table)
    for l in range(depth):
        mod3 = _ada(c, w_ada[l], b_ada[l]).reshape(bsz, 6, d)
        proj = _inproj(x2, mod3, _regroup_w_in(w_in, l), seq)

        p3 = proj.reshape(bsz, nkb, KEY_BLOCK, PROJ_COLS)
        ko, io = _NEW_OFF["a_k"], _NEW_OFF["i_k"]
        kt = jnp.swapaxes(p3[..., ko:ko + A_HEAD_DIM], 2, 3)
        kit = jnp.swapaxes(p3[..., io:io + IDX_DIM], 2, 3)
        zz = jnp.zeros_like(kit)
        kbd = jnp.concatenate([jnp.concatenate([kit, zz], axis=3),
                               jnp.concatenate([zz, kit], axis=3)], axis=2)

        o_a = _dsa(proj, kbd, kt, tiles, bsz, seq)
        o_b = _gla(proj, w_g2[l], b_g2[l], gla_norm[l], bsz, seq)
        x1, h2 = _outproj(o_a, o_b, w_out[l].astype(BF16), x2, mod3, ln1_g[l], ln1_b[l], seq, alpha)
        x2 = _ffn(h2, w_up, w_gate, conv_w[l], conv_b[l], w_down, l, x1, mod3, ln2_g[l], ln2_b[l],
                  seq, alpha)
    return x2.reshape(bsz, seq, d)
```

```python
import functools
import math

import numpy as np
import jax
import jax.numpy as jnp
from jax import lax
from jax.experimental import pallas as pl
from jax.experimental.pallas import tpu as pltpu

F32 = jnp.float32
BF16 = jnp.bfloat16
I32 = jnp.int32

CHUNK = 64
Q_BLOCK = 128
A_HEADS = 16
A_HEAD_DIM = 128
IDX_HEADS = 32
IDX_DIM = 64
TOPK_MAX = 256
T5_BUCKETS = 32
T5_MAX_DIST = 128
B_HEADS = 4
B_HEAD_V = 512
B_HEAD_K = 256
GATE_RANK = 16
GATE_TAU = 16.0
CONV_W = 3
EPS = 1e-6

A_Q = A_HEADS * A_HEAD_DIM
IDX_Q = IDX_HEADS * IDX_DIM
B_QK = B_HEADS * B_HEAD_K
B_V = B_HEADS * B_HEAD_V
IN_SPLITS = (A_Q, A_HEAD_DIM, A_HEAD_DIM, IDX_Q, IDX_DIM, IDX_HEADS, B_QK, B_QK, B_V, GATE_RANK, B_V)
IN_NAMES = ("a_q", "a_k", "a_v", "i_q", "i_k", "i_w", "g_q", "g_k", "g_v", "g_lr", "g_r")
PROJ_ORDER = ("a_q", "i_q", "g_v", "g_r", "g_q", "g_k", "a_k", "a_v", "i_k", "i_w", "g_lr")
PROJ_TILE = 1536

LANES = 128
VMEM_LIMIT = 56 * 1024 * 1024
VMEM_LIMIT_FFN = 60 * 1024 * 1024

KEY_BLOCK = 2 * Q_BLOCK
NEG_MASK = -1e30
LOG2E = math.log2(math.e)
INT_MIN = -(2 ** 31)


def _proj_layout():
    old_off = dict(zip(IN_NAMES, np.concatenate([[0], np.cumsum(IN_SPLITS)[:-1]]).tolist()))
    width = dict(zip(IN_NAMES, IN_SPLITS))
    new_off, pos = {}, 0
    for name in PROJ_ORDER:
        new_off[name] = pos
        pos += width[name]
    total = -(-pos // PROJ_TILE) * PROJ_TILE
    return old_off, width, new_off, pos, total


_OLD_OFF, _WIDTH, _NEW_OFF, _PROJ_USED, PROJ_COLS = _proj_layout()
SMALL_OFF = _NEW_OFF["i_k"]
IK_LO = 0
IW_LO = _NEW_OFF["i_w"] - SMALL_OFF
GLR_LO = _NEW_OFF["g_lr"] - SMALL_OFF


def _layer_norm(x):
    mu = jnp.mean(x, axis=-1, keepdims=True)
    xc = x - mu
    var = jnp.mean(xc * xc, axis=-1, keepdims=True)
    return xc * lax.rsqrt(var + EPS)


def _for_row_chunks(nrows, chunk, fn):
    def body(r, carry):
        fn(pl.ds(pl.multiple_of(r * chunk, chunk), chunk))
        return carry
    lax.fori_loop(0, nrows // chunk, body, 0)


ROW_CHUNK = 64
FFN_TILE = 512
GLA_SEQ_TILE = 512
FFN_DOWN_CHUNK = 512


def _cparams(sem, vmem=VMEM_LIMIT):
    return pltpu.CompilerParams(dimension_semantics=sem, vmem_limit_bytes=vmem)


def _ada_kernel(c_ref, w_ref, b_ref, o_ref):
    c = c_ref[...]
    ca = (c * jax.nn.sigmoid(c)).astype(BF16)
    o_ref[...] = jnp.dot(ca, w_ref[...].astype(BF16), preferred_element_type=F32) + b_ref[...]


def _ada(c, w, b):
    bsz, d = c.shape
    n = w.shape[1]
    tn = 512
    return pl.pallas_call(
        _ada_kernel,
        grid=(n // tn,),
        in_specs=[pl.BlockSpec((bsz, d), lambda j: (0, 0)),
                  pl.BlockSpec((d, tn), lambda j: (0, j)),
                  pl.BlockSpec((1, tn), lambda j: (0, j))],
        out_specs=pl.BlockSpec((bsz, tn), lambda j: (0, j)),
        out_shape=jax.ShapeDtypeStruct((bsz, n), F32),
        compiler_params=_cparams(("arbitrary",)),
        name="ada",
    )(c, w, b.reshape(1, n))


def _inproj_kernel(x_ref, mod_ref, w_ref, o_ref, h_ref):
    @pl.when(pl.program_id(1) == 0)
    def _():
        def rows(rs):
            xn = _layer_norm(x_ref[rs, :])
            h_ref[rs, :] = (xn * (1.0 + mod_ref[1:2, :]) + mod_ref[0:1, :]).astype(BF16)
        _for_row_chunks(x_ref.shape[0], ROW_CHUNK, rows)

    o_ref[...] = lax.dot_general(h_ref[...], w_ref[...], (((1,), (1,)), ((), ())),
                                 preferred_element_type=F32).astype(o_ref.dtype)


def _inproj(x2, mod3, wt, seq):
    m, d = x2.shape
    n = wt.shape[0]
    tm, tn = 512, PROJ_TILE
    rows_per_seq = seq // tm
    return pl.pallas_call(
        _inproj_kernel,
        grid=(m // tm, n // tn),
        in_specs=[pl.BlockSpec((tm, d), lambda i, j: (i, 0)),
                  pl.BlockSpec((None, 6, d), lambda i, j: (i // rows_per_seq, 0, 0)),
                  pl.BlockSpec((tn, d), lambda i, j: (j, 0))],
        out_specs=pl.BlockSpec((tm, tn), lambda i, j: (i, j)),
        out_shape=jax.ShapeDtypeStruct((m, n), BF16),
        scratch_shapes=[pltpu.VMEM((tm, d), BF16)],
        compiler_params=_cparams(("arbitrary", "arbitrary")),
        name="inproj",
    )(x2, mod3, wt)


def _t5_bucket(rel):
    half = T5_BUCKETS // 2
    max_exact = half // 2
    ret = jnp.where(rel > 0, half, 0)
    n = jnp.abs(rel)
    nf = jnp.maximum(n, 1).astype(jnp.float32)
    large = max_exact + (jnp.log(nf / max_exact) / math.log(T5_MAX_DIST / max_exact)
                         * (half - max_exact)).astype(jnp.int32)
    large = jnp.minimum(large, half - 1)
    return ret + jnp.where(n < max_exact, n, large)


def _bias_kernel(tab_ref, bkt_ref, o_ref):
    for u in range(3):
        bk = bkt_ref[u]
        for h in range(A_HEADS):
            acc = jnp.zeros((Q_BLOCK, Q_BLOCK), F32)
            for b in range(T5_BUCKETS):
                acc = jnp.where(bk == b, tab_ref[b, h], acc)
            o_ref[u, h] = acc * LOG2E


def _bias_tiles(t5_table):
    i = jnp.arange(Q_BLOCK, dtype=I32)[:, None]
    j = jnp.arange(Q_BLOCK, dtype=I32)[None, :]
    rel = jnp.stack([j - i - 2 * Q_BLOCK, j - i - Q_BLOCK, j - i])
    bkt = _t5_bucket(rel).astype(I32)
    return pl.pallas_call(
        _bias_kernel,
        in_specs=[pl.BlockSpec(memory_space=pltpu.SMEM),
                  pl.BlockSpec((3, Q_BLOCK, Q_BLOCK), lambda: (0, 0, 0))],
        out_specs=pl.BlockSpec((3, A_HEADS, Q_BLOCK, Q_BLOCK), lambda: (0, 0, 0, 0)),
        out_shape=jax.ShapeDtypeStruct((3, A_HEADS, Q_BLOCK, Q_BLOCK), F32),
        name="t5_bias",
    )(t5_table, bkt)


def _side_cast(src_ref, dst_ref, row0=None, rows_valid=None):
    r, n = src_ref.shape
    w = src_ref[...]
    if rows_valid is not None:
        rid = lax.broadcasted_iota(I32, (r, n), 0) + row0
        w = jnp.where(rid < rows_valid, w, 0.0)
    dst_ref[:, :n] = w.astype(BF16)
    if dst_ref.shape[1] > n:
        dst_ref[:, n:] = jnp.zeros((r, dst_ref.shape[1] - n), BF16)


def _dsa_kernel(aq_ref, iq_ref, sm_ref, kbd_ref, kt_ref, v_ref, tiles_ref, exp_ref, wa_ref, wb2_ref,
                o_ref, wa_o_ref, wb2_o_ref,
                key_ref, nm_ref, wb_ref, t_ref, q_ref, lhs_ref, m_ref, l_ref, acc_ref, x_ref,
                *, nkb_max, topk):
    _side_cast(wa_ref, wa_o_ref)
    _side_cast(wb2_ref, wb2_o_ref)
    qb = pl.program_id(1)
    nkb = lax.shift_right_logical(qb + 2, 1)
    rows = A_HEADS * Q_BLOCK
    half = KEY_BLOCK // 2

    iw = jnp.dot(sm_ref[:, IW_LO:IW_LO + IDX_HEADS], exp_ref[...], preferred_element_type=F32)
    for hh in range(IDX_HEADS):
        wb_ref[hh] = iw[:, hh * LANES:(hh + 1) * LANES] * (IDX_DIM ** -0.5 * IDX_HEADS ** -0.5)
    for p in range(IDX_HEADS // 2):
        lhs_ref[p * Q_BLOCK:(p + 1) * Q_BLOCK, :] = iq_ref[:, p * LANES:(p + 1) * LANES]
    for h in range(A_HEADS):
        q_ref[h * Q_BLOCK:(h + 1) * Q_BLOCK, :] = aq_ref[:, h * LANES:(h + 1) * LANES]

    row_chunk = lax.shift_right_logical(
        lax.broadcasted_iota(I32, (Q_BLOCK, KEY_BLOCK), 0) + qb * Q_BLOCK, 6)
    col_iota = lax.broadcasted_iota(I32, (Q_BLOCK, KEY_BLOCK), 1)

    def idx_body(kb, carry):
        res = jnp.dot(lhs_ref[...], kbd_ref[kb], preferred_element_type=F32)
        acc = jnp.zeros((Q_BLOCK, KEY_BLOCK), F32)
        for p in range(IDX_HEADS // 2):
            r = res[p * Q_BLOCK:(p + 1) * Q_BLOCK]
            we = wb_ref[2 * p]
            wo = wb_ref[2 * p + 1]
            acc = acc + jnp.maximum(r[:, :KEY_BLOCK], 0.0) * jnp.concatenate([we, we], axis=1)
            acc = acc + jnp.maximum(r[:, KEY_BLOCK:], 0.0) * jnp.concatenate([wo, wo], axis=1)
        bits = pltpu.bitcast(acc, I32)
        skey = bits ^ (lax.shift_right_arithmetic(bits, 31) & 0x7FFFFFFF)
        skey = jnp.where(acc == 0.0, 0, skey)
        adm = lax.shift_right_logical(col_iota + kb * KEY_BLOCK, 6) <= row_chunk
        key_ref[kb] = jnp.where(adm, skey, INT_MIN)
        return carry

    lax.fori_loop(0, nkb, idx_body, 0)

    def search(n):
        def count_ge(cand):
            c = jnp.zeros((Q_BLOCK, LANES), F32)
            for kb in range(n):
                k = key_ref[kb]
                c = c + jnp.where(k[:, :half] >= cand, 1.0, 0.0) + jnp.where(k[:, half:] >= cand, 1.0, 0.0)
            return jnp.broadcast_to(jnp.sum(c, axis=1, keepdims=True), (Q_BLOCK, LANES))

        zero = jnp.zeros((Q_BLOCK, LANES), I32)
        t0 = jnp.where(count_ge(zero) >= topk, zero, INT_MIN)

        def body(i, t):
            cand = t + lax.shift_left(jnp.int32(1), 30 - i)
            return jnp.where(count_ge(cand) >= topk, cand, t)

        t = lax.fori_loop(0, 31, body, t0)
        t_ref[...] = jnp.maximum(t, INT_MIN + 1)

    all_selected = (qb + 1) * Q_BLOCK <= topk

    @pl.when(all_selected)
    def _():
        t_ref[...] = jnp.full((Q_BLOCK, LANES), INT_MIN + 1, I32)

    for n in range(1, nkb_max + 1):
        pl.when(jnp.logical_and(nkb == n, jnp.logical_not(all_selected)))(functools.partial(search, n))

    thr = t_ref[...]
    thr2 = jnp.concatenate([thr, thr], axis=1)

    def cnt_body(kb, c):
        g = jnp.where(key_ref[kb] >= thr2, 1.0, 0.0)
        return c + g[:, :half] + g[:, half:]

    n_ge = jnp.sum(lax.fori_loop(0, nkb, cnt_body, jnp.zeros((Q_BLOCK, LANES), F32)),
                   axis=1, keepdims=True)
    has_ties = jnp.max(n_ge) > topk

    @pl.when(jnp.logical_not(has_ties))
    def _():
        def body(kb, carry):
            nm_ref[kb] = jnp.where(key_ref[kb] >= thr2, 0.0, NEG_MASK)
            return carry
        lax.fori_loop(0, nkb, body, 0)

    @pl.when(has_ties)
    def _():
        def gt_body(kb, c):
            g = jnp.where(key_ref[kb] > thr2, 1.0, 0.0)
            return c + g[:, :half] + g[:, half:]
        n_gt = jnp.sum(lax.fori_loop(0, nkb, gt_body, jnp.zeros((Q_BLOCK, LANES), F32)),
                       axis=1, keepdims=True)
        need = topk - n_gt
        tri = (lax.broadcasted_iota(I32, (half, half), 0)
               <= lax.broadcasted_iota(I32, (half, half), 1)).astype(BF16)

        def body(kb, seen):
            k = key_ref[kb]
            parts = []
            for s in range(2):
                ks = k[:, s * half:(s + 1) * half]
                eq = ks == thr
                eqf = jnp.where(eq, 1.0, 0.0)
                rank = jnp.dot(eqf.astype(BF16), tri, preferred_element_type=F32) + seen
                keep = jnp.logical_or(ks > thr, jnp.logical_and(eq, rank <= need))
                parts.append(jnp.where(keep, 0.0, NEG_MASK))
                seen = seen + jnp.sum(eqf, axis=1, keepdims=True)
            nm_ref[kb] = jnp.concatenate(parts, axis=1)
            return seen
        lax.fori_loop(0, nkb, body, jnp.zeros((Q_BLOCK, 1), F32))

    scale = A_HEAD_DIM ** -0.5 * LOG2E

    m_ref[...] = jnp.full((rows, LANES), NEG_MASK, F32)
    l_ref[...] = jnp.zeros((rows, LANES), F32)
    acc_ref[...] = jnp.zeros((rows, A_HEAD_DIM), F32)

    def max_body(kb, carry):
        x = jnp.dot(q_ref[...], kt_ref[kb], preferred_element_type=F32) * scale
        g0 = 2 * kb
        g1 = g0 + 1
        u0 = jnp.where(g0 == qb, 2, jnp.where(g0 == qb - 1, 1, 0))
        u1 = jnp.where(g1 == qb, 2, jnp.where(g1 == qb - 1, 1, 0))
        bias = jnp.concatenate([tiles_ref[u0].reshape(rows, half),
                                tiles_ref[u1].reshape(rows, half)], axis=1)
        x = ((x + bias).reshape(A_HEADS, Q_BLOCK, KEY_BLOCK) + nm_ref[kb][None]).reshape(rows, KEY_BLOCK)
        x_ref[kb] = x
        m_ref[...] = jnp.maximum(m_ref[...], jnp.maximum(x[:, :half], x[:, half:]))
        return carry

    lax.fori_loop(0, nkb, max_body, 0)
    m_ref[...] = jnp.broadcast_to(jnp.max(m_ref[...], axis=1, keepdims=True), (rows, LANES))

    def pv_body(kb, carry):
        m = m_ref[...]
        p = jnp.exp2(x_ref[kb] - jnp.concatenate([m, m], axis=1))
        l_ref[...] += p[:, :half] + p[:, half:]
        vb = v_ref[pl.ds(pl.multiple_of(kb * KEY_BLOCK, KEY_BLOCK), KEY_BLOCK), :]
        acc_ref[...] += jnp.dot(p.astype(BF16), vb, preferred_element_type=F32)
        return carry

    lax.fori_loop(0, nkb, pv_body, 0)

    out = acc_ref[...] / jnp.sum(l_ref[...], axis=1, keepdims=True)
    for h in range(A_HEADS):
        o_ref[:, h * A_HEAD_DIM:(h + 1) * A_HEAD_DIM] = out[h * Q_BLOCK:(h + 1) * Q_BLOCK].astype(o_ref.dtype)


def _dsa(proj, kbd, kt, tiles, bsz, seq, side_weights, layer, side_cols):
    nb = seq // Q_BLOCK
    nkb = seq // KEY_BLOCK
    topk = min(TOPK_MAX, seq // 4)
    rows = A_HEADS * Q_BLOCK
    av_blk = _NEW_OFF["a_v"] // A_HEAD_DIM
    kern = functools.partial(_dsa_kernel, nkb_max=nkb, topk=topk)
    expand = jnp.asarray(np.arange(IDX_HEADS)[:, None] == np.arange(IDX_HEADS * LANES)[None, :] // LANES, BF16)
    wa, wb2 = side_weights
    wd, wn = wa.shape[1], wa.shape[2]
    assert wd % (bsz * nb) == 0 and (wd // (bsz * nb)) % 16 == 0 and side_cols >= wn
    wr = wd // (bsz * nb)
    w_in_spec = pl.BlockSpec((None, wr, wn), lambda b, q: (layer, b * nb + q, 0))
    w_out_spec = pl.BlockSpec((wr, side_cols), lambda b, q: (b * nb + q, 0))
    w_out_shape = jax.ShapeDtypeStruct((wd, side_cols), BF16)
    return pl.pallas_call(
        kern,
        grid=(bsz, nb),
        in_specs=[
            pl.BlockSpec((Q_BLOCK, A_Q), lambda b, q: (b * nb + q, _NEW_OFF["a_q"] // A_Q)),
            pl.BlockSpec((Q_BLOCK, IDX_Q), lambda b, q: (b * nb + q, _NEW_OFF["i_q"] // IDX_Q)),
            pl.BlockSpec((Q_BLOCK, LANES), lambda b, q: (b * nb + q, SMALL_OFF // LANES)),
            pl.BlockSpec((None, nkb, 2 * IDX_DIM, 2 * KEY_BLOCK), lambda b, q: (b, 0, 0, 0)),
            pl.BlockSpec((None, nkb, A_HEAD_DIM, KEY_BLOCK), lambda b, q: (b, 0, 0, 0)),
            pl.BlockSpec((seq, A_HEAD_DIM), lambda b, q: (b, av_blk)),
            pl.BlockSpec((3, A_HEADS, Q_BLOCK, Q_BLOCK), lambda b, q: (0, 0, 0, 0)),
            pl.BlockSpec((IDX_HEADS, IDX_HEADS * LANES), lambda b, q: (0, 0)),
            w_in_spec, w_in_spec,
        ],
        out_specs=[pl.BlockSpec((Q_BLOCK, A_Q), lambda b, q: (b * nb + q, 0)), w_out_spec, w_out_spec],
        out_shape=[jax.ShapeDtypeStruct((bsz * seq, A_Q), BF16), w_out_shape, w_out_shape],
        scratch_shapes=[
            pltpu.VMEM((nkb, Q_BLOCK, KEY_BLOCK), I32),
            pltpu.VMEM((nkb, Q_BLOCK, KEY_BLOCK), F32),
            pltpu.VMEM((IDX_HEADS, Q_BLOCK, LANES), F32),
            pltpu.VMEM((Q_BLOCK, LANES), I32),
            pltpu.VMEM((rows, A_HEAD_DIM), BF16),
            pltpu.VMEM((rows, LANES), BF16),
            pltpu.VMEM((rows, LANES), F32),
            pltpu.VMEM((rows, LANES), F32),
            pltpu.VMEM((rows, A_HEAD_DIM), F32),
            pltpu.VMEM((nkb, rows, KEY_BLOCK), F32),
        ],
        compiler_params=_cparams(("arbitrary", "arbitrary")),
        name="dsa",
    )(proj, proj, proj, kbd, kt, proj, tiles, expand, wa, wb2)


def _gla_kernel(q_ref, k_ref, v_ref, r_ref, sm_ref, wg_ref, bg_ref, nrm_ref, tri_ref, sel_ref, wd_ref,
                o_ref, wd_o_ref, st_ref, oacc_ref, *, nchunk, side_rows):
    step = pl.program_id(0) * pl.num_programs(1) + pl.program_id(1)
    _side_cast(wd_ref, wd_o_ref, step * wd_ref.shape[0], side_rows)
    @pl.when(pl.program_id(1) == 0)
    def _():
        st_ref[...] = jnp.zeros(st_ref.shape, F32)

    ts = nchunk * CHUNK
    ri = lax.broadcasted_iota(I32, (ts, ts), 0)
    ci = lax.broadcasted_iota(I32, (ts, ts), 1)
    causal = jnp.logical_and(lax.shift_right_logical(ri, 6) == lax.shift_right_logical(ci, 6), ci <= ri)
    tn = (((0,), (0,)), ((), ()))
    nt = (((1,), (1,)), ((), ()))
    tri = tri_ref[...]
    sel = sel_ref[...]
    glr = sm_ref[:, GLR_LO:GLR_LO + GATE_RANK]

    heads = range(B_HEADS)
    ksl = [slice(h * B_HEAD_K, (h + 1) * B_HEAD_K) for h in heads]
    vsl = [slice(h * B_HEAD_V, (h + 1) * B_HEAD_V) for h in heads]
    qe, ke, kd, dcol = [], [], [], []

    def gates(h):
        ks = ksl[h]
        z = jnp.dot(glr, wg_ref[:, ks], preferred_element_type=F32) + bg_ref[:, ks]
        g = (jnp.minimum(z, 0.0) - jnp.log(1.0 + jnp.exp(-jnp.abs(z)))) * (1.0 / GATE_TAU)
        g_hi = g.astype(BF16)
        g_lo = (g - g_hi.astype(F32)).astype(BF16)
        b = jnp.dot(tri, g_hi, preferred_element_type=F32) + jnp.dot(tri, g_lo, preferred_element_type=F32)
        bl = jnp.concatenate(
            [jnp.broadcast_to(b[(c + 1) * CHUNK - 1:(c + 1) * CHUNK, :], (CHUNK, B_HEAD_K)) for c in range(nchunk)],
            axis=0)
        qc = q_ref[:, ks].astype(F32) * (B_HEAD_K ** -0.5)
        kc = k_ref[:, ks].astype(F32)
        qe.append((qc * jnp.exp(b)).astype(BF16))
        ke.append((kc * jnp.exp(-b)).astype(BF16))
        kd.append((kc * jnp.exp(bl - b)).astype(BF16))
        dcol.append(jnp.exp(lax.dot_general(g_hi, sel, tn, preferred_element_type=F32)
                            + lax.dot_general(g_lo, sel, tn, preferred_element_type=F32)))

    def intra(h):
        a = lax.dot_general(qe[h], ke[h], nt, preferred_element_type=F32)
        a = jnp.where(causal, a, 0.0).astype(BF16)
        oacc_ref[:, vsl[h]] = jnp.dot(a, v_ref[:, vsl[h]], preferred_element_type=F32)

    gates(0)
    for h in heads:
        if h + 1 < B_HEADS:
            gates(h + 1)
        intra(h)

    st = [st_ref[h] for h in heads]
    for c in range(nchunk):
        rs = slice(c * CHUNK, (c + 1) * CHUNK)
        for h in heads:
            oacc_ref[rs, vsl[h]] += jnp.dot(qe[h][rs], st[h].astype(BF16), preferred_element_type=F32)
            dec = jnp.broadcast_to(dcol[h][:, c:c + 1], (B_HEAD_K, B_HEAD_V))
            st[h] = st[h] * dec + lax.dot_general(kd[h][rs], v_ref[rs, vsl[h]], tn, preferred_element_type=F32)
    for h in heads:
        st_ref[h] = st[h]
        o = oacc_ref[:, vsl[h]]
        ms = jnp.mean(o * o, axis=-1, keepdims=True)
        on = o * lax.rsqrt(ms + EPS) * nrm_ref[...]
        rr = r_ref[:, vsl[h]].astype(F32)
        o_ref[:, vsl[h]] = (on * (rr * jax.nn.sigmoid(rr))).astype(o_ref.dtype)


def _gla(proj, w_g2, b_g2, gla_norm, bsz, seq, side_weight, layer, side_rows_padded):
    q_blk = _NEW_OFF["g_q"] // B_QK
    k_blk = _NEW_OFF["g_k"] // B_QK
    v_blk = _NEW_OFF["g_v"] // B_V
    r_blk = _NEW_OFF["g_r"] // B_V
    ts = min(seq, GLA_SEQ_TILE)
    nt = seq // ts
    nchunk = ts // CHUNK
    src_rows, wdim = side_weight.shape[1], side_weight.shape[2]
    nsteps = bsz * nt
    assert side_rows_padded % nsteps == 0 and (side_rows_padded // nsteps) % 16 == 0
    sr = side_rows_padded // nsteps
    last_src = (src_rows - 1) // sr
    kern = functools.partial(_gla_kernel, nchunk=nchunk, side_rows=src_rows)
    rc = np.arange(ts)[:, None] // CHUNK
    cc = np.arange(ts)[None, :] // CHUNK
    tri = jnp.asarray((rc == cc) & (np.arange(ts)[None, :] <= np.arange(ts)[:, None]), BF16)
    sel = jnp.asarray(rc == np.arange(LANES)[None, :], BF16)
    const = lambda shape: pl.BlockSpec(shape, lambda b, t: (0, 0))
    return pl.pallas_call(
        kern,
        grid=(bsz, nt),
        in_specs=[
            pl.BlockSpec((ts, B_QK), lambda b, t: (b * nt + t, q_blk)),
            pl.BlockSpec((ts, B_QK), lambda b, t: (b * nt + t, k_blk)),
            pl.BlockSpec((ts, B_V), lambda b, t: (b * nt + t, v_blk)),
            pl.BlockSpec((ts, B_V), lambda b, t: (b * nt + t, r_blk)),
            pl.BlockSpec((ts, LANES), lambda b, t: (b * nt + t, SMALL_OFF // LANES)),
            pl.BlockSpec((GATE_RANK, B_QK), lambda b, t: (0, 0)),
            pl.BlockSpec((1, B_QK), lambda b, t: (0, 0)),
            pl.BlockSpec((1, B_HEAD_V), lambda b, t: (0, 0)),
            const((ts, ts)), const((ts, LANES)),
            pl.BlockSpec((None, sr, wdim), lambda b, t: (layer, jnp.minimum(b * nt + t, last_src), 0)),
        ],
        out_specs=[pl.BlockSpec((ts, B_V), lambda b, t: (b * nt + t, 0)),
                   pl.BlockSpec((sr, wdim), lambda b, t: (b * nt + t, 0))],
        out_shape=[jax.ShapeDtypeStruct((bsz * seq, B_V), BF16),
                   jax.ShapeDtypeStruct((side_rows_padded, wdim), BF16)],
        scratch_shapes=[pltpu.VMEM((B_HEADS, B_HEAD_K, B_HEAD_V), F32),
                        pltpu.VMEM((ts, B_V), F32)],
        compiler_params=_cparams(("arbitrary", "arbitrary")),
        name="gla",
    )(proj, proj, proj, proj, proj, w_g2.astype(BF16), b_g2.reshape(1, -1), gla_norm.reshape(1, -1),
      tri, sel, side_weight)


def _outproj_kernel(oa_ref, ob_ref, w_ref, x_ref, modj_ref, mod_ref, g_ref, b_ref, x1_ref, h2_ref,
                    *, alpha, nj, tn):
    j = pl.program_id(1)
    a = jnp.concatenate([oa_ref[...], ob_ref[...]], axis=1)
    y = jnp.dot(a, w_ref[...], preferred_element_type=F32)
    z = alpha * x_ref[...] + modj_ref[2:3, :] * y
    for jj in range(nj):
        @pl.when(j == jj)
        def _(jj=jj):
            x1_ref[:, jj * tn:(jj + 1) * tn] = z

    @pl.when(j == nj - 1)
    def _():
        def rows(rs):
            x1 = _layer_norm(x1_ref[rs, :]) * g_ref[...] + b_ref[...]
            x1_ref[rs, :] = x1
            h2_ref[rs, :] = (_layer_norm(x1) * (1.0 + mod_ref[4:5, :]) + mod_ref[3:4, :]).astype(h2_ref.dtype)
        _for_row_chunks(x1_ref.shape[0], ROW_CHUNK, rows)


def _outproj(o_a, o_b, w, x2, mod3, ln_g, ln_b, seq, alpha):
    m, d = x2.shape
    tm, tn = 512, 1024
    nj = d // tn
    rows_per_seq = seq // tm
    kern = functools.partial(_outproj_kernel, alpha=alpha, nj=nj, tn=tn)
    return pl.pallas_call(
        kern,
        grid=(m // tm, nj),
        in_specs=[
            pl.BlockSpec((tm, A_Q), lambda i, j: (i, 0)),
            pl.BlockSpec((tm, B_V), lambda i, j: (i, 0)),
            pl.BlockSpec((A_Q + B_V, tn), lambda i, j: (0, j)),
            pl.BlockSpec((tm, tn), lambda i, j: (i, j)),
            pl.BlockSpec((None, 6, tn), lambda i, j: (i // rows_per_seq, 0, j)),
            pl.BlockSpec((None, 6, d), lambda i, j: (i // rows_per_seq, 0, 0)),
            pl.BlockSpec((1, d), lambda i, j: (0, 0)),
            pl.BlockSpec((1, d), lambda i, j: (0, 0)),
        ],
        out_specs=[pl.BlockSpec((tm, d), lambda i, j: (i, 0)),
                   pl.BlockSpec((tm, d), lambda i, j: (i, 0))],
        out_shape=[jax.ShapeDtypeStruct((m, d), F32), jax.ShapeDtypeStruct((m, d), BF16)],
        compiler_params=_cparams(("arbitrary", "arbitrary"), VMEM_LIMIT_FFN),
        name="outproj",
    )(o_a, o_b, w, x2, mod3, mod3, ln_g.reshape(1, d), ln_b.reshape(1, d))


def _ffn_kernel(h_ref, wu_ref, wg_ref, cw_ref, cb_ref, wd_ref, x1_ref, mod_ref, g_ref, b_ref, o_ref,
                carry_ref, act_ref, *, alpha, nj, rows_per_seq):
    i = pl.program_id(0)
    j = pl.program_id(1)

    def up_gate():
        h = h_ref[...]
        tm = h.shape[0]
        u = jnp.dot(h, wu_ref[...], preferred_element_type=F32)
        gt = jnp.dot(h, wg_ref[...], preferred_element_type=F32)
        first = (i % rows_per_seq) == 0
        prev = jnp.where(first, 0.0, carry_ref[j])
        carry_ref[j] = u[tm - 8:tm, :]
        rid = lax.broadcasted_iota(I32, u.shape, 0)
        p1 = prev[7:8, :]
        p2 = prev[6:7, :]
        u1 = jnp.where(rid == 0, p1, pltpu.roll(u, 1, axis=0))
        u2 = jnp.where(rid == 0, p2, jnp.where(rid == 1, p1, pltpu.roll(u, 2, axis=0)))
        cv = cw_ref[0:1, :] * u2 + cw_ref[1:2, :] * u1 + cw_ref[2:3, :] * u + cb_ref[...]
        cdf = 0.5 * (1.0 + jnp.tanh(math.sqrt(2.0 / math.pi) * (cv + 0.044715 * (cv ** 3))))
        return (cv * cdf * gt).astype(BF16)

    def down(act):
        for c0 in range(0, o_ref.shape[1], FFN_DOWN_CHUNK):
            cs = slice(c0, c0 + FFN_DOWN_CHUNK)
            o_ref[:, cs] += jnp.dot(act, wd_ref[:, cs], preferred_element_type=F32)

    @pl.when(j == 0)
    def _():
        o_ref[...] = jnp.zeros(o_ref.shape, F32)
        act_ref[...] = up_gate()

    @pl.when(jnp.logical_and(j > 0, j < nj))
    def _():
        act_prev = act_ref[...]
        act_ref[...] = up_gate()
        down(act_prev)

    @pl.when(j == nj)
    def _():
        down(act_ref[...])

        def rows(rs):
            z = alpha * x1_ref[rs, :] + mod_ref[5:6, :] * o_ref[rs, :]
            o_ref[rs, :] = _layer_norm(z) * g_ref[...] + b_ref[...]
        _for_row_chunks(o_ref.shape[0], ROW_CHUNK, rows)


def _ffn(h2, w_up, w_gate, conv_w, conv_b, w_down, x1, mod3, ln_g, ln_b, seq, alpha):
    m, d = h2.shape
    tm, tf = 512, FFN_TILE
    f = w_up.shape[1]
    pad = f - conv_w.shape[1]
    conv_w = jnp.pad(conv_w, ((0, 0), (0, pad)))
    conv_b = jnp.pad(conv_b, (0, pad))
    nj = f // tf
    rows_per_seq = seq // tm
    kern = functools.partial(_ffn_kernel, alpha=alpha, nj=nj, rows_per_seq=rows_per_seq)
    once = pl.Buffered(1)
    up_tile = lambda i, j: (0, jnp.minimum(j, nj - 1))
    return pl.pallas_call(
        kern,
        grid=(m // tm, nj + 1),
        in_specs=[
            pl.BlockSpec((tm, d), lambda i, j: (i, 0), pipeline_mode=once),
            pl.BlockSpec((d, tf), up_tile),
            pl.BlockSpec((d, tf), up_tile),
            pl.BlockSpec((CONV_W, tf), up_tile),
            pl.BlockSpec((1, tf), up_tile),
            pl.BlockSpec((tf, d), lambda i, j: (jnp.maximum(j - 1, 0), 0)),
            pl.BlockSpec((tm, d), lambda i, j: (i, 0), pipeline_mode=once),
            pl.BlockSpec((None, 6, d), lambda i, j: (i // rows_per_seq, 0, 0)),
            pl.BlockSpec((1, d), lambda i, j: (0, 0)),
            pl.BlockSpec((1, d), lambda i, j: (0, 0)),
        ],
        out_specs=pl.BlockSpec((tm, d), lambda i, j: (i, 0)),
        out_shape=jax.ShapeDtypeStruct((m, d), F32),
        scratch_shapes=[pltpu.VMEM((nj, 8, tf), F32),
                        pltpu.VMEM((tm, tf), BF16)],
        compiler_params=_cparams(("arbitrary", "arbitrary"), VMEM_LIMIT_FFN),
        name="ffn",
    )(h2, w_up, w_gate, conv_w, conv_b.reshape(1, f), w_down, x1, mod3, ln_g.reshape(1, d), ln_b.reshape(1, d))


def _regroup_kernel(w_ref, o_ref):
    for n in PROJ_ORDER:
        o_ref[_NEW_OFF[n]:_NEW_OFF[n] + _WIDTH[n], :] = w_ref[_OLD_OFF[n]:_OLD_OFF[n] + _WIDTH[n], :].astype(BF16)
    o_ref[_PROJ_USED:, :] = jnp.zeros((PROJ_COLS - _PROJ_USED, o_ref.shape[1]), BF16)


def _regroup_w_in(w, l):
    wt = jnp.swapaxes(w, 1, 2)
    _, n, d = wt.shape
    tc = 256
    return pl.pallas_call(
        _regroup_kernel,
        grid=(d // tc,),
        in_specs=[pl.BlockSpec((None, n, tc), lambda i: (l, 0, i))],
        out_specs=pl.BlockSpec((PROJ_COLS, tc), lambda i: (0, i)),
        out_shape=jax.ShapeDtypeStruct((PROJ_COLS, d), BF16),
        compiler_params=_cparams(("arbitrary",)),
        name="regroup_w_in",
    )(wt)


def kernel(x, c, t5_table, w_ada, b_ada, w_in, w_g2, b_g2, gla_norm, w_out, ln1_g, ln1_b, w_up, w_gate,
           conv_w, conv_b, w_down, ln2_g, ln2_b):
    bsz, seq, d = x.shape
    depth = w_ada.shape[0]
    alpha = (2 * depth) ** 0.25
    nkb = seq // KEY_BLOCK
    x2 = x.reshape(bsz * seq, d)
    tiles = _bias_tiles(t5_table)
    for l in range(depth):
        mod3 = _ada(c, w_ada[l], b_ada[l]).reshape(bsz, 6, d)
        proj = _inproj(x2, mod3, _regroup_w_in(w_in, l), seq)

        p3 = proj.reshape(bsz, nkb, KEY_BLOCK, PROJ_COLS)
        ko, io = _NEW_OFF["a_k"], _NEW_OFF["i_k"]
        kt = jnp.swapaxes(p3[..., ko:ko + A_HEAD_DIM], 2, 3)
        kit = jnp.swapaxes(p3[..., io:io + IDX_DIM], 2, 3)
        zz = jnp.zeros_like(kit)
        kbd = jnp.concatenate([jnp.concatenate([kit, zz], axis=3),
                               jnp.concatenate([zz, kit], axis=3)], axis=2)

        f_pad = -(-w_up.shape[2] // FFN_TILE) * FFN_TILE
        o_a, w_up_b, w_gate_b = _dsa(proj, kbd, kt, tiles, bsz, seq, (w_up, w_gate), l, f_pad)
        o_b, w_down_b = _gla(proj, w_g2[l], b_g2[l], gla_norm[l], bsz, seq, w_down, l, f_pad)
        x1, h2 = _outproj(o_a, o_b, w_out[l].astype(BF16), x2, mod3, ln1_g[l], ln1_b[l], seq, alpha)
        x2 = _ffn(h2, w_up_b, w_gate_b, conv_w[l], conv_b[l], w_down_b, x1, mod3, ln2_g[l], ln2_b[l],
                  seq, alpha)
    return x2.reshape(bsz, seq, d)
```

```python
import functools
import math

import numpy as np
import jax
import jax.numpy as jnp
from jax import lax
from jax.experimental import pallas as pl
from jax.experimental.pallas import tpu as pltpu

F32 = jnp.float32
BF16 = jnp.bfloat16
I32 = jnp.int32

CHUNK = 64
Q_BLOCK = 128
A_HEADS = 16
A_HEAD_DIM = 128
IDX_HEADS = 32
IDX_DIM = 64
TOPK_MAX = 256
T5_BUCKETS = 32
T5_MAX_DIST = 128
B_HEADS = 4
B_HEAD_V = 512
B_HEAD_K = 256
GATE_RANK = 16
GATE_TAU = 16.0
CONV_W = 3
EPS = 1e-6

A_Q = A_HEADS * A_HEAD_DIM
IDX_Q = IDX_HEADS * IDX_DIM
B_QK = B_HEADS * B_HEAD_K
B_V = B_HEADS * B_HEAD_V
IN_SPLITS = (A_Q, A_HEAD_DIM, A_HEAD_DIM, IDX_Q, IDX_DIM, IDX_HEADS, B_QK, B_QK, B_V, GATE_RANK, B_V)
IN_NAMES = ("a_q", "a_k", "a_v", "i_q", "i_k", "i_w", "g_q", "g_k", "g_v", "g_lr", "g_r")
PROJ_ORDER = ("a_q", "i_q", "g_v", "g_r", "g_q", "g_k", "a_k", "a_v", "i_k", "i_w", "g_lr")
PROJ_TILE = 1536

LANES = 128
VMEM_LIMIT = 56 * 1024 * 1024
VMEM_LIMIT_FFN = 60 * 1024 * 1024

KEY_BLOCK = 2 * Q_BLOCK
DSA_ROWS = KEY_BLOCK
DSA_HEAD_GROUP = 8
NEG_MASK = -1e30
LOG2E = math.log2(math.e)
INT_MIN = -(2 ** 31)


def _proj_layout():
    old_off = dict(zip(IN_NAMES, np.concatenate([[0], np.cumsum(IN_SPLITS)[:-1]]).tolist()))
    width = dict(zip(IN_NAMES, IN_SPLITS))
    new_off, pos = {}, 0
    for name in PROJ_ORDER:
        new_off[name] = pos
        pos += width[name]
    total = -(-pos // PROJ_TILE) * PROJ_TILE
    return old_off, width, new_off, pos, total


_OLD_OFF, _WIDTH, _NEW_OFF, _PROJ_USED, PROJ_COLS = _proj_layout()
SMALL_OFF = _NEW_OFF["i_k"]
IK_LO = 0
IW_LO = _NEW_OFF["i_w"] - SMALL_OFF
GLR_LO = _NEW_OFF["g_lr"] - SMALL_OFF


def _layer_norm(x):
    mu = jnp.mean(x, axis=-1, keepdims=True)
    xc = x - mu
    var = jnp.mean(xc * xc, axis=-1, keepdims=True)
    return xc * lax.rsqrt(var + EPS)


def _for_row_chunks(nrows, chunk, fn):
    def body(r, carry):
        fn(pl.ds(pl.multiple_of(r * chunk, chunk), chunk))
        return carry
    lax.fori_loop(0, nrows // chunk, body, 0)


ROW_CHUNK = 64
FFN_TILE = 512
GLA_SEQ_TILE = 512
FFN_DOWN_CHUNK = 512


def _cparams(sem, vmem=VMEM_LIMIT):
    return pltpu.CompilerParams(dimension_semantics=sem, vmem_limit_bytes=vmem)


def _ada_kernel(c_ref, w_ref, b_ref, o_ref):
    c = c_ref[...]
    ca = (c * jax.nn.sigmoid(c)).astype(BF16)
    o_ref[...] = jnp.dot(ca, w_ref[...].astype(BF16), preferred_element_type=F32) + b_ref[...]


def _ada(c, w, b):
    bsz, d = c.shape
    n = w.shape[1]
    tn = 512
    return pl.pallas_call(
        _ada_kernel,
        grid=(n // tn,),
        in_specs=[pl.BlockSpec((bsz, d), lambda j: (0, 0)),
                  pl.BlockSpec((d, tn), lambda j: (0, j)),
                  pl.BlockSpec((1, tn), lambda j: (0, j))],
        out_specs=pl.BlockSpec((bsz, tn), lambda j: (0, j)),
        out_shape=jax.ShapeDtypeStruct((bsz, n), F32),
        compiler_params=_cparams(("arbitrary",)),
        name="ada",
    )(c, w, b.reshape(1, n))


def _inproj_kernel(x_ref, mod_ref, w_ref, ws_ref, o_ref, ws_o_ref, h_ref):
    _side_cast(ws_ref, ws_o_ref)

    @pl.when(pl.program_id(1) == 0)
    def _():
        def rows(rs):
            xn = _layer_norm(x_ref[rs, :])
            h_ref[rs, :] = (xn * (1.0 + mod_ref[1:2, :]) + mod_ref[0:1, :]).astype(BF16)
        _for_row_chunks(x_ref.shape[0], ROW_CHUNK, rows)

    o_ref[...] = lax.dot_general(h_ref[...], w_ref[...], (((1,), (1,)), ((), ())),
                                 preferred_element_type=F32).astype(o_ref.dtype)


def _side_block_rows(total, nsteps):
    return next(r for r in range(16, total + 1, 16) if total % r == 0 and total // r <= nsteps)


def _inproj(x2, mod3, wt, seq, side_weight, layer, side_cols):
    m, d = x2.shape
    n = wt.shape[0]
    tm, tn = 512, PROJ_TILE
    nj = n // tn
    rows_per_seq = seq // tm
    wd, wn = side_weight.shape[1], side_weight.shape[2]
    wr = _side_block_rows(wd, (m // tm) * nj)
    blk = lambda i, j: jnp.minimum(i * nj + j, wd // wr - 1)
    return pl.pallas_call(
        _inproj_kernel,
        grid=(m // tm, nj),
        in_specs=[pl.BlockSpec((tm, d), lambda i, j: (i, 0)),
                  pl.BlockSpec((None, 6, d), lambda i, j: (i // rows_per_seq, 0, 0)),
                  pl.BlockSpec((tn, d), lambda i, j: (j, 0)),
                  pl.BlockSpec((None, wr, wn), lambda i, j: (layer, blk(i, j), 0))],
        out_specs=[pl.BlockSpec((tm, tn), lambda i, j: (i, j)),
                   pl.BlockSpec((wr, side_cols), lambda i, j: (blk(i, j), 0))],
        out_shape=[jax.ShapeDtypeStruct((m, n), BF16), jax.ShapeDtypeStruct((wd, side_cols), BF16)],
        scratch_shapes=[pltpu.VMEM((tm, d), BF16)],
        compiler_params=_cparams(("arbitrary", "arbitrary"), VMEM_LIMIT_FFN),
        name="inproj",
    )(x2, mod3, wt, side_weight)


def _t5_bucket(rel):
    half = T5_BUCKETS // 2
    max_exact = half // 2
    ret = jnp.where(rel > 0, half, 0)
    n = jnp.abs(rel)
    nf = jnp.maximum(n, 1).astype(jnp.float32)
    large = max_exact + (jnp.log(nf / max_exact) / math.log(T5_MAX_DIST / max_exact)
                         * (half - max_exact)).astype(jnp.int32)
    large = jnp.minimum(large, half - 1)
    return ret + jnp.where(n < max_exact, n, large)


def _bias_kernel(tab_ref, bkt_ref, o_ref):
    for u in range(3):
        bk = bkt_ref[u]
        for h in range(A_HEADS):
            acc = jnp.zeros((Q_BLOCK, Q_BLOCK), F32)
            for b in range(T5_BUCKETS):
                acc = jnp.where(bk == b, tab_ref[b, h], acc)
            o_ref[u, h] = acc * LOG2E


def _bias_tiles(t5_table):
    i = jnp.arange(Q_BLOCK, dtype=I32)[:, None]
    j = jnp.arange(Q_BLOCK, dtype=I32)[None, :]
    rel = jnp.stack([j - i - 2 * Q_BLOCK, j - i - Q_BLOCK, j - i])
    bkt = _t5_bucket(rel).astype(I32)
    return pl.pallas_call(
        _bias_kernel,
        in_specs=[pl.BlockSpec(memory_space=pltpu.SMEM),
                  pl.BlockSpec((3, Q_BLOCK, Q_BLOCK), lambda: (0, 0, 0))],
        out_specs=pl.BlockSpec((3, A_HEADS, Q_BLOCK, Q_BLOCK), lambda: (0, 0, 0, 0)),
        out_shape=jax.ShapeDtypeStruct((3, A_HEADS, Q_BLOCK, Q_BLOCK), F32),
        name="t5_bias",
    )(t5_table, bkt)


def _side_cast(src_ref, dst_ref, row0=None, rows_valid=None):
    r, n = src_ref.shape
    w = src_ref[...]
    if rows_valid is not None:
        rid = lax.broadcasted_iota(I32, (r, n), 0) + row0
        w = jnp.where(rid < rows_valid, w, 0.0)
    dst_ref[:, :n] = w.astype(BF16)
    if dst_ref.shape[1] > n:
        dst_ref[:, n:] = jnp.zeros((r, dst_ref.shape[1] - n), BF16)


def _dsa_kernel(aq_ref, iq_ref, sm_ref, kbd_ref, kt_ref, v_ref, tiles_ref, exp_ref, wa_ref, wc_ref,
                o_ref, wa_o_ref, wc_o_ref,
                key_ref, nm_ref, wb_ref, t_ref, q_ref, lhs_ref, m_ref, l_ref, acc_ref, x_ref,
                *, nkb_max, topk):
    _side_cast(wa_ref, wa_o_ref)
    _side_cast(wc_ref, wc_o_ref)
    pb = pl.program_id(1)
    nkb = pb + 1
    tq = DSA_ROWS
    grows = DSA_HEAD_GROUP * tq
    half = KEY_BLOCK // 2
    pairs_per_dot = 8

    iw = jnp.dot(sm_ref[:, IW_LO:IW_LO + IDX_HEADS], exp_ref[...], preferred_element_type=F32)
    for hh in range(IDX_HEADS):
        wb_ref[hh] = iw[:, hh * LANES:(hh + 1) * LANES] * (IDX_DIM ** -0.5 * IDX_HEADS ** -0.5)
    for p in range(IDX_HEADS // 2):
        lhs_ref[p * tq:(p + 1) * tq, :] = iq_ref[:, p * LANES:(p + 1) * LANES]
    for h in range(A_HEADS):
        q_ref[h * tq:(h + 1) * tq, :] = aq_ref[:, h * LANES:(h + 1) * LANES]

    row_chunk = lax.shift_right_logical(lax.broadcasted_iota(I32, (tq, KEY_BLOCK), 0) + pb * tq, 6)
    col_iota = lax.broadcasted_iota(I32, (tq, KEY_BLOCK), 1)

    def idx_body(kb, carry):
        acc = jnp.zeros((tq, KEY_BLOCK), F32)
        for p0 in range(0, IDX_HEADS // 2, pairs_per_dot):
            res = jnp.dot(lhs_ref[p0 * tq:(p0 + pairs_per_dot) * tq, :], kbd_ref[kb],
                          preferred_element_type=F32)
            for p in range(pairs_per_dot):
                r = res[p * tq:(p + 1) * tq]
                we = wb_ref[2 * (p0 + p)]
                wo = wb_ref[2 * (p0 + p) + 1]
                acc = acc + jnp.maximum(r[:, :KEY_BLOCK], 0.0) * jnp.concatenate([we, we], axis=1)
                acc = acc + jnp.maximum(r[:, KEY_BLOCK:], 0.0) * jnp.concatenate([wo, wo], axis=1)
        bits = pltpu.bitcast(acc, I32)
        skey = bits ^ (lax.shift_right_arithmetic(bits, 31) & 0x7FFFFFFF)
        skey = jnp.where(acc == 0.0, 0, skey)
        adm = lax.shift_right_logical(col_iota + kb * KEY_BLOCK, 6) <= row_chunk
        key_ref[kb] = jnp.where(adm, skey, INT_MIN)
        return carry

    lax.fori_loop(0, nkb, idx_body, 0)

    def search(n):
        def count_ge(cand):
            c = jnp.zeros((tq, LANES), F32)
            for kb in range(n):
                k = key_ref[kb]
                c = c + jnp.where(k[:, :half] >= cand, 1.0, 0.0) + jnp.where(k[:, half:] >= cand, 1.0, 0.0)
            return jnp.broadcast_to(jnp.sum(c, axis=1, keepdims=True), (tq, LANES))

        zero = jnp.zeros((tq, LANES), I32)
        t0 = jnp.where(count_ge(zero) >= topk, zero, INT_MIN)

        def body(i, t):
            cand = t + lax.shift_left(jnp.int32(1), 30 - i)
            return jnp.where(count_ge(cand) >= topk, cand, t)

        t = lax.fori_loop(0, 31, body, t0)
        t_ref[...] = jnp.maximum(t, INT_MIN + 1)

    all_selected = (pb + 1) * tq <= topk

    @pl.when(all_selected)
    def _():
        t_ref[...] = jnp.full((tq, LANES), INT_MIN + 1, I32)

    for n in range(1, nkb_max + 1):
        pl.when(jnp.logical_and(nkb == n, jnp.logical_not(all_selected)))(functools.partial(search, n))

    thr = t_ref[...]
    thr2 = jnp.concatenate([thr, thr], axis=1)

    def cnt_body(kb, c):
        g = jnp.where(key_ref[kb] >= thr2, 1.0, 0.0)
        return c + g[:, :half] + g[:, half:]

    n_ge = jnp.sum(lax.fori_loop(0, nkb, cnt_body, jnp.zeros((tq, LANES), F32)),
                   axis=1, keepdims=True)
    has_ties = jnp.max(n_ge) > topk

    @pl.when(jnp.logical_not(has_ties))
    def _():
        def body(kb, carry):
            nm_ref[kb] = jnp.where(key_ref[kb] >= thr2, 0.0, NEG_MASK)
            return carry
        lax.fori_loop(0, nkb, body, 0)

    @pl.when(has_ties)
    def _():
        def gt_body(kb, c):
            g = jnp.where(key_ref[kb] > thr2, 1.0, 0.0)
            return c + g[:, :half] + g[:, half:]
        n_gt = jnp.sum(lax.fori_loop(0, nkb, gt_body, jnp.zeros((tq, LANES), F32)),
                       axis=1, keepdims=True)
        need = topk - n_gt
        tri = (lax.broadcasted_iota(I32, (half, half), 0)
               <= lax.broadcasted_iota(I32, (half, half), 1)).astype(BF16)

        def body(kb, seen):
            k = key_ref[kb]
            parts = []
            for s in range(2):
                ks = k[:, s * half:(s + 1) * half]
                eq = ks == thr
                eqf = jnp.where(eq, 1.0, 0.0)
                rank = jnp.dot(eqf.astype(BF16), tri, preferred_element_type=F32) + seen
                keep = jnp.logical_or(ks > thr, jnp.logical_and(eq, rank <= need))
                parts.append(jnp.where(keep, 0.0, NEG_MASK))
                seen = seen + jnp.sum(eqf, axis=1, keepdims=True)
            nm_ref[kb] = jnp.concatenate(parts, axis=1)
            return seen
        lax.fori_loop(0, nkb, body, jnp.zeros((tq, 1), F32))

    scale = A_HEAD_DIM ** -0.5 * LOG2E

    def tile_of(g, qblk):
        return jnp.where(g == qblk, 2, jnp.where(g == qblk - 1, 1, 0))

    for h0 in range(0, A_HEADS, DSA_HEAD_GROUP):
        hs = slice(h0, h0 + DSA_HEAD_GROUP)
        qrows = slice(h0 * tq, (h0 + DSA_HEAD_GROUP) * tq)
        m_ref[...] = jnp.full((grows, LANES), NEG_MASK, F32)
        l_ref[...] = jnp.zeros((grows, LANES), F32)
        acc_ref[...] = jnp.zeros((grows, A_HEAD_DIM), F32)

        def max_body(kb, carry, hs=hs, qrows=qrows):
            x = jnp.dot(q_ref[qrows, :], kt_ref[kb], preferred_element_type=F32) * scale
            per_qblk = []
            for r in range(tq // Q_BLOCK):
                qblk = (tq // Q_BLOCK) * pb + r
                per_qblk.append(jnp.concatenate([tiles_ref[tile_of(2 * kb, qblk), hs],
                                                 tiles_ref[tile_of(2 * kb + 1, qblk), hs]], axis=2))
            bias = jnp.concatenate(per_qblk, axis=1)
            x = (x.reshape(DSA_HEAD_GROUP, tq, KEY_BLOCK) + bias + nm_ref[kb][None]).reshape(grows, KEY_BLOCK)
            x_ref[kb] = x
            m_ref[...] = jnp.maximum(m_ref[...], jnp.maximum(x[:, :half], x[:, half:]))
            return carry

        lax.fori_loop(0, nkb, max_body, 0)
        m_ref[...] = jnp.broadcast_to(jnp.max(m_ref[...], axis=1, keepdims=True), (grows, LANES))

        def pv_body(kb, carry):
            m = m_ref[...]
            p = jnp.exp2(x_ref[kb] - jnp.concatenate([m, m], axis=1))
            l_ref[...] += p[:, :half] + p[:, half:]
            vb = v_ref[pl.ds(pl.multiple_of(kb * KEY_BLOCK, KEY_BLOCK), KEY_BLOCK), :]
            acc_ref[...] += jnp.dot(p.astype(BF16), vb, preferred_element_type=F32)
            return carry

        lax.fori_loop(0, nkb, pv_body, 0)

        out = acc_ref[...] / jnp.sum(l_ref[...], axis=1, keepdims=True)
        for h in range(DSA_HEAD_GROUP):
            o_ref[:, (h0 + h) * A_HEAD_DIM:(h0 + h + 1) * A_HEAD_DIM] = out[h * tq:(h + 1) * tq].astype(o_ref.dtype)


def _dsa(proj, kbd, kt, tiles, bsz, seq, side_weight, layer, side_cols, side_weight2):
    tq = DSA_ROWS
    nb = seq // tq
    nkb = seq // KEY_BLOCK
    topk = min(TOPK_MAX, seq // 4)
    grows = DSA_HEAD_GROUP * tq
    av_blk = _NEW_OFF["a_v"] // A_HEAD_DIM
    kern = functools.partial(_dsa_kernel, nkb_max=nkb, topk=topk)
    expand = jnp.asarray(np.arange(IDX_HEADS)[:, None] == np.arange(IDX_HEADS * LANES)[None, :] // LANES, BF16)
    wd, wn = side_weight.shape[1], side_weight.shape[2]
    assert wd % (bsz * nb) == 0 and (wd // (bsz * nb)) % 16 == 0 and side_cols >= wn
    wr = wd // (bsz * nb)
    wd2, wn2 = side_weight2.shape[1], side_weight2.shape[2]
    assert wd2 % (bsz * nb) == 0 and (wd2 // (bsz * nb)) % 16 == 0
    wr2 = wd2 // (bsz * nb)
    once = pl.Buffered(1)
    return pl.pallas_call(
        kern,
        grid=(bsz, nb),
        in_specs=[
            pl.BlockSpec((tq, A_Q), lambda b, q: (b * nb + q, _NEW_OFF["a_q"] // A_Q)),
            pl.BlockSpec((tq, IDX_Q), lambda b, q: (b * nb + q, _NEW_OFF["i_q"] // IDX_Q)),
            pl.BlockSpec((tq, LANES), lambda b, q: (b * nb + q, SMALL_OFF // LANES)),
            pl.BlockSpec((None, nkb, 2 * IDX_DIM, 2 * KEY_BLOCK), lambda b, q: (b, 0, 0, 0)),
            pl.BlockSpec((None, nkb, A_HEAD_DIM, KEY_BLOCK), lambda b, q: (b, 0, 0, 0)),
            pl.BlockSpec((seq, A_HEAD_DIM), lambda b, q: (b, av_blk)),
            pl.BlockSpec((3, A_HEADS, Q_BLOCK, Q_BLOCK), lambda b, q: (0, 0, 0, 0), pipeline_mode=once),
            pl.BlockSpec((IDX_HEADS, IDX_HEADS * LANES), lambda b, q: (0, 0), pipeline_mode=once),
            pl.BlockSpec((None, wr, wn), lambda b, q: (layer, b * nb + q, 0)),
            pl.BlockSpec((None, wr2, wn2), lambda b, q: (layer, b * nb + q, 0)),
        ],
        out_specs=[pl.BlockSpec((tq, A_Q), lambda b, q: (b * nb + q, 0)),
                   pl.BlockSpec((wr, side_cols), lambda b, q: (b * nb + q, 0)),
                   pl.BlockSpec((wr2, wn2), lambda b, q: (b * nb + q, 0))],
        out_shape=[jax.ShapeDtypeStruct((bsz * seq, A_Q), BF16),
                   jax.ShapeDtypeStruct((wd, side_cols), BF16),
                   jax.ShapeDtypeStruct((wd2, wn2), BF16)],
        scratch_shapes=[
            pltpu.VMEM((nkb, tq, KEY_BLOCK), I32),
            pltpu.VMEM((nkb, tq, KEY_BLOCK), F32),
            pltpu.VMEM((IDX_HEADS, tq, LANES), F32),
            pltpu.VMEM((tq, LANES), I32),
            pltpu.VMEM((A_HEADS * tq, A_HEAD_DIM), BF16),
            pltpu.VMEM((A_HEADS * tq, LANES), BF16),
            pltpu.VMEM((grows, LANES), F32),
            pltpu.VMEM((grows, LANES), F32),
            pltpu.VMEM((grows, A_HEAD_DIM), F32),
            pltpu.VMEM((nkb, grows, KEY_BLOCK), F32),
        ],
        compiler_params=_cparams(("arbitrary", "arbitrary"), VMEM_LIMIT_FFN),
        name="dsa",
    )(proj, proj, proj, kbd, kt, proj, tiles, expand, side_weight, side_weight2)


def _gla_kernel(q_ref, k_ref, v_ref, r_ref, sm_ref, wg_ref, bg_ref, nrm_ref, tri_ref, sel_ref, wd_ref,
                o_ref, wd_o_ref, st_ref, oacc_ref, *, nchunk, side_rows):
    step = pl.program_id(0) * pl.num_programs(1) + pl.program_id(1)
    _side_cast(wd_ref, wd_o_ref, step * wd_ref.shape[0], side_rows)
    @pl.when(pl.program_id(1) == 0)
    def _():
        st_ref[...] = jnp.zeros(st_ref.shape, F32)

    ts = nchunk * CHUNK
    ri = lax.broadcasted_iota(I32, (ts, ts), 0)
    ci = lax.broadcasted_iota(I32, (ts, ts), 1)
    causal = jnp.logical_and(lax.shift_right_logical(ri, 6) == lax.shift_right_logical(ci, 6), ci <= ri)
    tn = (((0,), (0,)), ((), ()))
    nt = (((1,), (1,)), ((), ()))
    tri = tri_ref[...]
    sel = sel_ref[...]
    glr = sm_ref[:, GLR_LO:GLR_LO + GATE_RANK]

    heads = range(B_HEADS)
    ksl = [slice(h * B_HEAD_K, (h + 1) * B_HEAD_K) for h in heads]
    vsl = [slice(h * B_HEAD_V, (h + 1) * B_HEAD_V) for h in heads]
    qe, ke, kd, dcol = [], [], [], []

    def gates(h):
        ks = ksl[h]
        z = jnp.dot(glr, wg_ref[:, ks], preferred_element_type=F32) + bg_ref[:, ks]
        g = (jnp.minimum(z, 0.0) - jnp.log(1.0 + jnp.exp(-jnp.abs(z)))) * (1.0 / GATE_TAU)
        g_hi = g.astype(BF16)
        g_lo = (g - g_hi.astype(F32)).astype(BF16)
        b = jnp.dot(tri, g_hi, preferred_element_type=F32) + jnp.dot(tri, g_lo, preferred_element_type=F32)
        bl = jnp.concatenate(
            [jnp.broadcast_to(b[(c + 1) * CHUNK - 1:(c + 1) * CHUNK, :], (CHUNK, B_HEAD_K)) for c in range(nchunk)],
            axis=0)
        qc = q_ref[:, ks].astype(F32) * (B_HEAD_K ** -0.5)
        kc = k_ref[:, ks].astype(F32)
        qe.append((qc * jnp.exp(b)).astype(BF16))
        ke.append((kc * jnp.exp(-b)).astype(BF16))
        kd.append((kc * jnp.exp(bl - b)).astype(BF16))
        dcol.append(jnp.exp(lax.dot_general(g_hi, sel, tn, preferred_element_type=F32)
                            + lax.dot_general(g_lo, sel, tn, preferred_element_type=F32)))

    def intra(h):
        a = lax.dot_general(qe[h], ke[h], nt, preferred_element_type=F32)
        a = jnp.where(causal, a, 0.0).astype(BF16)
        oacc_ref[:, vsl[h]] = jnp.dot(a, v_ref[:, vsl[h]], preferred_element_type=F32)

    gates(0)
    for h in heads:
        if h + 1 < B_HEADS:
            gates(h + 1)
        intra(h)

    st = [st_ref[h] for h in heads]
    for c in range(nchunk):
        rs = slice(c * CHUNK, (c + 1) * CHUNK)
        for h in heads:
            oacc_ref[rs, vsl[h]] += jnp.dot(qe[h][rs], st[h].astype(BF16), preferred_element_type=F32)
            dec = jnp.broadcast_to(dcol[h][:, c:c + 1], (B_HEAD_K, B_HEAD_V))
            st[h] = st[h] * dec + lax.dot_general(kd[h][rs], v_ref[rs, vsl[h]], tn, preferred_element_type=F32)
    for h in heads:
        st_ref[h] = st[h]
        o = oacc_ref[:, vsl[h]]
        ms = jnp.mean(o * o, axis=-1, keepdims=True)
        on = o * lax.rsqrt(ms + EPS) * nrm_ref[...]
        rr = r_ref[:, vsl[h]].astype(F32)
        o_ref[:, vsl[h]] = (on * (rr * jax.nn.sigmoid(rr))).astype(o_ref.dtype)


def _gla(proj, w_g2, b_g2, gla_norm, bsz, seq, side_weight, layer, side_rows_padded):
    q_blk = _NEW_OFF["g_q"] // B_QK
    k_blk = _NEW_OFF["g_k"] // B_QK
    v_blk = _NEW_OFF["g_v"] // B_V
    r_blk = _NEW_OFF["g_r"] // B_V
    ts = min(seq, GLA_SEQ_TILE)
    nt = seq // ts
    nchunk = ts // CHUNK
    src_rows, wdim = side_weight.shape[1], side_weight.shape[2]
    nsteps = bsz * nt
    assert side_rows_padded % nsteps == 0 and (side_rows_padded // nsteps) % 16 == 0
    sr = side_rows_padded // nsteps
    last_src = (src_rows - 1) // sr
    kern = functools.partial(_gla_kernel, nchunk=nchunk, side_rows=src_rows)
    rc = np.arange(ts)[:, None] // CHUNK
    cc = np.arange(ts)[None, :] // CHUNK
    tri = jnp.asarray((rc == cc) & (np.arange(ts)[None, :] <= np.arange(ts)[:, None]), BF16)
    sel = jnp.asarray(rc == np.arange(LANES)[None, :], BF16)
    const = lambda shape: pl.BlockSpec(shape, lambda b, t: (0, 0))
    return pl.pallas_call(
        kern,
        grid=(bsz, nt),
        in_specs=[
            pl.BlockSpec((ts, B_QK), lambda b, t: (b * nt + t, q_blk)),
            pl.BlockSpec((ts, B_QK), lambda b, t: (b * nt + t, k_blk)),
            pl.BlockSpec((ts, B_V), lambda b, t: (b * nt + t, v_blk)),
            pl.BlockSpec((ts, B_V), lambda b, t: (b * nt + t, r_blk)),
            pl.BlockSpec((ts, LANES), lambda b, t: (b * nt + t, SMALL_OFF // LANES)),
            pl.BlockSpec((GATE_RANK, B_QK), lambda b, t: (0, 0)),
            pl.BlockSpec((1, B_QK), lambda b, t: (0, 0)),
            pl.BlockSpec((1, B_HEAD_V), lambda b, t: (0, 0)),
            const((ts, ts)), const((ts, LANES)),
            pl.BlockSpec((None, sr, wdim), lambda b, t: (layer, jnp.minimum(b * nt + t, last_src), 0)),
        ],
        out_specs=[pl.BlockSpec((ts, B_V), lambda b, t: (b * nt + t, 0)),
                   pl.BlockSpec((sr, wdim), lambda b, t: (b * nt + t, 0))],
        out_shape=[jax.ShapeDtypeStruct((bsz * seq, B_V), BF16),
                   jax.ShapeDtypeStruct((side_rows_padded, wdim), BF16)],
        scratch_shapes=[pltpu.VMEM((B_HEADS, B_HEAD_K, B_HEAD_V), F32),
                        pltpu.VMEM((ts, B_V), F32)],
        compiler_params=_cparams(("arbitrary", "arbitrary")),
        name="gla",
    )(proj, proj, proj, proj, proj, w_g2.astype(BF16), b_g2.reshape(1, -1), gla_norm.reshape(1, -1),
      tri, sel, side_weight)


def _outproj_kernel(oa_ref, ob_ref, w_ref, x_ref, modj_ref, mod_ref, g_ref, b_ref, x1_ref, h2_ref,
                    *, alpha, nj, tn):
    j = pl.program_id(1)
    a = jnp.concatenate([oa_ref[...], ob_ref[...]], axis=1)
    y = jnp.dot(a, w_ref[...], preferred_element_type=F32)
    z = alpha * x_ref[...] + modj_ref[2:3, :] * y
    for jj in range(nj):
        @pl.when(j == jj)
        def _(jj=jj):
            x1_ref[:, jj * tn:(jj + 1) * tn] = z

    @pl.when(j == nj - 1)
    def _():
        def rows(rs):
            x1 = _layer_norm(x1_ref[rs, :]) * g_ref[...] + b_ref[...]
            x1_ref[rs, :] = x1
            h2_ref[rs, :] = (_layer_norm(x1) * (1.0 + mod_ref[4:5, :]) + mod_ref[3:4, :]).astype(h2_ref.dtype)
        _for_row_chunks(x1_ref.shape[0], ROW_CHUNK, rows)


def _outproj(o_a, o_b, w, x2, mod3, ln_g, ln_b, seq, alpha):
    m, d = x2.shape
    tm, tn = 512, 1024
    nj = d // tn
    rows_per_seq = seq // tm
    kern = functools.partial(_outproj_kernel, alpha=alpha, nj=nj, tn=tn)
    return pl.pallas_call(
        kern,
        grid=(m // tm, nj),
        in_specs=[
            pl.BlockSpec((tm, A_Q), lambda i, j: (i, 0)),
            pl.BlockSpec((tm, B_V), lambda i, j: (i, 0)),
            pl.BlockSpec((A_Q + B_V, tn), lambda i, j: (0, j)),
            pl.BlockSpec((tm, tn), lambda i, j: (i, j)),
            pl.BlockSpec((None, 6, tn), lambda i, j: (i // rows_per_seq, 0, j)),
            pl.BlockSpec((None, 6, d), lambda i, j: (i // rows_per_seq, 0, 0)),
            pl.BlockSpec((1, d), lambda i, j: (0, 0)),
            pl.BlockSpec((1, d), lambda i, j: (0, 0)),
        ],
        out_specs=[pl.BlockSpec((tm, d), lambda i, j: (i, 0)),
                   pl.BlockSpec((tm, d), lambda i, j: (i, 0))],
        out_shape=[jax.ShapeDtypeStruct((m, d), F32), jax.ShapeDtypeStruct((m, d), BF16)],
        compiler_params=_cparams(("arbitrary", "arbitrary"), VMEM_LIMIT_FFN),
        name="outproj",
    )(o_a, o_b, w, x2, mod3, mod3, ln_g.reshape(1, d), ln_b.reshape(1, d))


def _ffn_kernel(h_ref, wu_ref, wg_ref, cw_ref, cb_ref, wd_ref, x1_ref, mod_ref, g_ref, b_ref, o_ref,
                carry_ref, act_ref, *, alpha, nj, rows_per_seq):
    i = pl.program_id(0)
    j = pl.program_id(1)

    def up_gate():
        h = h_ref[...]
        tm = h.shape[0]
        u = jnp.dot(h, wu_ref[...], preferred_element_type=F32)
        gt = jnp.dot(h, wg_ref[...], preferred_element_type=F32)
        first = (i % rows_per_seq) == 0
        prev = jnp.where(first, 0.0, carry_ref[j])
        carry_ref[j] = u[tm - 8:tm, :]
        rid = lax.broadcasted_iota(I32, u.shape, 0)
        p1 = prev[7:8, :]
        p2 = prev[6:7, :]
        u1 = jnp.where(rid == 0, p1, pltpu.roll(u, 1, axis=0))
        u2 = jnp.where(rid == 0, p2, jnp.where(rid == 1, p1, pltpu.roll(u, 2, axis=0)))
        cv = cw_ref[0:1, :] * u2 + cw_ref[1:2, :] * u1 + cw_ref[2:3, :] * u + cb_ref[...]
        cdf = 0.5 * (1.0 + jnp.tanh(math.sqrt(2.0 / math.pi) * (cv + 0.044715 * (cv ** 3))))
        return (cv * cdf * gt).astype(BF16)

    def down(act):
        for c0 in range(0, o_ref.shape[1], FFN_DOWN_CHUNK):
            cs = slice(c0, c0 + FFN_DOWN_CHUNK)
            o_ref[:, cs] += jnp.dot(act, wd_ref[:, cs], preferred_element_type=F32)

    @pl.when(j == 0)
    def _():
        o_ref[...] = jnp.zeros(o_ref.shape, F32)
        act_ref[...] = up_gate()

    @pl.when(jnp.logical_and(j > 0, j < nj))
    def _():
        act_prev = act_ref[...]
        act_ref[...] = up_gate()
        down(act_prev)

    @pl.when(j == nj)
    def _():
        down(act_ref[...])

        def rows(rs):
            z = alpha * x1_ref[rs, :] + mod_ref[5:6, :] * o_ref[rs, :]
            o_ref[rs, :] = _layer_norm(z) * g_ref[...] + b_ref[...]
        _for_row_chunks(o_ref.shape[0], ROW_CHUNK, rows)


def _ffn(h2, w_up, w_gate, conv_w, conv_b, w_down, x1, mod3, ln_g, ln_b, seq, alpha):
    m, d = h2.shape
    tm, tf = 512, FFN_TILE
    f = w_up.shape[1]
    pad = f - conv_w.shape[1]
    conv_w = jnp.pad(conv_w, ((0, 0), (0, pad)))
    conv_b = jnp.pad(conv_b, (0, pad))
    nj = f // tf
    rows_per_seq = seq // tm
    kern = functools.partial(_ffn_kernel, alpha=alpha, nj=nj, rows_per_seq=rows_per_seq)
    once = pl.Buffered(1)
    up_tile = lambda i, j: (0, jnp.minimum(j, nj - 1))
    return pl.pallas_call(
        kern,
        grid=(m // tm, nj + 1),
        in_specs=[
            pl.BlockSpec((tm, d), lambda i, j: (i, 0), pipeline_mode=once),
            pl.BlockSpec((d, tf), up_tile),
            pl.BlockSpec((d, tf), up_tile),
            pl.BlockSpec((CONV_W, tf), up_tile),
            pl.BlockSpec((1, tf), up_tile),
            pl.BlockSpec((tf, d), lambda i, j: (jnp.maximum(j - 1, 0), 0)),
            pl.BlockSpec((tm, d), lambda i, j: (i, 0), pipeline_mode=once),
            pl.BlockSpec((None, 6, d), lambda i, j: (i // rows_per_seq, 0, 0)),
            pl.BlockSpec((1, d), lambda i, j: (0, 0)),
            pl.BlockSpec((1, d), lambda i, j: (0, 0)),
        ],
        out_specs=pl.BlockSpec((tm, d), lambda i, j: (i, 0)),
        out_shape=jax.ShapeDtypeStruct((m, d), F32),
        scratch_shapes=[pltpu.VMEM((nj, 8, tf), F32),
                        pltpu.VMEM((tm, tf), BF16)],
        compiler_params=_cparams(("arbitrary", "arbitrary"), VMEM_LIMIT_FFN),
        name="ffn",
    )(h2, w_up, w_gate, conv_w, conv_b.reshape(1, f), w_down, x1, mod3, ln_g.reshape(1, d), ln_b.reshape(1, d))


def _regroup_kernel(w_ref, o_ref):
    for n in PROJ_ORDER:
        o_ref[_NEW_OFF[n]:_NEW_OFF[n] + _WIDTH[n], :] = w_ref[_OLD_OFF[n]:_OLD_OFF[n] + _WIDTH[n], :].astype(BF16)
    o_ref[_PROJ_USED:, :] = jnp.zeros((PROJ_COLS - _PROJ_USED, o_ref.shape[1]), BF16)


def _regroup_w_in(w, l):
    wt = jnp.swapaxes(w, 1, 2)
    _, n, d = wt.shape
    tc = 256
    return pl.pallas_call(
        _regroup_kernel,
        grid=(d // tc,),
        in_specs=[pl.BlockSpec((None, n, tc), lambda i: (l, 0, i))],
        out_specs=pl.BlockSpec((PROJ_COLS, tc), lambda i: (0, i)),
        out_shape=jax.ShapeDtypeStruct((PROJ_COLS, d), BF16),
        compiler_params=_cparams(("arbitrary",)),
        name="regroup_w_in",
    )(wt)


def kernel(x, c, t5_table, w_ada, b_ada, w_in, w_g2, b_g2, gla_norm, w_out, ln1_g, ln1_b, w_up, w_gate,
           conv_w, conv_b, w_down, ln2_g, ln2_b):
    bsz, seq, d = x.shape
    depth = w_ada.shape[0]
    alpha = (2 * depth) ** 0.25
    nkb = seq // KEY_BLOCK
    x2 = x.reshape(bsz * seq, d)
    tiles = _bias_tiles(t5_table)
    for l in range(depth):
        mod3 = _ada(c, w_ada[l], b_ada[l]).reshape(bsz, 6, d)
        f_pad = -(-w_up.shape[2] // FFN_TILE) * FFN_TILE
        proj, w_gate_b = _inproj(x2, mod3, _regroup_w_in(w_in, l), seq, w_gate, l, f_pad)

        p3 = proj.reshape(bsz, nkb, KEY_BLOCK, PROJ_COLS)
        ko, io = _NEW_OFF["a_k"], _NEW_OFF["i_k"]
        kt = jnp.swapaxes(p3[..., ko:ko + A_HEAD_DIM], 2, 3)
        kit = jnp.swapaxes(p3[..., io:io + IDX_DIM], 2, 3)
        zz = jnp.zeros_like(kit)
        kbd = jnp.concatenate([jnp.concatenate([kit, zz], axis=3),
                               jnp.concatenate([zz, kit], axis=3)], axis=2)

        o_a, w_up_b, w_out_b = _dsa(proj, kbd, kt, tiles, bsz, seq, w_up, l, f_pad, w_out)
        o_b, w_down_b = _gla(proj, w_g2[l], b_g2[l], gla_norm[l], bsz, seq, w_down, l, f_pad)
        x1, h2 = _outproj(o_a, o_b, w_out_b, x2, mod3, ln1_g[l], ln1_b[l], seq, alpha)
        x2 = _ffn(h2, w_up_b, w_gate_b, conv_w[l], conv_b[l], w_down_b, x1, mod3, ln2_g[l], ln2_b[l],
                  seq, alpha)
    return x2.reshape(bsz, seq, d)
```

```python
import functools
import math

import numpy as np
import jax
import jax.numpy as jnp
from jax import lax
from jax.experimental import pallas as pl
from jax.experimental.pallas import tpu as pltpu

F32 = jnp.float32
BF16 = jnp.bfloat16
I32 = jnp.int32

CHUNK = 64
Q_BLOCK = 128
A_HEADS = 16
A_HEAD_DIM = 128
IDX_HEADS = 32
IDX_DIM = 64
TOPK_MAX = 256
T5_BUCKETS = 32
T5_MAX_DIST = 128
B_HEADS = 4
B_HEAD_V = 512
B_HEAD_K = 256
GATE_RANK = 16
GATE_TAU = 16.0
CONV_W = 3
EPS = 1e-6

A_Q = A_HEADS * A_HEAD_DIM
IDX_Q = IDX_HEADS * IDX_DIM
B_QK = B_HEADS * B_HEAD_K
B_V = B_HEADS * B_HEAD_V
IN_SPLITS = (A_Q, A_HEAD_DIM, A_HEAD_DIM, IDX_Q, IDX_DIM, IDX_HEADS, B_QK, B_QK, B_V, GATE_RANK, B_V)
IN_NAMES = ("a_q", "a_k", "a_v", "i_q", "i_k", "i_w", "g_q", "g_k", "g_v", "g_lr", "g_r")
PROJ_ORDER = ("a_q", "i_q", "g_v", "g_r", "g_q", "g_k", "a_k", "a_v", "i_k", "i_w", "g_lr")
PROJ_TILE = 1536

LANES = 128
VMEM_LIMIT = 56 * 1024 * 1024
VMEM_LIMIT_FFN = 60 * 1024 * 1024

KEY_BLOCK = 2 * Q_BLOCK
DSA_ROWS = KEY_BLOCK
DSA_HEAD_GROUP = 8
NEG_MASK = -1e30
LOG2E = math.log2(math.e)
INT_MIN = -(2 ** 31)


def _proj_layout():
    old_off = dict(zip(IN_NAMES, np.concatenate([[0], np.cumsum(IN_SPLITS)[:-1]]).tolist()))
    width = dict(zip(IN_NAMES, IN_SPLITS))
    new_off, pos = {}, 0
    for name in PROJ_ORDER:
        new_off[name] = pos
        pos += width[name]
    total = -(-pos // PROJ_TILE) * PROJ_TILE
    return old_off, width, new_off, pos, total


_OLD_OFF, _WIDTH, _NEW_OFF, _PROJ_USED, PROJ_COLS = _proj_layout()
SMALL_OFF = _NEW_OFF["i_k"]
IK_LO = 0
IW_LO = _NEW_OFF["i_w"] - SMALL_OFF
GLR_LO = _NEW_OFF["g_lr"] - SMALL_OFF


def _layer_norm(x):
    mu = jnp.mean(x, axis=-1, keepdims=True)
    xc = x - mu
    var = jnp.mean(xc * xc, axis=-1, keepdims=True)
    return xc * lax.rsqrt(var + EPS)


def _for_row_chunks(nrows, chunk, fn):
    def body(r, carry):
        fn(pl.ds(pl.multiple_of(r * chunk, chunk), chunk))
        return carry
    lax.fori_loop(0, nrows // chunk, body, 0)


ROW_CHUNK = 64
FFN_TILE = 512
GLA_SEQ_TILE = 512
FFN_DOWN_CHUNK = 512


def _cparams(sem, vmem=VMEM_LIMIT):
    return pltpu.CompilerParams(dimension_semantics=sem, vmem_limit_bytes=vmem)


def _ada_kernel(c_ref, w_ref, b_ref, o_ref):
    c = c_ref[...]
    ca = (c * jax.nn.sigmoid(c)).astype(BF16)
    o_ref[...] = jnp.dot(ca, w_ref[...].astype(BF16), preferred_element_type=F32) + b_ref[...]


def _ada(c, w, b):
    bsz, d = c.shape
    n = w.shape[1]
    tn = 512
    return pl.pallas_call(
        _ada_kernel,
        grid=(n // tn,),
        in_specs=[pl.BlockSpec((bsz, d), lambda j: (0, 0)),
                  pl.BlockSpec((d, tn), lambda j: (0, j)),
                  pl.BlockSpec((1, tn), lambda j: (0, j))],
        out_specs=pl.BlockSpec((bsz, tn), lambda j: (0, j)),
        out_shape=jax.ShapeDtypeStruct((bsz, n), F32),
        compiler_params=_cparams(("arbitrary",)),
        name="ada",
    )(c, w, b.reshape(1, n))


def _inproj_kernel(x_ref, mod_ref, w_ref, ws_ref, o_ref, ws_o_ref, h_ref):
    _side_cast(ws_ref, ws_o_ref)

    @pl.when(pl.program_id(1) == 0)
    def _():
        def rows(rs):
            xn = _layer_norm(x_ref[rs, :])
            h_ref[rs, :] = (xn * (1.0 + mod_ref[1:2, :]) + mod_ref[0:1, :]).astype(BF16)
        _for_row_chunks(x_ref.shape[0], ROW_CHUNK, rows)

    o_ref[...] = lax.dot_general(h_ref[...], w_ref[...], (((1,), (1,)), ((), ())),
                                 preferred_element_type=F32).astype(o_ref.dtype)


def _side_block_rows(total, nsteps):
    return next(r for r in range(16, total + 1, 16) if total % r == 0 and total // r <= nsteps)


def _inproj(x2, mod3, wt, seq, side_weight, layer, side_cols):
    m, d = x2.shape
    n = wt.shape[0]
    tm, tn = 512, PROJ_TILE
    nj = n // tn
    rows_per_seq = seq // tm
    wd, wn = side_weight.shape[1], side_weight.shape[2]
    wr = _side_block_rows(wd, (m // tm) * nj)
    blk = lambda i, j: jnp.minimum(i * nj + j, wd // wr - 1)
    return pl.pallas_call(
        _inproj_kernel,
        grid=(m // tm, nj),
        in_specs=[pl.BlockSpec((tm, d), lambda i, j: (i, 0)),
                  pl.BlockSpec((None, 6, d), lambda i, j: (i // rows_per_seq, 0, 0)),
                  pl.BlockSpec((tn, d), lambda i, j: (j, 0)),
                  pl.BlockSpec((None, wr, wn), lambda i, j: (layer, blk(i, j), 0))],
        out_specs=[pl.BlockSpec((tm, tn), lambda i, j: (i, j)),
                   pl.BlockSpec((wr, side_cols), lambda i, j: (blk(i, j), 0))],
        out_shape=[jax.ShapeDtypeStruct((m, n), BF16), jax.ShapeDtypeStruct((wd, side_cols), BF16)],
        scratch_shapes=[pltpu.VMEM((tm, d), BF16)],
        compiler_params=_cparams(("arbitrary", "arbitrary"), VMEM_LIMIT_FFN),
        name="inproj",
    )(x2, mod3, wt, side_weight)


def _t5_bucket(rel):
    half = T5_BUCKETS // 2
    max_exact = half // 2
    ret = jnp.where(rel > 0, half, 0)
    n = jnp.abs(rel)
    nf = jnp.maximum(n, 1).astype(jnp.float32)
    large = max_exact + (jnp.log(nf / max_exact) / math.log(T5_MAX_DIST / max_exact)
                         * (half - max_exact)).astype(jnp.int32)
    large = jnp.minimum(large, half - 1)
    return ret + jnp.where(n < max_exact, n, large)


def _bias_kernel(tab_ref, bkt_ref, o_ref):
    for u in range(3):
        bk = bkt_ref[u]
        for h in range(A_HEADS):
            acc = jnp.zeros((Q_BLOCK, Q_BLOCK), F32)
            for b in range(T5_BUCKETS):
                acc = jnp.where(bk == b, tab_ref[b, h], acc)
            o_ref[u, h] = acc * LOG2E


def _bias_tiles(t5_table):
    i = jnp.arange(Q_BLOCK, dtype=I32)[:, None]
    j = jnp.arange(Q_BLOCK, dtype=I32)[None, :]
    rel = jnp.stack([j - i - 2 * Q_BLOCK, j - i - Q_BLOCK, j - i])
    bkt = _t5_bucket(rel).astype(I32)
    return pl.pallas_call(
        _bias_kernel,
        in_specs=[pl.BlockSpec(memory_space=pltpu.SMEM),
                  pl.BlockSpec((3, Q_BLOCK, Q_BLOCK), lambda: (0, 0, 0))],
        out_specs=pl.BlockSpec((3, A_HEADS, Q_BLOCK, Q_BLOCK), lambda: (0, 0, 0, 0)),
        out_shape=jax.ShapeDtypeStruct((3, A_HEADS, Q_BLOCK, Q_BLOCK), F32),
        name="t5_bias",
    )(t5_table, bkt)


def _side_cast(src_ref, dst_ref, row0=None, rows_valid=None):
    r, n = src_ref.shape
    w = src_ref[...]
    if rows_valid is not None:
        rid = lax.broadcasted_iota(I32, (r, n), 0) + row0
        w = jnp.where(rid < rows_valid, w, 0.0)
    dst_ref[:, :n] = w.astype(BF16)
    if dst_ref.shape[1] > n:
        dst_ref[:, n:] = jnp.zeros((r, dst_ref.shape[1] - n), BF16)


def _dsa_kernel(aq_ref, iq_ref, sm_ref, kbd_ref, kt_ref, v_ref, tiles_ref, exp_ref, wa_ref, wc_ref,
                o_ref, wa_o_ref, wc_o_ref,
                key_ref, nm_ref, wb_ref, t_ref, q_ref, lhs_ref, m_ref, l_ref, acc_ref, x_ref,
                *, nkb_max, topk):
    _side_cast(wa_ref, wa_o_ref)
    _side_cast(wc_ref, wc_o_ref)
    pb = pl.program_id(1)
    nkb = pb + 1
    tq = DSA_ROWS
    grows = DSA_HEAD_GROUP * tq
    half = KEY_BLOCK // 2
    pairs_per_dot = 8

    iw = jnp.dot(sm_ref[:, IW_LO:IW_LO + IDX_HEADS], exp_ref[...], preferred_element_type=F32)
    for hh in range(IDX_HEADS):
        wb_ref[hh] = iw[:, hh * LANES:(hh + 1) * LANES] * (IDX_DIM ** -0.5 * IDX_HEADS ** -0.5)
    for p in range(IDX_HEADS // 2):
        lhs_ref[p * tq:(p + 1) * tq, :] = iq_ref[:, p * LANES:(p + 1) * LANES]
    for h in range(A_HEADS):
        q_ref[h * tq:(h + 1) * tq, :] = aq_ref[:, h * LANES:(h + 1) * LANES]

    row_chunk = lax.shift_right_logical(lax.broadcasted_iota(I32, (tq, KEY_BLOCK), 0) + pb * tq, 6)
    col_iota = lax.broadcasted_iota(I32, (tq, KEY_BLOCK), 1)

    def idx_body(kb, carry):
        acc = jnp.zeros((tq, KEY_BLOCK), F32)
        for p0 in range(0, IDX_HEADS // 2, pairs_per_dot):
            res = jnp.dot(lhs_ref[p0 * tq:(p0 + pairs_per_dot) * tq, :], kbd_ref[kb],
                          preferred_element_type=F32)
            for p in range(pairs_per_dot):
                r = res[p * tq:(p + 1) * tq]
                we = wb_ref[2 * (p0 + p)]
                wo = wb_ref[2 * (p0 + p) + 1]
                acc = acc + jnp.maximum(r[:, :KEY_BLOCK], 0.0) * jnp.concatenate([we, we], axis=1)
                acc = acc + jnp.maximum(r[:, KEY_BLOCK:], 0.0) * jnp.concatenate([wo, wo], axis=1)
        bits = pltpu.bitcast(acc, I32)
        skey = bits ^ (lax.shift_right_arithmetic(bits, 31) & 0x7FFFFFFF)
        skey = jnp.where(acc == 0.0, 0, skey)
        adm = lax.shift_right_logical(col_iota + kb * KEY_BLOCK, 6) <= row_chunk
        key_ref[kb] = jnp.where(adm, skey, INT_MIN)
        return carry

    lax.fori_loop(0, nkb, idx_body, 0)

    def search(n):
        def count_ge(cand):
            c = jnp.zeros((tq, LANES), F32)
            for kb in range(n):
                k = key_ref[kb]
                c = c + jnp.where(k[:, :half] >= cand, 1.0, 0.0) + jnp.where(k[:, half:] >= cand, 1.0, 0.0)
            return jnp.broadcast_to(jnp.sum(c, axis=1, keepdims=True), (tq, LANES))

        zero = jnp.zeros((tq, LANES), I32)
        t0 = jnp.where(count_ge(zero) >= topk, zero, INT_MIN)

        def body(i, t):
            cand = t + lax.shift_left(jnp.int32(1), 30 - i)
            return jnp.where(count_ge(cand) >= topk, cand, t)

        t = lax.fori_loop(0, 31, body, t0)
        t_ref[...] = jnp.maximum(t, INT_MIN + 1)

    all_selected = (pb + 1) * tq <= topk

    @pl.when(all_selected)
    def _():
        t_ref[...] = jnp.full((tq, LANES), INT_MIN + 1, I32)

    for n in range(1, nkb_max + 1):
        pl.when(jnp.logical_and(nkb == n, jnp.logical_not(all_selected)))(functools.partial(search, n))

    thr = t_ref[...]
    thr2 = jnp.concatenate([thr, thr], axis=1)

    def cnt_body(kb, c):
        g = jnp.where(key_ref[kb] >= thr2, 1.0, 0.0)
        return c + g[:, :half] + g[:, half:]

    n_ge = jnp.sum(lax.fori_loop(0, nkb, cnt_body, jnp.zeros((tq, LANES), F32)),
                   axis=1, keepdims=True)
    has_ties = jnp.max(n_ge) > topk

    @pl.when(jnp.logical_not(has_ties))
    def _():
        def body(kb, carry):
            nm_ref[kb] = jnp.where(key_ref[kb] >= thr2, 0.0, NEG_MASK)
            return carry
        lax.fori_loop(0, nkb, body, 0)

    @pl.when(has_ties)
    def _():
        def gt_body(kb, c):
            g = jnp.where(key_ref[kb] > thr2, 1.0, 0.0)
            return c + g[:, :half] + g[:, half:]
        n_gt = jnp.sum(lax.fori_loop(0, nkb, gt_body, jnp.zeros((tq, LANES), F32)),
                       axis=1, keepdims=True)
        need = topk - n_gt
        tri = (lax.broadcasted_iota(I32, (half, half), 0)
               <= lax.broadcasted_iota(I32, (half, half), 1)).astype(BF16)

        def body(kb, seen):
            k = key_ref[kb]
            parts = []
            for s in range(2):
                ks = k[:, s * half:(s + 1) * half]
                eq = ks == thr
                eqf = jnp.where(eq, 1.0, 0.0)
                rank = jnp.dot(eqf.astype(BF16), tri, preferred_element_type=F32) + seen
                keep = jnp.logical_or(ks > thr, jnp.logical_and(eq, rank <= need))
                parts.append(jnp.where(keep, 0.0, NEG_MASK))
                seen = seen + jnp.sum(eqf, axis=1, keepdims=True)
            nm_ref[kb] = jnp.concatenate(parts, axis=1)
            return seen
        lax.fori_loop(0, nkb, body, jnp.zeros((tq, 1), F32))

    scale = A_HEAD_DIM ** -0.5 * LOG2E

    def tile_of(g, qblk):
        return jnp.where(g == qblk, 2, jnp.where(g == qblk - 1, 1, 0))

    for h0 in range(0, A_HEADS, DSA_HEAD_GROUP):
        hs = slice(h0, h0 + DSA_HEAD_GROUP)
        qrows = slice(h0 * tq, (h0 + DSA_HEAD_GROUP) * tq)
        m_ref[...] = jnp.full((grows, LANES), NEG_MASK, F32)
        l_ref[...] = jnp.zeros((grows, LANES), F32)
        acc_ref[...] = jnp.zeros((grows, A_HEAD_DIM), F32)

        def max_body(kb, carry, hs=hs, qrows=qrows):
            x = jnp.dot(q_ref[qrows, :], kt_ref[kb], preferred_element_type=F32) * scale
            per_qblk = []
            for r in range(tq // Q_BLOCK):
                qblk = (tq // Q_BLOCK) * pb + r
                per_qblk.append(jnp.concatenate([tiles_ref[tile_of(2 * kb, qblk), hs],
                                                 tiles_ref[tile_of(2 * kb + 1, qblk), hs]], axis=2))
            bias = jnp.concatenate(per_qblk, axis=1)
            x = (x.reshape(DSA_HEAD_GROUP, tq, KEY_BLOCK) + bias + nm_ref[kb][None]).reshape(grows, KEY_BLOCK)
            x_ref[kb] = x
            m_ref[...] = jnp.maximum(m_ref[...], jnp.maximum(x[:, :half], x[:, half:]))
            return carry

        lax.fori_loop(0, nkb, max_body, 0)
        m_ref[...] = jnp.broadcast_to(jnp.max(m_ref[...], axis=1, keepdims=True), (grows, LANES))

        def pv_body(kb, carry):
            m = m_ref[...]
            p = jnp.exp2(x_ref[kb] - jnp.concatenate([m, m], axis=1))
            l_ref[...] += p[:, :half] + p[:, half:]
            vb = v_ref[pl.ds(pl.multiple_of(kb * KEY_BLOCK, KEY_BLOCK), KEY_BLOCK), :]
            acc_ref[...] += jnp.dot(p.astype(BF16), vb, preferred_element_type=F32)
            return carry

        lax.fori_loop(0, nkb, pv_body, 0)

        out = acc_ref[...] / jnp.sum(l_ref[...], axis=1, keepdims=True)
        for h in range(DSA_HEAD_GROUP):
            o_ref[:, (h0 + h) * A_HEAD_DIM:(h0 + h + 1) * A_HEAD_DIM] = out[h * tq:(h + 1) * tq].astype(o_ref.dtype)


def _dsa(proj, kbd, kt, tiles, bsz, seq, side_weight, layer, side_cols, side_weight2):
    tq = DSA_ROWS
    nb = seq // tq
    nkb = seq // KEY_BLOCK
    topk = min(TOPK_MAX, seq // 4)
    grows = DSA_HEAD_GROUP * tq
    av_blk = _NEW_OFF["a_v"] // A_HEAD_DIM
    kern = functools.partial(_dsa_kernel, nkb_max=nkb, topk=topk)
    expand = jnp.asarray(np.arange(IDX_HEADS)[:, None] == np.arange(IDX_HEADS * LANES)[None, :] // LANES, BF16)
    wd, wn = side_weight.shape[1], side_weight.shape[2]
    assert wd % (bsz * nb) == 0 and (wd // (bsz * nb)) % 16 == 0 and side_cols >= wn
    wr = wd // (bsz * nb)
    wd2, wn2 = side_weight2.shape[1], side_weight2.shape[2]
    assert wd2 % (bsz * nb) == 0 and (wd2 // (bsz * nb)) % 16 == 0
    wr2 = wd2 // (bsz * nb)
    once = pl.Buffered(1)
    return pl.pallas_call(
        kern,
        grid=(bsz, nb),
        in_specs=[
            pl.BlockSpec((tq, A_Q), lambda b, q: (b * nb + q, _NEW_OFF["a_q"] // A_Q)),
            pl.BlockSpec((tq, IDX_Q), lambda b, q: (b * nb + q, _NEW_OFF["i_q"] // IDX_Q)),
            pl.BlockSpec((tq, LANES), lambda b, q: (b * nb + q, SMALL_OFF // LANES)),
            pl.BlockSpec((None, nkb, 2 * IDX_DIM, 2 * KEY_BLOCK), lambda b, q: (b, 0, 0, 0)),
            pl.BlockSpec((None, nkb, A_HEAD_DIM, KEY_BLOCK), lambda b, q: (b, 0, 0, 0)),
            pl.BlockSpec((seq, A_HEAD_DIM), lambda b, q: (b, av_blk)),
            pl.BlockSpec((3, A_HEADS, Q_BLOCK, Q_BLOCK), lambda b, q: (0, 0, 0, 0), pipeline_mode=once),
            pl.BlockSpec((IDX_HEADS, IDX_HEADS * LANES), lambda b, q: (0, 0), pipeline_mode=once),
            pl.BlockSpec((None, wr, wn), lambda b, q: (layer, b * nb + q, 0)),
            pl.BlockSpec((None, wr2, wn2), lambda b, q: (layer, b * nb + q, 0)),
        ],
        out_specs=[pl.BlockSpec((tq, A_Q), lambda b, q: (b * nb + q, 0)),
                   pl.BlockSpec((wr, side_cols), lambda b, q: (b * nb + q, 0)),
                   pl.BlockSpec((wr2, wn2), lambda b, q: (b * nb + q, 0))],
        out_shape=[jax.ShapeDtypeStruct((bsz * seq, A_Q), BF16),
                   jax.ShapeDtypeStruct((wd, side_cols), BF16),
                   jax.ShapeDtypeStruct((wd2, wn2), BF16)],
        scratch_shapes=[
            pltpu.VMEM((nkb, tq, KEY_BLOCK), I32),
            pltpu.VMEM((nkb, tq, KEY_BLOCK), F32),
            pltpu.VMEM((IDX_HEADS, tq, LANES), F32),
            pltpu.VMEM((tq, LANES), I32),
            pltpu.VMEM((A_HEADS * tq, A_HEAD_DIM), BF16),
            pltpu.VMEM((A_HEADS * tq, LANES), BF16),
            pltpu.VMEM((grows, LANES), F32),
            pltpu.VMEM((grows, LANES), F32),
            pltpu.VMEM((grows, A_HEAD_DIM), F32),
            pltpu.VMEM((nkb, grows, KEY_BLOCK), F32),
        ],
        compiler_params=_cparams(("arbitrary", "arbitrary"), VMEM_LIMIT_FFN),
        name="dsa",
    )(proj, proj, proj, kbd, kt, proj, tiles, expand, side_weight, side_weight2)


def _gla_kernel(q_ref, k_ref, v_ref, r_ref, sm_ref, wg_ref, bg_ref, nrm_ref, tri_ref, sel_ref, wd_ref,
                o_ref, wd_o_ref, st_ref, oacc_ref, *, nchunk, side_rows):
    step = pl.program_id(0) * pl.num_programs(1) + pl.program_id(1)
    _side_cast(wd_ref, wd_o_ref, step * wd_ref.shape[0], side_rows)
    @pl.when(pl.program_id(1) == 0)
    def _():
        st_ref[...] = jnp.zeros(st_ref.shape, F32)

    ts = nchunk * CHUNK
    ri = lax.broadcasted_iota(I32, (ts, ts), 0)
    ci = lax.broadcasted_iota(I32, (ts, ts), 1)
    causal = jnp.logical_and(lax.shift_right_logical(ri, 6) == lax.shift_right_logical(ci, 6), ci <= ri)
    tn = (((0,), (0,)), ((), ()))
    nt = (((1,), (1,)), ((), ()))
    tri = tri_ref[...]
    sel = sel_ref[...]
    glr = sm_ref[:, GLR_LO:GLR_LO + GATE_RANK]

    heads = range(B_HEADS)
    ksl = [slice(h * B_HEAD_K, (h + 1) * B_HEAD_K) for h in heads]
    vsl = [slice(h * B_HEAD_V, (h + 1) * B_HEAD_V) for h in heads]
    qe, ke, kd, dcol = [], [], [], []

    def gates(h):
        ks = ksl[h]
        z = jnp.dot(glr, wg_ref[:, ks], preferred_element_type=F32) + bg_ref[:, ks]
        g = (jnp.minimum(z, 0.0) - jnp.log(1.0 + jnp.exp(-jnp.abs(z)))) * (1.0 / GATE_TAU)
        g_hi = g.astype(BF16)
        g_lo = (g - g_hi.astype(F32)).astype(BF16)
        b = jnp.dot(tri, g_hi, preferred_element_type=F32) + jnp.dot(tri, g_lo, preferred_element_type=F32)
        bl = jnp.concatenate(
            [jnp.broadcast_to(b[(c + 1) * CHUNK - 1:(c + 1) * CHUNK, :], (CHUNK, B_HEAD_K)) for c in range(nchunk)],
            axis=0)
        qc = q_ref[:, ks].astype(F32) * (B_HEAD_K ** -0.5)
        kc = k_ref[:, ks].astype(F32)
        qe.append((qc * jnp.exp(b)).astype(BF16))
        ke.append((kc * jnp.exp(-b)).astype(BF16))
        kd.append((kc * jnp.exp(bl - b)).astype(BF16))
        dcol.append(jnp.exp(lax.dot_general(g_hi, sel, tn, preferred_element_type=F32)
                            + lax.dot_general(g_lo, sel, tn, preferred_element_type=F32)))

    def intra(h):
        a = lax.dot_general(qe[h], ke[h], nt, preferred_element_type=F32)
        a = jnp.where(causal, a, 0.0).astype(BF16)
        oacc_ref[:, vsl[h]] = jnp.dot(a, v_ref[:, vsl[h]], preferred_element_type=F32)

    gates(0)
    for h in heads:
        if h + 1 < B_HEADS:
            gates(h + 1)
        intra(h)

    st = [st_ref[h] for h in heads]
    for c in range(nchunk):
        rs = slice(c * CHUNK, (c + 1) * CHUNK)
        for h in heads:
            oacc_ref[rs, vsl[h]] += jnp.dot(qe[h][rs], st[h].astype(BF16), preferred_element_type=F32)
            dec = jnp.broadcast_to(dcol[h][:, c:c + 1], (B_HEAD_K, B_HEAD_V))
            st[h] = st[h] * dec + lax.dot_general(kd[h][rs], v_ref[rs, vsl[h]], tn, preferred_element_type=F32)
    for h in heads:
        st_ref[h] = st[h]
        o = oacc_ref[:, vsl[h]]
        ms = jnp.mean(o * o, axis=-1, keepdims=True)
        on = o * lax.rsqrt(ms + EPS) * nrm_ref[...]
        rr = r_ref[:, vsl[h]].astype(F32)
        o_ref[:, vsl[h]] = (on * (rr * jax.nn.sigmoid(rr))).astype(o_ref.dtype)


def _gla(proj, w_g2, b_g2, gla_norm, bsz, seq, side_weight, layer, side_rows_padded):
    q_blk = _NEW_OFF["g_q"] // B_QK
    k_blk = _NEW_OFF["g_k"] // B_QK
    v_blk = _NEW_OFF["g_v"] // B_V
    r_blk = _NEW_OFF["g_r"] // B_V
    ts = min(seq, GLA_SEQ_TILE)
    nt = seq // ts
    nchunk = ts // CHUNK
    src_rows, wdim = side_weight.shape[1], side_weight.shape[2]
    nsteps = bsz * nt
    assert side_rows_padded % nsteps == 0 and (side_rows_padded // nsteps) % 16 == 0
    sr = side_rows_padded // nsteps
    last_src = (src_rows - 1) // sr
    kern = functools.partial(_gla_kernel, nchunk=nchunk, side_rows=src_rows)
    rc = np.arange(ts)[:, None] // CHUNK
    cc = np.arange(ts)[None, :] // CHUNK
    tri = jnp.asarray((rc == cc) & (np.arange(ts)[None, :] <= np.arange(ts)[:, None]), BF16)
    sel = jnp.asarray(rc == np.arange(LANES)[None, :], BF16)
    const = lambda shape: pl.BlockSpec(shape, lambda b, t: (0, 0))
    return pl.pallas_call(
        kern,
        grid=(bsz, nt),
        in_specs=[
            pl.BlockSpec((ts, B_QK), lambda b, t: (b * nt + t, q_blk)),
            pl.BlockSpec((ts, B_QK), lambda b, t: (b * nt + t, k_blk)),
            pl.BlockSpec((ts, B_V), lambda b, t: (b * nt + t, v_blk)),
            pl.BlockSpec((ts, B_V), lambda b, t: (b * nt + t, r_blk)),
            pl.BlockSpec((ts, LANES), lambda b, t: (b * nt + t, SMALL_OFF // LANES)),
            pl.BlockSpec((GATE_RANK, B_QK), lambda b, t: (0, 0)),
            pl.BlockSpec((1, B_QK), lambda b, t: (0, 0)),
            pl.BlockSpec((1, B_HEAD_V), lambda b, t: (0, 0)),
            const((ts, ts)), const((ts, LANES)),
            pl.BlockSpec((None, sr, wdim), lambda b, t: (layer, jnp.minimum(b * nt + t, last_src), 0)),
        ],
        out_specs=[pl.BlockSpec((ts, B_V), lambda b, t: (b * nt + t, 0)),
                   pl.BlockSpec((sr, wdim), lambda b, t: (b * nt + t, 0))],
        out_shape=[jax.ShapeDtypeStruct((bsz * seq, B_V), BF16),
                   jax.ShapeDtypeStruct((side_rows_padded, wdim), BF16)],
        scratch_shapes=[pltpu.VMEM((B_HEADS, B_HEAD_K, B_HEAD_V), F32),
                        pltpu.VMEM((ts, B_V), F32)],
        compiler_params=_cparams(("arbitrary", "arbitrary")),
        name="gla",
    )(proj, proj, proj, proj, proj, w_g2.astype(BF16), b_g2.reshape(1, -1), gla_norm.reshape(1, -1),
      tri, sel, side_weight)


def _outproj_kernel(oa_ref, ob_ref, w_ref, x_ref, modj_ref, mod_ref, g_ref, b_ref, x1_ref, h2_ref,
                    *, alpha, nj, tn):
    j = pl.program_id(1)
    a = jnp.concatenate([oa_ref[...], ob_ref[...]], axis=1)
    y = jnp.dot(a, w_ref[...], preferred_element_type=F32)
    z = alpha * x_ref[...] + modj_ref[2:3, :] * y
    for jj in range(nj):
        @pl.when(j == jj)
        def _(jj=jj):
            x1_ref[:, jj * tn:(jj + 1) * tn] = z

    @pl.when(j == nj - 1)
    def _():
        def rows(rs):
            x1 = _layer_norm(x1_ref[rs, :]) * g_ref[...] + b_ref[...]
            x1_ref[rs, :] = x1
            h2_ref[rs, :] = (_layer_norm(x1) * (1.0 + mod_ref[4:5, :]) + mod_ref[3:4, :]).astype(h2_ref.dtype)
        _for_row_chunks(x1_ref.shape[0], ROW_CHUNK, rows)


def _outproj(o_a, o_b, w, x2, mod3, ln_g, ln_b, seq, alpha):
    m, d = x2.shape
    tm, tn = 512, 1024
    nj = d // tn
    rows_per_seq = seq // tm
    kern = functools.partial(_outproj_kernel, alpha=alpha, nj=nj, tn=tn)
    return pl.pallas_call(
        kern,
        grid=(m // tm, nj),
        in_specs=[
            pl.BlockSpec((tm, A_Q), lambda i, j: (i, 0)),
            pl.BlockSpec((tm, B_V), lambda i, j: (i, 0)),
            pl.BlockSpec((A_Q + B_V, tn), lambda i, j: (0, j)),
            pl.BlockSpec((tm, tn), lambda i, j: (i, j)),
            pl.BlockSpec((None, 6, tn), lambda i, j: (i // rows_per_seq, 0, j)),
            pl.BlockSpec((None, 6, d), lambda i, j: (i // rows_per_seq, 0, 0)),
            pl.BlockSpec((1, d), lambda i, j: (0, 0)),
            pl.BlockSpec((1, d), lambda i, j: (0, 0)),
        ],
        out_specs=[pl.BlockSpec((tm, d), lambda i, j: (i, 0)),
                   pl.BlockSpec((tm, d), lambda i, j: (i, 0))],
        out_shape=[jax.ShapeDtypeStruct((m, d), F32), jax.ShapeDtypeStruct((m, d), BF16)],
        compiler_params=_cparams(("arbitrary", "arbitrary"), VMEM_LIMIT_FFN),
        name="outproj",
    )(o_a, o_b, w, x2, mod3, mod3, ln_g.reshape(1, d), ln_b.reshape(1, d))


def _ffn_kernel(h_ref, wu_ref, wg_ref, cw_ref, cb_ref, wd_ref, x1_hbm, mod_ref, g_ref, b_ref, o_ref,
                carry_ref, act_ref, xbuf_ref, xsem, *, alpha, nj, rows_per_seq):
    i = pl.program_id(0)
    j = pl.program_id(1)
    tm = o_ref.shape[0]
    nchunks = tm // ROW_CHUNK

    def x1_copy(c, slot):
        row0 = pl.multiple_of(i * tm + c * ROW_CHUNK, ROW_CHUNK)
        return pltpu.make_async_copy(x1_hbm.at[pl.ds(row0, ROW_CHUNK), :], xbuf_ref.at[slot], xsem.at[slot])

    def up_gate():
        h = h_ref[...]
        tm = h.shape[0]
        u = jnp.dot(h, wu_ref[...], preferred_element_type=F32)
        gt = jnp.dot(h, wg_ref[...], preferred_element_type=F32)
        first = (i % rows_per_seq) == 0
        prev = jnp.where(first, 0.0, carry_ref[j])
        carry_ref[j] = u[tm - 8:tm, :]
        rid = lax.broadcasted_iota(I32, u.shape, 0)
        p1 = prev[7:8, :]
        p2 = prev[6:7, :]
        u1 = jnp.where(rid == 0, p1, pltpu.roll(u, 1, axis=0))
        u2 = jnp.where(rid == 0, p2, jnp.where(rid == 1, p1, pltpu.roll(u, 2, axis=0)))
        cv = cw_ref[0:1, :] * u2 + cw_ref[1:2, :] * u1 + cw_ref[2:3, :] * u + cb_ref[...]
        cdf = 0.5 * (1.0 + jnp.tanh(math.sqrt(2.0 / math.pi) * (cv + 0.044715 * (cv ** 3))))
        return (cv * cdf * gt).astype(BF16)

    def down(act):
        for c0 in range(0, o_ref.shape[1], FFN_DOWN_CHUNK):
            cs = slice(c0, c0 + FFN_DOWN_CHUNK)
            o_ref[:, cs] += jnp.dot(act, wd_ref[:, cs], preferred_element_type=F32)

    @pl.when(j == 0)
    def _():
        o_ref[...] = jnp.zeros(o_ref.shape, F32)
        act_ref[...] = up_gate()

    @pl.when(jnp.logical_and(j > 0, j < nj))
    def _():
        act_prev = act_ref[...]
        act_ref[...] = up_gate()
        down(act_prev)

    @pl.when(j == nj - 1)
    def _():
        x1_copy(0, 0).start()

    @pl.when(j == nj)
    def _():
        down(act_ref[...])

        def body(c, carry):
            slot = c % 2

            @pl.when(c + 1 < nchunks)
            def _():
                x1_copy(c + 1, 1 - slot).start()

            x1_copy(c, slot).wait()
            rs = pl.ds(pl.multiple_of(c * ROW_CHUNK, ROW_CHUNK), ROW_CHUNK)
            z = alpha * xbuf_ref[slot] + mod_ref[5:6, :] * o_ref[rs, :]
            o_ref[rs, :] = _layer_norm(z) * g_ref[...] + b_ref[...]
            return carry

        lax.fori_loop(0, nchunks, body, 0)


def _ffn(h2, w_up, w_gate, conv_w, conv_b, w_down, x1, mod3, ln_g, ln_b, seq, alpha):
    m, d = h2.shape
    tm, tf = 512, FFN_TILE
    f = w_up.shape[1]
    pad = f - conv_w.shape[1]
    conv_w = jnp.pad(conv_w, ((0, 0), (0, pad)))
    conv_b = jnp.pad(conv_b, (0, pad))
    nj = f // tf
    rows_per_seq = seq // tm
    kern = functools.partial(_ffn_kernel, alpha=alpha, nj=nj, rows_per_seq=rows_per_seq)
    up_tile = lambda i, j: (0, jnp.minimum(j, nj - 1))
    return pl.pallas_call(
        kern,
        grid=(m // tm, nj + 1),
        in_specs=[
            pl.BlockSpec((tm, d), lambda i, j: (i, 0)),
            pl.BlockSpec((d, tf), up_tile),
            pl.BlockSpec((d, tf), up_tile),
            pl.BlockSpec((CONV_W, tf), up_tile),
            pl.BlockSpec((1, tf), up_tile),
            pl.BlockSpec((tf, d), lambda i, j: (jnp.maximum(j - 1, 0), 0)),
            pl.BlockSpec(memory_space=pl.ANY),
            pl.BlockSpec((None, 6, d), lambda i, j: (i // rows_per_seq, 0, 0)),
            pl.BlockSpec((1, d), lambda i, j: (0, 0)),
            pl.BlockSpec((1, d), lambda i, j: (0, 0)),
        ],
        out_specs=pl.BlockSpec((tm, d), lambda i, j: (i, 0)),
        out_shape=jax.ShapeDtypeStruct((m, d), F32),
        scratch_shapes=[pltpu.VMEM((nj, 8, tf), F32),
                        pltpu.VMEM((tm, tf), BF16),
                        pltpu.VMEM((2, ROW_CHUNK, d), F32),
                        pltpu.SemaphoreType.DMA((2,))],
        compiler_params=_cparams(("arbitrary", "arbitrary"), VMEM_LIMIT_FFN),
        name="ffn",
    )(h2, w_up, w_gate, conv_w, conv_b.reshape(1, f), w_down, x1, mod3, ln_g.reshape(1, d), ln_b.reshape(1, d))


def _regroup_kernel(w_ref, o_ref):
    for n in PROJ_ORDER:
        o_ref[_NEW_OFF[n]:_NEW_OFF[n] + _WIDTH[n], :] = w_ref[_OLD_OFF[n]:_OLD_OFF[n] + _WIDTH[n], :].astype(BF16)
    o_ref[_PROJ_USED:, :] = jnp.zeros((PROJ_COLS - _PROJ_USED, o_ref.shape[1]), BF16)


def _regroup_w_in(w, l):
    wt = jnp.swapaxes(w, 1, 2)
    _, n, d = wt.shape
    tc = 256
    return pl.pallas_call(
        _regroup_kernel,
        grid=(d // tc,),
        in_specs=[pl.BlockSpec((None, n, tc), lambda i: (l, 0, i))],
        out_specs=pl.BlockSpec((PROJ_COLS, tc), lambda i: (0, i)),
        out_shape=jax.ShapeDtypeStruct((PROJ_COLS, d), BF16),
        compiler_params=_cparams(("arbitrary",)),
        name="regroup_w_in",
    )(wt)


def kernel(x, c, t5_table, w_ada, b_ada, w_in, w_g2, b_g2, gla_norm, w_out, ln1_g, ln1_b, w_up, w_gate,
           conv_w, conv_b, w_down, ln2_g, ln2_b):
    bsz, seq, d = x.shape
    depth = w_ada.shape[0]
    alpha = (2 * depth) ** 0.25
    nkb = seq // KEY_BLOCK
    x2 = x.reshape(bsz * seq, d)
    tiles = _bias_tiles(t5_table)
    for l in range(depth):
        mod3 = _ada(c, w_ada[l], b_ada[l]).reshape(bsz, 6, d)
        f_pad = -(-w_up.shape[2] // FFN_TILE) * FFN_TILE
        proj, w_gate_b = _inproj(x2, mod3, _regroup_w_in(w_in, l), seq, w_gate, l, f_pad)

        p3 = proj.reshape(bsz, nkb, KEY_BLOCK, PROJ_COLS)
        ko, io = _NEW_OFF["a_k"], _NEW_OFF["i_k"]
        kt = jnp.swapaxes(p3[..., ko:ko + A_HEAD_DIM], 2, 3)
        kit = jnp.swapaxes(p3[..., io:io + IDX_DIM], 2, 3)
        zz = jnp.zeros_like(kit)
        kbd = jnp.concatenate([jnp.concatenate([kit, zz], axis=3),
                               jnp.concatenate([zz, kit], axis=3)], axis=2)

        o_a, w_up_b, w_out_b = _dsa(proj, kbd, kt, tiles, bsz, seq, w_up, l, f_pad, w_out)
        o_b, w_down_b = _gla(proj, w_g2[l], b_g2[l], gla_norm[l], bsz, seq, w_down, l, f_pad)
        x1, h2 = _outproj(o_a, o_b, w_out_b, x2, mod3, ln1_g[l], ln1_b[l], seq, alpha)
        x2 = _ffn(h2, w_up_b, w_gate_b, conv_w[l], conv_b[l], w_down_b, x1, mod3, ln2_g[l], ln2_b[l],
                  seq, alpha)
    return x2.reshape(bsz, seq, d)
```

```python
import functools
import math

import numpy as np
import jax
import jax.numpy as jnp
from jax import lax
from jax.experimental import pallas as pl
from jax.experimental.pallas import tpu as pltpu

F32 = jnp.float32
BF16 = jnp.bfloat16
I32 = jnp.int32

CHUNK = 64
Q_BLOCK = 128
A_HEADS = 16
A_HEAD_DIM = 128
IDX_HEADS = 32
IDX_DIM = 64
TOPK_MAX = 256
T5_BUCKETS = 32
T5_MAX_DIST = 128
B_HEADS = 4
B_HEAD_V = 512
B_HEAD_K = 256
GATE_RANK = 16
GATE_TAU = 16.0
CONV_W = 3
EPS = 1e-6

A_Q = A_HEADS * A_HEAD_DIM
IDX_Q = IDX_HEADS * IDX_DIM
B_QK = B_HEADS * B_HEAD_K
B_V = B_HEADS * B_HEAD_V
IN_SPLITS = (A_Q, A_HEAD_DIM, A_HEAD_DIM, IDX_Q, IDX_DIM, IDX_HEADS, B_QK, B_QK, B_V, GATE_RANK, B_V)
IN_NAMES = ("a_q", "a_k", "a_v", "i_q", "i_k", "i_w", "g_q", "g_k", "g_v", "g_lr", "g_r")
PROJ_ORDER = ("a_q", "i_q", "g_v", "g_r", "g_q", "g_k", "a_k", "a_v", "i_k", "i_w", "g_lr")
PROJ_TILE = 1536

LANES = 128
VMEM_LIMIT = 56 * 1024 * 1024
VMEM_LIMIT_FFN = 60 * 1024 * 1024

KEY_BLOCK = 2 * Q_BLOCK
DSA_ROWS = KEY_BLOCK
DSA_HEAD_GROUP = 8
NEG_MASK = -1e30
LOG2E = math.log2(math.e)
INT_MIN = -(2 ** 31)


def _proj_layout():
    old_off = dict(zip(IN_NAMES, np.concatenate([[0], np.cumsum(IN_SPLITS)[:-1]]).tolist()))
    width = dict(zip(IN_NAMES, IN_SPLITS))
    new_off, pos = {}, 0
    for name in PROJ_ORDER:
        new_off[name] = pos
        pos += width[name]
    total = -(-pos // PROJ_TILE) * PROJ_TILE
    return old_off, width, new_off, pos, total


_OLD_OFF, _WIDTH, _NEW_OFF, _PROJ_USED, PROJ_COLS = _proj_layout()
SMALL_OFF = _NEW_OFF["i_k"]
IK_LO = 0
IW_LO = _NEW_OFF["i_w"] - SMALL_OFF
GLR_LO = _NEW_OFF["g_lr"] - SMALL_OFF


def _layer_norm(x):
    mu = jnp.mean(x, axis=-1, keepdims=True)
    xc = x - mu
    var = jnp.mean(xc * xc, axis=-1, keepdims=True)
    return xc * lax.rsqrt(var + EPS)


def _for_row_chunks(nrows, chunk, fn):
    def body(r, carry):
        fn(pl.ds(pl.multiple_of(r * chunk, chunk), chunk))
        return carry
    lax.fori_loop(0, nrows // chunk, body, 0)


ROW_CHUNK = 64
FFN_TILE = 512
OUTPROJ_TILE = 1024
GLA_SEQ_TILE = 512
FFN_DOWN_CHUNK = 512


def _cparams(sem, vmem=VMEM_LIMIT):
    return pltpu.CompilerParams(dimension_semantics=sem, vmem_limit_bytes=vmem)


def _ada_kernel(c_ref, w_ref, b_ref, o_ref):
    c = c_ref[...]
    ca = (c * jax.nn.sigmoid(c)).astype(BF16)
    o_ref[...] = jnp.dot(ca, w_ref[...].astype(BF16), preferred_element_type=F32) + b_ref[...]


def _ada(c, w, b):
    bsz, d = c.shape
    n = w.shape[1]
    tn = 512
    return pl.pallas_call(
        _ada_kernel,
        grid=(n // tn,),
        in_specs=[pl.BlockSpec((bsz, d), lambda j: (0, 0)),
                  pl.BlockSpec((d, tn), lambda j: (0, j)),
                  pl.BlockSpec((1, tn), lambda j: (0, j))],
        out_specs=pl.BlockSpec((bsz, tn), lambda j: (0, j)),
        out_shape=jax.ShapeDtypeStruct((bsz, n), F32),
        compiler_params=_cparams(("arbitrary",)),
        name="ada",
    )(c, w, b.reshape(1, n))


def _inproj_kernel(x_ref, mod_ref, w_ref, ws_ref, o_ref, ws_o_ref, h_ref):
    _side_cast(ws_ref, ws_o_ref)

    @pl.when(pl.program_id(1) == 0)
    def _():
        def rows(rs):
            xn = _layer_norm(x_ref[rs, :])
            h_ref[rs, :] = (xn * (1.0 + mod_ref[1:2, :]) + mod_ref[0:1, :]).astype(BF16)
        _for_row_chunks(x_ref.shape[0], ROW_CHUNK, rows)

    o_ref[...] = lax.dot_general(h_ref[...], w_ref[...], (((1,), (1,)), ((), ())),
                                 preferred_element_type=F32).astype(o_ref.dtype)


def _side_block_rows(total, nsteps):
    return next(r for r in range(16, total + 1, 16) if total % r == 0 and total // r <= nsteps)


def _inproj(x2, mod3, wt, seq, side_weight, layer, side_cols):
    m, d = x2.shape
    n = wt.shape[0]
    tm, tn = 512, PROJ_TILE
    nj = n // tn
    rows_per_seq = seq // tm
    wd, wn = side_weight.shape[1], side_weight.shape[2]
    wr = _side_block_rows(wd, (m // tm) * nj)
    blk = lambda i, j: jnp.minimum(i * nj + j, wd // wr - 1)
    return pl.pallas_call(
        _inproj_kernel,
        grid=(m // tm, nj),
        in_specs=[pl.BlockSpec((tm, d), lambda i, j: (i, 0)),
                  pl.BlockSpec((None, 6, d), lambda i, j: (i // rows_per_seq, 0, 0)),
                  pl.BlockSpec((tn, d), lambda i, j: (j, 0)),
                  pl.BlockSpec((None, wr, wn), lambda i, j: (layer, blk(i, j), 0))],
        out_specs=[pl.BlockSpec((tm, tn), lambda i, j: (i, j)),
                   pl.BlockSpec((side_cols // FFN_TILE, wr, FFN_TILE), lambda i, j: (0, blk(i, j), 0))],
        out_shape=[jax.ShapeDtypeStruct((m, n), BF16),
                   jax.ShapeDtypeStruct((side_cols // FFN_TILE, wd, FFN_TILE), BF16)],
        scratch_shapes=[pltpu.VMEM((tm, d), BF16)],
        compiler_params=_cparams(("arbitrary", "arbitrary"), VMEM_LIMIT_FFN),
        name="inproj",
    )(x2, mod3, wt, side_weight)


def _t5_bucket(rel):
    half = T5_BUCKETS // 2
    max_exact = half // 2
    ret = jnp.where(rel > 0, half, 0)
    n = jnp.abs(rel)
    nf = jnp.maximum(n, 1).astype(jnp.float32)
    large = max_exact + (jnp.log(nf / max_exact) / math.log(T5_MAX_DIST / max_exact)
                         * (half - max_exact)).astype(jnp.int32)
    large = jnp.minimum(large, half - 1)
    return ret + jnp.where(n < max_exact, n, large)


def _bias_kernel(tab_ref, bkt_ref, o_ref):
    for u in range(3):
        bk = bkt_ref[u]
        for h in range(A_HEADS):
            acc = jnp.zeros((Q_BLOCK, Q_BLOCK), F32)
            for b in range(T5_BUCKETS):
                acc = jnp.where(bk == b, tab_ref[b, h], acc)
            o_ref[u, h] = acc * LOG2E


def _bias_tiles(t5_table):
    i = jnp.arange(Q_BLOCK, dtype=I32)[:, None]
    j = jnp.arange(Q_BLOCK, dtype=I32)[None, :]
    rel = jnp.stack([j - i - 2 * Q_BLOCK, j - i - Q_BLOCK, j - i])
    bkt = _t5_bucket(rel).astype(I32)
    return pl.pallas_call(
        _bias_kernel,
        in_specs=[pl.BlockSpec(memory_space=pltpu.SMEM),
                  pl.BlockSpec((3, Q_BLOCK, Q_BLOCK), lambda: (0, 0, 0))],
        out_specs=pl.BlockSpec((3, A_HEADS, Q_BLOCK, Q_BLOCK), lambda: (0, 0, 0, 0)),
        out_shape=jax.ShapeDtypeStruct((3, A_HEADS, Q_BLOCK, Q_BLOCK), F32),
        name="t5_bias",
    )(t5_table, bkt)


def _side_cast(src_ref, dst_ref, row0=None, rows_valid=None):
    r, n = src_ref.shape
    if len(dst_ref.shape) == 3:
        nt, _, tw = dst_ref.shape
        for t in range(nt):
            width = min(tw, n - t * tw)
            dst_ref[t, :, :width] = src_ref[:, t * tw:t * tw + width].astype(BF16)
            if width < tw:
                dst_ref[t, :, width:] = jnp.zeros((r, tw - width), BF16)
        return
    w = src_ref[...]
    if rows_valid is not None:
        rid = lax.broadcasted_iota(I32, (r, n), 0) + row0
        w = jnp.where(rid < rows_valid, w, 0.0)
    dst_ref[:, :n] = w.astype(BF16)
    if dst_ref.shape[1] > n:
        dst_ref[:, n:] = jnp.zeros((r, dst_ref.shape[1] - n), BF16)


def _dsa_kernel(aq_ref, iq_ref, sm_ref, kbd_ref, kt_ref, v_ref, tiles_ref, exp_ref, wa_ref, wc_ref,
                o_ref, wa_o_ref, wc_o_ref,
                key_ref, nm_ref, wb_ref, t_ref, q_ref, lhs_ref, m_ref, l_ref, acc_ref, x_ref,
                *, nkb_max, topk):
    _side_cast(wa_ref, wa_o_ref)
    _side_cast(wc_ref, wc_o_ref)
    pb = pl.program_id(1)
    nkb = pb + 1
    tq = DSA_ROWS
    grows = DSA_HEAD_GROUP * tq
    half = KEY_BLOCK // 2
    pairs_per_dot = 8

    iw = jnp.dot(sm_ref[:, IW_LO:IW_LO + IDX_HEADS], exp_ref[...], preferred_element_type=F32)
    for hh in range(IDX_HEADS):
        wb_ref[hh] = iw[:, hh * LANES:(hh + 1) * LANES] * (IDX_DIM ** -0.5 * IDX_HEADS ** -0.5)
    for p in range(IDX_HEADS // 2):
        lhs_ref[p * tq:(p + 1) * tq, :] = iq_ref[:, p * LANES:(p + 1) * LANES]
    for h in range(A_HEADS):
        q_ref[h * tq:(h + 1) * tq, :] = aq_ref[:, h * LANES:(h + 1) * LANES]

    row_chunk = lax.shift_right_logical(lax.broadcasted_iota(I32, (tq, KEY_BLOCK), 0) + pb * tq, 6)
    col_iota = lax.broadcasted_iota(I32, (tq, KEY_BLOCK), 1)

    def idx_body(kb, carry):
        acc = jnp.zeros((tq, KEY_BLOCK), F32)
        for p0 in range(0, IDX_HEADS // 2, pairs_per_dot):
            res = jnp.dot(lhs_ref[p0 * tq:(p0 + pairs_per_dot) * tq, :], kbd_ref[kb],
                          preferred_element_type=F32)
            for p in range(pairs_per_dot):
                r = res[p * tq:(p + 1) * tq]
                we = wb_ref[2 * (p0 + p)]
                wo = wb_ref[2 * (p0 + p) + 1]
                acc = acc + jnp.maximum(r[:, :KEY_BLOCK], 0.0) * jnp.concatenate([we, we], axis=1)
                acc = acc + jnp.maximum(r[:, KEY_BLOCK:], 0.0) * jnp.concatenate([wo, wo], axis=1)
        bits = pltpu.bitcast(acc, I32)
        skey = bits ^ (lax.shift_right_arithmetic(bits, 31) & 0x7FFFFFFF)
        skey = jnp.where(acc == 0.0, 0, skey)
        adm = lax.shift_right_logical(col_iota + kb * KEY_BLOCK, 6) <= row_chunk
        key_ref[kb] = jnp.where(adm, skey, INT_MIN)
        return carry

    lax.fori_loop(0, nkb, idx_body, 0)

    def search(n):
        def count_ge(cand):
            c = jnp.zeros((tq, LANES), F32)
            for kb in range(n):
                k = key_ref[kb]
                c = c + jnp.where(k[:, :half] >= cand, 1.0, 0.0) + jnp.where(k[:, half:] >= cand, 1.0, 0.0)
            return jnp.broadcast_to(jnp.sum(c, axis=1, keepdims=True), (tq, LANES))

        zero = jnp.zeros((tq, LANES), I32)
        t0 = jnp.where(count_ge(zero) >= topk, zero, INT_MIN)

        def body(i, t):
            cand = t + lax.shift_left(jnp.int32(1), 30 - i)
            return jnp.where(count_ge(cand) >= topk, cand, t)

        t = lax.fori_loop(0, 31, body, t0)
        t_ref[...] = jnp.maximum(t, INT_MIN + 1)

    all_selected = (pb + 1) * tq <= topk

    @pl.when(all_selected)
    def _():
        t_ref[...] = jnp.full((tq, LANES), INT_MIN + 1, I32)

    for n in range(1, nkb_max + 1):
        pl.when(jnp.logical_and(nkb == n, jnp.logical_not(all_selected)))(functools.partial(search, n))

    thr = t_ref[...]
    thr2 = jnp.concatenate([thr, thr], axis=1)

    def cnt_body(kb, c):
        g = jnp.where(key_ref[kb] >= thr2, 1.0, 0.0)
        return c + g[:, :half] + g[:, half:]

    n_ge = jnp.sum(lax.fori_loop(0, nkb, cnt_body, jnp.zeros((tq, LANES), F32)),
                   axis=1, keepdims=True)
    has_ties = jnp.max(n_ge) > topk

    @pl.when(jnp.logical_not(has_ties))
    def _():
        def body(kb, carry):
            nm_ref[kb] = jnp.where(key_ref[kb] >= thr2, 0.0, NEG_MASK)
            return carry
        lax.fori_loop(0, nkb, body, 0)

    @pl.when(has_ties)
    def _():
        def gt_body(kb, c):
            g = jnp.where(key_ref[kb] > thr2, 1.0, 0.0)
            return c + g[:, :half] + g[:, half:]
        n_gt = jnp.sum(lax.fori_loop(0, nkb, gt_body, jnp.zeros((tq, LANES), F32)),
                       axis=1, keepdims=True)
        need = topk - n_gt
        tri = (lax.broadcasted_iota(I32, (half, half), 0)
               <= lax.broadcasted_iota(I32, (half, half), 1)).astype(BF16)

        def body(kb, seen):
            k = key_ref[kb]
            parts = []
            for s in range(2):
                ks = k[:, s * half:(s + 1) * half]
                eq = ks == thr
                eqf = jnp.where(eq, 1.0, 0.0)
                rank = jnp.dot(eqf.astype(BF16), tri, preferred_element_type=F32) + seen
                keep = jnp.logical_or(ks > thr, jnp.logical_and(eq, rank <= need))
                parts.append(jnp.where(keep, 0.0, NEG_MASK))
                seen = seen + jnp.sum(eqf, axis=1, keepdims=True)
            nm_ref[kb] = jnp.concatenate(parts, axis=1)
            return seen
        lax.fori_loop(0, nkb, body, jnp.zeros((tq, 1), F32))

    scale = A_HEAD_DIM ** -0.5 * LOG2E

    def tile_of(g, qblk):
        return jnp.where(g == qblk, 2, jnp.where(g == qblk - 1, 1, 0))

    for h0 in range(0, A_HEADS, DSA_HEAD_GROUP):
        hs = slice(h0, h0 + DSA_HEAD_GROUP)
        qrows = slice(h0 * tq, (h0 + DSA_HEAD_GROUP) * tq)
        m_ref[...] = jnp.full((grows, LANES), NEG_MASK, F32)
        l_ref[...] = jnp.zeros((grows, LANES), F32)
        acc_ref[...] = jnp.zeros((grows, A_HEAD_DIM), F32)

        def max_body(kb, carry, hs=hs, qrows=qrows):
            x = jnp.dot(q_ref[qrows, :], kt_ref[kb], preferred_element_type=F32) * scale
            per_qblk = []
            for r in range(tq // Q_BLOCK):
                qblk = (tq // Q_BLOCK) * pb + r
                per_qblk.append(jnp.concatenate([tiles_ref[tile_of(2 * kb, qblk), hs],
                                                 tiles_ref[tile_of(2 * kb + 1, qblk), hs]], axis=2))
            bias = jnp.concatenate(per_qblk, axis=1)
            x = (x.reshape(DSA_HEAD_GROUP, tq, KEY_BLOCK) + bias + nm_ref[kb][None]).reshape(grows, KEY_BLOCK)
            x_ref[kb] = x
            m_ref[...] = jnp.maximum(m_ref[...], jnp.maximum(x[:, :half], x[:, half:]))
            return carry

        lax.fori_loop(0, nkb, max_body, 0)
        m_ref[...] = jnp.broadcast_to(jnp.max(m_ref[...], axis=1, keepdims=True), (grows, LANES))

        def pv_body(kb, carry):
            m = m_ref[...]
            p = jnp.exp2(x_ref[kb] - jnp.concatenate([m, m], axis=1))
            l_ref[...] += p[:, :half] + p[:, half:]
            vb = v_ref[pl.ds(pl.multiple_of(kb * KEY_BLOCK, KEY_BLOCK), KEY_BLOCK), :]
            acc_ref[...] += jnp.dot(p.astype(BF16), vb, preferred_element_type=F32)
            return carry

        lax.fori_loop(0, nkb, pv_body, 0)

        out = acc_ref[...] / jnp.sum(l_ref[...], axis=1, keepdims=True)
        for h in range(DSA_HEAD_GROUP):
            o_ref[:, (h0 + h) * A_HEAD_DIM:(h0 + h + 1) * A_HEAD_DIM] = out[h * tq:(h + 1) * tq].astype(o_ref.dtype)


def _dsa(proj, kbd, kt, tiles, bsz, seq, side_weight, layer, side_cols, side_weight2):
    tq = DSA_ROWS
    nb = seq // tq
    nkb = seq // KEY_BLOCK
    topk = min(TOPK_MAX, seq // 4)
    grows = DSA_HEAD_GROUP * tq
    av_blk = _NEW_OFF["a_v"] // A_HEAD_DIM
    kern = functools.partial(_dsa_kernel, nkb_max=nkb, topk=topk)
    expand = jnp.asarray(np.arange(IDX_HEADS)[:, None] == np.arange(IDX_HEADS * LANES)[None, :] // LANES, BF16)
    wd, wn = side_weight.shape[1], side_weight.shape[2]
    assert wd % (bsz * nb) == 0 and (wd // (bsz * nb)) % 16 == 0 and side_cols >= wn
    wr = wd // (bsz * nb)
    wd2, wn2 = side_weight2.shape[1], side_weight2.shape[2]
    assert wd2 % (bsz * nb) == 0 and (wd2 // (bsz * nb)) % 16 == 0
    wr2 = wd2 // (bsz * nb)
    once = pl.Buffered(1)
    return pl.pallas_call(
        kern,
        grid=(bsz, nb),
        in_specs=[
            pl.BlockSpec((tq, A_Q), lambda b, q: (b * nb + q, _NEW_OFF["a_q"] // A_Q)),
            pl.BlockSpec((tq, IDX_Q), lambda b, q: (b * nb + q, _NEW_OFF["i_q"] // IDX_Q)),
            pl.BlockSpec((tq, LANES), lambda b, q: (b * nb + q, SMALL_OFF // LANES)),
            pl.BlockSpec((None, nkb, 2 * IDX_DIM, 2 * KEY_BLOCK), lambda b, q: (b, 0, 0, 0)),
            pl.BlockSpec((None, nkb, A_HEAD_DIM, KEY_BLOCK), lambda b, q: (b, 0, 0, 0)),
            pl.BlockSpec((seq, A_HEAD_DIM), lambda b, q: (b, av_blk)),
            pl.BlockSpec((3, A_HEADS, Q_BLOCK, Q_BLOCK), lambda b, q: (0, 0, 0, 0), pipeline_mode=once),
            pl.BlockSpec((IDX_HEADS, IDX_HEADS * LANES), lambda b, q: (0, 0), pipeline_mode=once),
            pl.BlockSpec((None, wr, wn), lambda b, q: (layer, b * nb + q, 0)),
            pl.BlockSpec((None, wr2, wn2), lambda b, q: (layer, b * nb + q, 0)),
        ],
        out_specs=[pl.BlockSpec((tq, A_Q), lambda b, q: (b * nb + q, 0)),
                   pl.BlockSpec((side_cols // FFN_TILE, wr, FFN_TILE), lambda b, q: (0, b * nb + q, 0)),
                   pl.BlockSpec((wn2 // OUTPROJ_TILE, wr2, OUTPROJ_TILE), lambda b, q: (0, b * nb + q, 0))],
        out_shape=[jax.ShapeDtypeStruct((bsz * seq, A_Q), BF16),
                   jax.ShapeDtypeStruct((side_cols // FFN_TILE, wd, FFN_TILE), BF16),
                   jax.ShapeDtypeStruct((wn2 // OUTPROJ_TILE, wd2, OUTPROJ_TILE), BF16)],
        scratch_shapes=[
            pltpu.VMEM((nkb, tq, KEY_BLOCK), I32),
            pltpu.VMEM((nkb, tq, KEY_BLOCK), F32),
            pltpu.VMEM((IDX_HEADS, tq, LANES), F32),
            pltpu.VMEM((tq, LANES), I32),
            pltpu.VMEM((A_HEADS * tq, A_HEAD_DIM), BF16),
            pltpu.VMEM((A_HEADS * tq, LANES), BF16),
            pltpu.VMEM((grows, LANES), F32),
            pltpu.VMEM((grows, LANES), F32),
            pltpu.VMEM((grows, A_HEAD_DIM), F32),
            pltpu.VMEM((nkb, grows, KEY_BLOCK), F32),
        ],
        compiler_params=_cparams(("arbitrary", "arbitrary"), VMEM_LIMIT_FFN),
        name="dsa",
    )(proj, proj, proj, kbd, kt, proj, tiles, expand, side_weight, side_weight2)


def _gla_kernel(q_ref, k_ref, v_ref, r_ref, sm_ref, wg_ref, bg_ref, nrm_ref, tri_ref, sel_ref, wd_ref,
                o_ref, wd_o_ref, st_ref, oacc_ref, *, nchunk, side_rows):
    step = pl.program_id(0) * pl.num_programs(1) + pl.program_id(1)
    _side_cast(wd_ref, wd_o_ref, step * wd_ref.shape[0], side_rows)
    @pl.when(pl.program_id(1) == 0)
    def _():
        st_ref[...] = jnp.zeros(st_ref.shape, F32)

    ts = nchunk * CHUNK
    ri = lax.broadcasted_iota(I32, (ts, ts), 0)
    ci = lax.broadcasted_iota(I32, (ts, ts), 1)
    causal = jnp.logical_and(lax.shift_right_logical(ri, 6) == lax.shift_right_logical(ci, 6), ci <= ri)
    tn = (((0,), (0,)), ((), ()))
    nt = (((1,), (1,)), ((), ()))
    tri = tri_ref[...]
    sel = sel_ref[...]
    glr = sm_ref[:, GLR_LO:GLR_LO + GATE_RANK]

    heads = range(B_HEADS)
    ksl = [slice(h * B_HEAD_K, (h + 1) * B_HEAD_K) for h in heads]
    vsl = [slice(h * B_HEAD_V, (h + 1) * B_HEAD_V) for h in heads]
    qe, ke, kd, dcol = [], [], [], []

    def gates(h):
        ks = ksl[h]
        z = jnp.dot(glr, wg_ref[:, ks], preferred_element_type=F32) + bg_ref[:, ks]
        g = (jnp.minimum(z, 0.0) - jnp.log(1.0 + jnp.exp(-jnp.abs(z)))) * (1.0 / GATE_TAU)
        g_hi = g.astype(BF16)
        g_lo = (g - g_hi.astype(F32)).astype(BF16)
        b = jnp.dot(tri, g_hi, preferred_element_type=F32) + jnp.dot(tri, g_lo, preferred_element_type=F32)
        bl = jnp.concatenate(
            [jnp.broadcast_to(b[(c + 1) * CHUNK - 1:(c + 1) * CHUNK, :], (CHUNK, B_HEAD_K)) for c in range(nchunk)],
            axis=0)
        qc = q_ref[:, ks].astype(F32) * (B_HEAD_K ** -0.5)
        kc = k_ref[:, ks].astype(F32)
        qe.append((qc * jnp.exp(b)).astype(BF16))
        ke.append((kc * jnp.exp(-b)).astype(BF16))
        kd.append((kc * jnp.exp(bl - b)).astype(BF16))
        dcol.append(jnp.exp(lax.dot_general(g_hi, sel, tn, preferred_element_type=F32)
                            + lax.dot_general(g_lo, sel, tn, preferred_element_type=F32)))

    def intra(h):
        a = lax.dot_general(qe[h], ke[h], nt, preferred_element_type=F32)
        a = jnp.where(causal, a, 0.0).astype(BF16)
        oacc_ref[:, vsl[h]] = jnp.dot(a, v_ref[:, vsl[h]], preferred_element_type=F32)

    gates(0)
    for h in heads:
        if h + 1 < B_HEADS:
            gates(h + 1)
        intra(h)

    st = [st_ref[h] for h in heads]
    for c in range(nchunk):
        rs = slice(c * CHUNK, (c + 1) * CHUNK)
        for h in heads:
            oacc_ref[rs, vsl[h]] += jnp.dot(qe[h][rs], st[h].astype(BF16), preferred_element_type=F32)
            dec = jnp.broadcast_to(dcol[h][:, c:c + 1], (B_HEAD_K, B_HEAD_V))
            st[h] = st[h] * dec + lax.dot_general(kd[h][rs], v_ref[rs, vsl[h]], tn, preferred_element_type=F32)
    for h in heads:
        st_ref[h] = st[h]
        o = oacc_ref[:, vsl[h]]
        ms = jnp.mean(o * o, axis=-1, keepdims=True)
        on = o * lax.rsqrt(ms + EPS) * nrm_ref[...]
        rr = r_ref[:, vsl[h]].astype(F32)
        o_ref[:, vsl[h]] = (on * (rr * jax.nn.sigmoid(rr))).astype(o_ref.dtype)


def _gla(proj, w_g2, b_g2, gla_norm, bsz, seq, side_weight, layer, side_rows_padded):
    q_blk = _NEW_OFF["g_q"] // B_QK
    k_blk = _NEW_OFF["g_k"] // B_QK
    v_blk = _NEW_OFF["g_v"] // B_V
    r_blk = _NEW_OFF["g_r"] // B_V
    ts = min(seq, GLA_SEQ_TILE)
    nt = seq // ts
    nchunk = ts // CHUNK
    src_rows, wdim = side_weight.shape[1], side_weight.shape[2]
    nsteps = bsz * nt
    assert side_rows_padded % nsteps == 0 and (side_rows_padded // nsteps) % 16 == 0
    sr = side_rows_padded // nsteps
    last_src = (src_rows - 1) // sr
    kern = functools.partial(_gla_kernel, nchunk=nchunk, side_rows=src_rows)
    rc = np.arange(ts)[:, None] // CHUNK
    cc = np.arange(ts)[None, :] // CHUNK
    tri = jnp.asarray((rc == cc) & (np.arange(ts)[None, :] <= np.arange(ts)[:, None]), BF16)
    sel = jnp.asarray(rc == np.arange(LANES)[None, :], BF16)
    const = lambda shape: pl.BlockSpec(shape, lambda b, t: (0, 0))
    return pl.pallas_call(
        kern,
        grid=(bsz, nt),
        in_specs=[
            pl.BlockSpec((ts, B_QK), lambda b, t: (b * nt + t, q_blk)),
            pl.BlockSpec((ts, B_QK), lambda b, t: (b * nt + t, k_blk)),
            pl.BlockSpec((ts, B_V), lambda b, t: (b * nt + t, v_blk)),
            pl.BlockSpec((ts, B_V), lambda b, t: (b * nt + t, r_blk)),
            pl.BlockSpec((ts, LANES), lambda b, t: (b * nt + t, SMALL_OFF // LANES)),
            pl.BlockSpec((GATE_RANK, B_QK), lambda b, t: (0, 0)),
            pl.BlockSpec((1, B_QK), lambda b, t: (0, 0)),
            pl.BlockSpec((1, B_HEAD_V), lambda b, t: (0, 0)),
            const((ts, ts)), const((ts, LANES)),
            pl.BlockSpec((None, sr, wdim), lambda b, t: (layer, jnp.minimum(b * nt + t, last_src), 0)),
        ],
        out_specs=[pl.BlockSpec((ts, B_V), lambda b, t: (b * nt + t, 0)),
                   pl.BlockSpec((sr, wdim), lambda b, t: (b * nt + t, 0))],
        out_shape=[jax.ShapeDtypeStruct((bsz * seq, B_V), BF16),
                   jax.ShapeDtypeStruct((side_rows_padded, wdim), BF16)],
        scratch_shapes=[pltpu.VMEM((B_HEADS, B_HEAD_K, B_HEAD_V), F32),
                        pltpu.VMEM((ts, B_V), F32)],
        compiler_params=_cparams(("arbitrary", "arbitrary")),
        name="gla",
    )(proj, proj, proj, proj, proj, w_g2.astype(BF16), b_g2.reshape(1, -1), gla_norm.reshape(1, -1),
      tri, sel, side_weight)


def _outproj_kernel(oa_ref, ob_ref, w_ref, x_ref, modj_ref, mod_ref, g_ref, b_ref, x1_ref, h2_ref,
                    *, alpha, nj, tn):
    j = pl.program_id(1)
    a = jnp.concatenate([oa_ref[...], ob_ref[...]], axis=1)
    y = jnp.dot(a, w_ref[...], preferred_element_type=F32)
    z = alpha * x_ref[...] + modj_ref[2:3, :] * y
    for jj in range(nj):
        @pl.when(j == jj)
        def _(jj=jj):
            x1_ref[:, jj * tn:(jj + 1) * tn] = z

    @pl.when(j == nj - 1)
    def _():
        def rows(rs):
            x1 = _layer_norm(x1_ref[rs, :]) * g_ref[...] + b_ref[...]
            x1_ref[rs, :] = x1
            h2_ref[rs, :] = (_layer_norm(x1) * (1.0 + mod_ref[4:5, :]) + mod_ref[3:4, :]).astype(h2_ref.dtype)
        _for_row_chunks(x1_ref.shape[0], ROW_CHUNK, rows)


def _outproj(o_a, o_b, w, x2, mod3, ln_g, ln_b, seq, alpha):
    m, d = x2.shape
    tm, tn = 512, OUTPROJ_TILE
    nj = d // tn
    rows_per_seq = seq // tm
    kern = functools.partial(_outproj_kernel, alpha=alpha, nj=nj, tn=tn)
    return pl.pallas_call(
        kern,
        grid=(m // tm, nj),
        in_specs=[
            pl.BlockSpec((tm, A_Q), lambda i, j: (i, 0)),
            pl.BlockSpec((tm, B_V), lambda i, j: (i, 0)),
            pl.BlockSpec((None, A_Q + B_V, tn), lambda i, j: (j, 0, 0)),
            pl.BlockSpec((tm, tn), lambda i, j: (i, j)),
            pl.BlockSpec((None, 6, tn), lambda i, j: (i // rows_per_seq, 0, j)),
            pl.BlockSpec((None, 6, d), lambda i, j: (i // rows_per_seq, 0, 0)),
            pl.BlockSpec((1, d), lambda i, j: (0, 0)),
            pl.BlockSpec((1, d), lambda i, j: (0, 0)),
        ],
        out_specs=[pl.BlockSpec((tm, d), lambda i, j: (i, 0)),
                   pl.BlockSpec((tm, d), lambda i, j: (i, 0))],
        out_shape=[jax.ShapeDtypeStruct((m, d), F32), jax.ShapeDtypeStruct((m, d), BF16)],
        compiler_params=_cparams(("arbitrary", "arbitrary"), VMEM_LIMIT_FFN),
        name="outproj",
    )(o_a, o_b, w, x2, mod3, mod3, ln_g.reshape(1, d), ln_b.reshape(1, d))


def _ffn_kernel(h_ref, wu_ref, wg_ref, cw_ref, cb_ref, wd_ref, x1_hbm, mod_ref, g_ref, b_ref, o_ref,
                carry_ref, act_ref, xbuf_ref, xsem, *, alpha, nj, rows_per_seq):
    i = pl.program_id(0)
    j = pl.program_id(1)
    tm = o_ref.shape[0]
    nchunks = tm // ROW_CHUNK

    def x1_copy(c, slot):
        row0 = pl.multiple_of(i * tm + c * ROW_CHUNK, ROW_CHUNK)
        return pltpu.make_async_copy(x1_hbm.at[pl.ds(row0, ROW_CHUNK), :], xbuf_ref.at[slot], xsem.at[slot])

    def up_gate():
        h = h_ref[...]
        tm = h.shape[0]
        u = jnp.dot(h, wu_ref[...], preferred_element_type=F32)
        gt = jnp.dot(h, wg_ref[...], preferred_element_type=F32)
        first = (i % rows_per_seq) == 0
        prev = jnp.where(first, 0.0, carry_ref[j])
        carry_ref[j] = u[tm - 8:tm, :]
        rid = lax.broadcasted_iota(I32, u.shape, 0)
        p1 = prev[7:8, :]
        p2 = prev[6:7, :]
        u1 = jnp.where(rid == 0, p1, pltpu.roll(u, 1, axis=0))
        u2 = jnp.where(rid == 0, p2, jnp.where(rid == 1, p1, pltpu.roll(u, 2, axis=0)))
        cv = cw_ref[0:1, :] * u2 + cw_ref[1:2, :] * u1 + cw_ref[2:3, :] * u + cb_ref[...]
        cdf = 0.5 * (1.0 + jnp.tanh(math.sqrt(2.0 / math.pi) * (cv + 0.044715 * (cv ** 3))))
        return (cv * cdf * gt).astype(BF16)

    def down(act):
        for c0 in range(0, o_ref.shape[1], FFN_DOWN_CHUNK):
            cs = slice(c0, c0 + FFN_DOWN_CHUNK)
            o_ref[:, cs] += jnp.dot(act, wd_ref[:, cs], preferred_element_type=F32)

    @pl.when(j == 0)
    def _():
        o_ref[...] = jnp.zeros(o_ref.shape, F32)
        act_ref[...] = up_gate()

    @pl.when(jnp.logical_and(j > 0, j < nj))
    def _():
        act_prev = act_ref[...]
        act_ref[...] = up_gate()
        down(act_prev)

    @pl.when(j == nj - 1)
    def _():
        x1_copy(0, 0).start()

    @pl.when(j == nj)
    def _():
        down(act_ref[...])

        def body(c, carry):
            slot = c % 2

            @pl.when(c + 1 < nchunks)
            def _():
                x1_copy(c + 1, 1 - slot).start()

            x1_copy(c, slot).wait()
            rs = pl.ds(pl.multiple_of(c * ROW_CHUNK, ROW_CHUNK), ROW_CHUNK)
            z = alpha * xbuf_ref[slot] + mod_ref[5:6, :] * o_ref[rs, :]
            o_ref[rs, :] = _layer_norm(z) * g_ref[...] + b_ref[...]
            return carry

        lax.fori_loop(0, nchunks, body, 0)


def _ffn(h2, w_up, w_gate, conv_w, conv_b, w_down, x1, mod3, ln_g, ln_b, seq, alpha):
    m, d = h2.shape
    tm, tf = 512, FFN_TILE
    f = w_down.shape[0]
    pad = f - conv_w.shape[1]
    conv_w = jnp.pad(conv_w, ((0, 0), (0, pad)))
    conv_b = jnp.pad(conv_b, (0, pad))
    nj = f // tf
    rows_per_seq = seq // tm
    kern = functools.partial(_ffn_kernel, alpha=alpha, nj=nj, rows_per_seq=rows_per_seq)
    up_tile = lambda i, j: (0, jnp.minimum(j, nj - 1))
    up_weight = lambda i, j: (jnp.minimum(j, nj - 1), 0, 0)
    return pl.pallas_call(
        kern,
        grid=(m // tm, nj + 1),
        in_specs=[
            pl.BlockSpec((tm, d), lambda i, j: (i, 0)),
            pl.BlockSpec((None, d, tf), up_weight),
            pl.BlockSpec((None, d, tf), up_weight),
            pl.BlockSpec((CONV_W, tf), up_tile),
            pl.BlockSpec((1, tf), up_tile),
            pl.BlockSpec((tf, d), lambda i, j: (jnp.maximum(j - 1, 0), 0)),
            pl.BlockSpec(memory_space=pl.ANY),
            pl.BlockSpec((None, 6, d), lambda i, j: (i // rows_per_seq, 0, 0)),
            pl.BlockSpec((1, d), lambda i, j: (0, 0)),
            pl.BlockSpec((1, d), lambda i, j: (0, 0)),
        ],
        out_specs=pl.BlockSpec((tm, d), lambda i, j: (i, 0)),
        out_shape=jax.ShapeDtypeStruct((m, d), F32),
        scratch_shapes=[pltpu.VMEM((nj, 8, tf), F32),
                        pltpu.VMEM((tm, tf), BF16),
                        pltpu.VMEM((2, ROW_CHUNK, d), F32),
                        pltpu.SemaphoreType.DMA((2,))],
        compiler_params=_cparams(("arbitrary", "arbitrary"), VMEM_LIMIT_FFN),
        name="ffn",
    )(h2, w_up, w_gate, conv_w, conv_b.reshape(1, f), w_down, x1, mod3, ln_g.reshape(1, d), ln_b.reshape(1, d))


def _regroup_kernel(w_ref, o_ref):
    for n in PROJ_ORDER:
        o_ref[_NEW_OFF[n]:_NEW_OFF[n] + _WIDTH[n], :] = w_ref[_OLD_OFF[n]:_OLD_OFF[n] + _WIDTH[n], :].astype(BF16)
    o_ref[_PROJ_USED:, :] = jnp.zeros((PROJ_COLS - _PROJ_USED, o_ref.shape[1]), BF16)


def _regroup_w_in(w, l):
    wt = jnp.swapaxes(w, 1, 2)
    _, n, d = wt.shape
    tc = 256
    return pl.pallas_call(
        _regroup_kernel,
        grid=(d // tc,),
        in_specs=[pl.BlockSpec((None, n, tc), lambda i: (l, 0, i))],
        out_specs=pl.BlockSpec((PROJ_COLS, tc), lambda i: (0, i)),
        out_shape=jax.ShapeDtypeStruct((PROJ_COLS, d), BF16),
        compiler_params=_cparams(("arbitrary",)),
        name="regroup_w_in",
    )(wt)


def kernel(x, c, t5_table, w_ada, b_ada, w_in, w_g2, b_g2, gla_norm, w_out, ln1_g, ln1_b, w_up, w_gate,
           conv_w, conv_b, w_down, ln2_g, ln2_b):
    bsz, seq, d = x.shape
    depth = w_ada.shape[0]
    alpha = (2 * depth) ** 0.25
    nkb = seq // KEY_BLOCK
    x2 = x.reshape(bsz * seq, d)
    tiles = _bias_tiles(t5_table)
    for l in range(depth):
        mod3 = _ada(c, w_ada[l], b_ada[l]).reshape(bsz, 6, d)
        f_pad = -(-w_up.shape[2] // FFN_TILE) * FFN_TILE
        proj, w_gate_b = _inproj(x2, mod3, _regroup_w_in(w_in, l), seq, w_gate, l, f_pad)

        p3 = proj.reshape(bsz, nkb, KEY_BLOCK, PROJ_COLS)
        ko, io = _NEW_OFF["a_k"], _NEW_OFF["i_k"]
        kt = jnp.swapaxes(p3[..., ko:ko + A_HEAD_DIM], 2, 3)
        kit = jnp.swapaxes(p3[..., io:io + IDX_DIM], 2, 3)
        zz = jnp.zeros_like(kit)
        kbd = jnp.concatenate([jnp.concatenate([kit, zz], axis=3),
                               jnp.concatenate([zz, kit], axis=3)], axis=2)

        o_a, w_up_b, w_out_b = _dsa(proj, kbd, kt, tiles, bsz, seq, w_up, l, f_pad, w_out)
        o_b, w_down_b = _gla(proj, w_g2[l], b_g2[l], gla_norm[l], bsz, seq, w_down, l, f_pad)
        x1, h2 = _outproj(o_a, o_b, w_out_b, x2, mod3, ln1_g[l], ln1_b[l], seq, alpha)
        x2 = _ffn(h2, w_up_b, w_gate_b, conv_w[l], conv_b[l], w_down_b, x1, mod3, ln2_g[l], ln2_b[l],
                  seq, alpha)
    return x2.reshape(bsz, seq, d)
```

```python
import functools
import math

import numpy as np
import jax
import jax.numpy as jnp
from jax import lax
from jax.experimental import pallas as pl
from jax.experimental.pallas import tpu as pltpu

F32 = jnp.float32
BF16 = jnp.bfloat16
I32 = jnp.int32

CHUNK = 64
Q_BLOCK = 128
A_HEADS = 16
A_HEAD_DIM = 128
IDX_HEADS = 32
IDX_DIM = 64
TOPK_MAX = 256
T5_BUCKETS = 32
T5_MAX_DIST = 128
B_HEADS = 4
B_HEAD_V = 512
B_HEAD_K = 256
GATE_RANK = 16
GATE_TAU = 16.0
CONV_W = 3
EPS = 1e-6

A_Q = A_HEADS * A_HEAD_DIM
IDX_Q = IDX_HEADS * IDX_DIM
B_QK = B_HEADS * B_HEAD_K
B_V = B_HEADS * B_HEAD_V
IN_SPLITS = (A_Q, A_HEAD_DIM, A_HEAD_DIM, IDX_Q, IDX_DIM, IDX_HEADS, B_QK, B_QK, B_V, GATE_RANK, B_V)
IN_NAMES = ("a_q", "a_k", "a_v", "i_q", "i_k", "i_w", "g_q", "g_k", "g_v", "g_lr", "g_r")
PROJ_ORDER = ("a_q", "i_q", "g_v", "g_r", "g_q", "g_k", "a_k", "a_v", "i_k", "i_w", "g_lr")
PROJ_TILE = 1536

LANES = 128
VMEM_LIMIT = 56 * 1024 * 1024
VMEM_LIMIT_FFN = 60 * 1024 * 1024

KEY_BLOCK = 2 * Q_BLOCK
DSA_ROWS = KEY_BLOCK
DSA_HEAD_GROUP = 8
NEG_MASK = -1e30
LOG2E = math.log2(math.e)
INT_MIN = -(2 ** 31)


def _proj_layout():
    old_off = dict(zip(IN_NAMES, np.concatenate([[0], np.cumsum(IN_SPLITS)[:-1]]).tolist()))
    width = dict(zip(IN_NAMES, IN_SPLITS))
    new_off, pos = {}, 0
    for name in PROJ_ORDER:
        new_off[name] = pos
        pos += width[name]
    total = -(-pos // PROJ_TILE) * PROJ_TILE
    return old_off, width, new_off, pos, total


_OLD_OFF, _WIDTH, _NEW_OFF, _PROJ_USED, PROJ_COLS = _proj_layout()
SMALL_OFF = _NEW_OFF["i_k"]
IK_LO = 0
IW_LO = _NEW_OFF["i_w"] - SMALL_OFF
GLR_LO = _NEW_OFF["g_lr"] - SMALL_OFF


def _layer_norm(x):
    mu = jnp.mean(x, axis=-1, keepdims=True)
    xc = x - mu
    var = jnp.mean(xc * xc, axis=-1, keepdims=True)
    return xc * lax.rsqrt(var + EPS)


def _for_row_chunks(nrows, chunk, fn):
    def body(r, carry):
        fn(pl.ds(pl.multiple_of(r * chunk, chunk), chunk))
        return carry
    lax.fori_loop(0, nrows // chunk, body, 0)


ROW_CHUNK = 64
FFN_TILE = 512
GLA_SEQ_TILE = 512
FFN_DOWN_CHUNK = 512


def _cparams(sem, vmem=VMEM_LIMIT):
    return pltpu.CompilerParams(dimension_semantics=sem, vmem_limit_bytes=vmem)


def _ada_kernel(c_ref, w_ref, b_ref, o_ref):
    c = c_ref[...]
    ca = (c * jax.nn.sigmoid(c)).astype(BF16)
    o_ref[...] = jnp.dot(ca, w_ref[...].astype(BF16), preferred_element_type=F32) + b_ref[...]


def _ada(c, w, b):
    bsz, d = c.shape
    n = w.shape[1]
    tn = 512
    return pl.pallas_call(
        _ada_kernel,
        grid=(n // tn,),
        in_specs=[pl.BlockSpec((bsz, d), lambda j: (0, 0)),
                  pl.BlockSpec((d, tn), lambda j: (0, j)),
                  pl.BlockSpec((1, tn), lambda j: (0, j))],
        out_specs=pl.BlockSpec((bsz, tn), lambda j: (0, j)),
        out_shape=jax.ShapeDtypeStruct((bsz, n), F32),
        compiler_params=_cparams(("arbitrary",)),
        name="ada",
    )(c, w, b.reshape(1, n))


def _inproj_kernel(x_ref, mod_ref, w_ref, ws_ref, o_ref, ws_o_ref, h_ref):
    _side_cast(ws_ref, ws_o_ref)

    @pl.when(pl.program_id(1) == 0)
    def _():
        def rows(rs):
            xn = _layer_norm(x_ref[rs, :])
            h_ref[rs, :] = (xn * (1.0 + mod_ref[1:2, :]) + mod_ref[0:1, :]).astype(BF16)
        _for_row_chunks(x_ref.shape[0], ROW_CHUNK, rows)

    o_ref[...] = lax.dot_general(h_ref[...], w_ref[...], (((1,), (1,)), ((), ())),
                                 preferred_element_type=F32).astype(o_ref.dtype)


def _side_block_rows(total, nsteps):
    return next(r for r in range(16, total + 1, 16) if total % r == 0 and total // r <= nsteps)


def _inproj(x2, mod3, wt, seq, side_weight, layer, side_cols):
    m, d = x2.shape
    n = wt.shape[0]
    tm, tn = 512, PROJ_TILE
    nj = n // tn
    rows_per_seq = seq // tm
    wd, wn = side_weight.shape[1], side_weight.shape[2]
    wr = _side_block_rows(wd, (m // tm) * nj)
    blk = lambda i, j: jnp.minimum(i * nj + j, wd // wr - 1)
    return pl.pallas_call(
        _inproj_kernel,
        grid=(m // tm, nj),
        in_specs=[pl.BlockSpec((tm, d), lambda i, j: (i, 0)),
                  pl.BlockSpec((None, 6, d), lambda i, j: (i // rows_per_seq, 0, 0)),
                  pl.BlockSpec((tn, d), lambda i, j: (j, 0)),
                  pl.BlockSpec((None, wr, wn), lambda i, j: (layer, blk(i, j), 0))],
        out_specs=[pl.BlockSpec((tm, tn), lambda i, j: (i, j)),
                   pl.BlockSpec((wr, side_cols), lambda i, j: (blk(i, j), 0))],
        out_shape=[jax.ShapeDtypeStruct((m, n), BF16), jax.ShapeDtypeStruct((wd, side_cols), BF16)],
        scratch_shapes=[pltpu.VMEM((tm, d), BF16)],
        compiler_params=_cparams(("arbitrary", "arbitrary"), VMEM_LIMIT_FFN),
        name="inproj",
    )(x2, mod3, wt, side_weight)


def _t5_bucket(rel):
    half = T5_BUCKETS // 2
    max_exact = half // 2
    ret = jnp.where(rel > 0, half, 0)
    n = jnp.abs(rel)
    nf = jnp.maximum(n, 1).astype(jnp.float32)
    large = max_exact + (jnp.log(nf / max_exact) / math.log(T5_MAX_DIST / max_exact)
                         * (half - max_exact)).astype(jnp.int32)
    large = jnp.minimum(large, half - 1)
    return ret + jnp.where(n < max_exact, n, large)


def _bias_kernel(tab_ref, bkt_ref, o_ref):
    for u in range(3):
        bk = bkt_ref[u]
        for h in range(A_HEADS):
            acc = jnp.zeros((Q_BLOCK, Q_BLOCK), F32)
            for b in range(T5_BUCKETS):
                acc = jnp.where(bk == b, tab_ref[b, h], acc)
            o_ref[u, h] = acc * LOG2E


def _bias_tiles(t5_table):
    i = jnp.arange(Q_BLOCK, dtype=I32)[:, None]
    j = jnp.arange(Q_BLOCK, dtype=I32)[None, :]
    rel = jnp.stack([j - i - 2 * Q_BLOCK, j - i - Q_BLOCK, j - i])
    bkt = _t5_bucket(rel).astype(I32)
    return pl.pallas_call(
        _bias_kernel,
        in_specs=[pl.BlockSpec(memory_space=pltpu.SMEM),
                  pl.BlockSpec((3, Q_BLOCK, Q_BLOCK), lambda: (0, 0, 0))],
        out_specs=pl.BlockSpec((3, A_HEADS, Q_BLOCK, Q_BLOCK), lambda: (0, 0, 0, 0)),
        out_shape=jax.ShapeDtypeStruct((3, A_HEADS, Q_BLOCK, Q_BLOCK), F32),
        name="t5_bias",
    )(t5_table, bkt)


def _side_cast(src_ref, dst_ref, row0=None, rows_valid=None):
    r, n = src_ref.shape
    w = src_ref[...]
    if rows_valid is not None:
        rid = lax.broadcasted_iota(I32, (r, n), 0) + row0
        w = jnp.where(rid < rows_valid, w, 0.0)
    dst_ref[:, :n] = w.astype(BF16)
    if dst_ref.shape[1] > n:
        dst_ref[:, n:] = jnp.zeros((r, dst_ref.shape[1] - n), BF16)


def _dsa_kernel(aq_ref, iq_ref, sm_ref, kbd_ref, kt_ref, v_ref, tiles_ref, exp_ref, wa_ref, wc_ref,
                o_ref, wa_o_ref, wc_o_ref,
                key_ref, nm_ref, wb_ref, t_ref, q_ref, lhs_ref, m_ref, l_ref, acc_ref, x_ref,
                *, nkb_max, topk):
    _side_cast(wa_ref, wa_o_ref)
    _side_cast(wc_ref, wc_o_ref)
    pb = pl.program_id(1)
    nkb = pb + 1
    tq = DSA_ROWS
    grows = DSA_HEAD_GROUP * tq
    half = KEY_BLOCK // 2
    pairs_per_dot = 8

    iw = jnp.dot(sm_ref[:, IW_LO:IW_LO + IDX_HEADS], exp_ref[...], preferred_element_type=F32)
    for hh in range(IDX_HEADS):
        wb_ref[hh] = iw[:, hh * LANES:(hh + 1) * LANES] * (IDX_DIM ** -0.5 * IDX_HEADS ** -0.5)
    for p in range(IDX_HEADS // 2):
        lhs_ref[p * tq:(p + 1) * tq, :] = iq_ref[:, p * LANES:(p + 1) * LANES]
    for h in range(A_HEADS):
        q_ref[h * tq:(h + 1) * tq, :] = aq_ref[:, h * LANES:(h + 1) * LANES]

    row_chunk = lax.shift_right_logical(lax.broadcasted_iota(I32, (tq, KEY_BLOCK), 0) + pb * tq, 6)
    col_iota = lax.broadcasted_iota(I32, (tq, KEY_BLOCK), 1)

    def idx_body(kb, carry):
        acc = jnp.zeros((tq, KEY_BLOCK), F32)
        for p0 in range(0, IDX_HEADS // 2, pairs_per_dot):
            res = jnp.dot(lhs_ref[p0 * tq:(p0 + pairs_per_dot) * tq, :], kbd_ref[kb],
                          preferred_element_type=F32)
            for p in range(pairs_per_dot):
                r = res[p * tq:(p + 1) * tq]
                we = wb_ref[2 * (p0 + p)]
                wo = wb_ref[2 * (p0 + p) + 1]
                acc = acc + jnp.maximum(r[:, :KEY_BLOCK], 0.0) * jnp.concatenate([we, we], axis=1)
                acc = acc + jnp.maximum(r[:, KEY_BLOCK:], 0.0) * jnp.concatenate([wo, wo], axis=1)
        bits = pltpu.bitcast(acc, I32)
        skey = bits ^ (lax.shift_right_arithmetic(bits, 31) & 0x7FFFFFFF)
        skey = jnp.where(acc == 0.0, 0, skey)
        adm = lax.shift_right_logical(col_iota + kb * KEY_BLOCK, 6) <= row_chunk
        key_ref[kb] = jnp.where(adm, skey, INT_MIN)
        return carry

    lax.fori_loop(0, nkb, idx_body, 0)

    def search(n):
        hr = tq // 2

        def partial_counts(cand, r0):
            c = jnp.zeros((hr, LANES), F32)
            for kb in range(n):
                k = key_ref[kb, r0:r0 + hr, :]
                c = c + jnp.where(k[:, :half] >= cand, 1.0, 0.0) + jnp.where(k[:, half:] >= cand, 1.0, 0.0)
            return c

        def enough(c):
            return jnp.broadcast_to(jnp.sum(c, axis=1, keepdims=True), (hr, LANES)) >= topk

        def bit(s):
            return lax.shift_left(jnp.int32(1), 31 - s)

        t_init = jnp.full((hr, LANES), INT_MIN, I32)
        cand_b = t_init + bit(0)
        c_b = partial_counts(cand_b, hr)

        def body(s, carry):
            t_a, t_b, cand_b, c_b = carry
            ok_b = enough(c_b)
            cand_a = t_a + bit(s)
            ok_a = enough(partial_counts(cand_a, 0))
            t_b = jnp.where(ok_b, cand_b, t_b)
            cand_b = t_b + bit(s + 1)
            c_b = partial_counts(cand_b, hr)
            t_a = jnp.where(ok_a, cand_a, t_a)
            return t_a, t_b, cand_b, c_b

        t_a, t_b, cand_b, c_b = lax.fori_loop(0, 31, body, (t_init, t_init, cand_b, c_b))
        cand_a = t_a + bit(31)
        t_a = jnp.where(enough(partial_counts(cand_a, 0)), cand_a, t_a)
        t_b = jnp.where(enough(c_b), cand_b, t_b)
        t_ref[0:hr, :] = jnp.maximum(t_a, INT_MIN + 1)
        t_ref[hr:tq, :] = jnp.maximum(t_b, INT_MIN + 1)

    all_selected = (pb + 1) * tq <= topk

    @pl.when(all_selected)
    def _():
        t_ref[...] = jnp.full((tq, LANES), INT_MIN + 1, I32)

    for n in range(1, nkb_max + 1):
        pl.when(jnp.logical_and(nkb == n, jnp.logical_not(all_selected)))(functools.partial(search, n))

    thr = t_ref[...]
    thr2 = jnp.concatenate([thr, thr], axis=1)

    def cnt_body(kb, c):
        g = jnp.where(key_ref[kb] >= thr2, 1.0, 0.0)
        return c + g[:, :half] + g[:, half:]

    n_ge = jnp.sum(lax.fori_loop(0, nkb, cnt_body, jnp.zeros((tq, LANES), F32)),
                   axis=1, keepdims=True)
    has_ties = jnp.max(n_ge) > topk

    @pl.when(jnp.logical_not(has_ties))
    def _():
        def body(kb, carry):
            nm_ref[kb] = jnp.where(key_ref[kb] >= thr2, 0.0, NEG_MASK)
            return carry
        lax.fori_loop(0, nkb, body, 0)

    @pl.when(has_ties)
    def _():
        def gt_body(kb, c):
            g = jnp.where(key_ref[kb] > thr2, 1.0, 0.0)
            return c + g[:, :half] + g[:, half:]
        n_gt = jnp.sum(lax.fori_loop(0, nkb, gt_body, jnp.zeros((tq, LANES), F32)),
                       axis=1, keepdims=True)
        need = topk - n_gt
        tri = (lax.broadcasted_iota(I32, (half, half), 0)
               <= lax.broadcasted_iota(I32, (half, half), 1)).astype(BF16)

        def body(kb, seen):
            k = key_ref[kb]
            parts = []
            for s in range(2):
                ks = k[:, s * half:(s + 1) * half]
                eq = ks == thr
                eqf = jnp.where(eq, 1.0, 0.0)
                rank = jnp.dot(eqf.astype(BF16), tri, preferred_element_type=F32) + seen
                keep = jnp.logical_or(ks > thr, jnp.logical_and(eq, rank <= need))
                parts.append(jnp.where(keep, 0.0, NEG_MASK))
                seen = seen + jnp.sum(eqf, axis=1, keepdims=True)
            nm_ref[kb] = jnp.concatenate(parts, axis=1)
            return seen
        lax.fori_loop(0, nkb, body, jnp.zeros((tq, 1), F32))

    scale = A_HEAD_DIM ** -0.5 * LOG2E

    def tile_of(g, qblk):
        return jnp.where(g == qblk, 2, jnp.where(g == qblk - 1, 1, 0))

    for h0 in range(0, A_HEADS, DSA_HEAD_GROUP):
        hs = slice(h0, h0 + DSA_HEAD_GROUP)
        qrows = slice(h0 * tq, (h0 + DSA_HEAD_GROUP) * tq)
        m_ref[...] = jnp.full((grows, LANES), NEG_MASK, F32)
        l_ref[...] = jnp.zeros((grows, LANES), F32)
        acc_ref[...] = jnp.zeros((grows, A_HEAD_DIM), F32)

        def max_body(kb, carry, hs=hs, qrows=qrows):
            x = jnp.dot(q_ref[qrows, :], kt_ref[kb], preferred_element_type=F32) * scale
            per_qblk = []
            for r in range(tq // Q_BLOCK):
                qblk = (tq // Q_BLOCK) * pb + r
                per_qblk.append(jnp.concatenate([tiles_ref[tile_of(2 * kb, qblk), hs],
                                                 tiles_ref[tile_of(2 * kb + 1, qblk), hs]], axis=2))
            bias = jnp.concatenate(per_qblk, axis=1)
            x = (x.reshape(DSA_HEAD_GROUP, tq, KEY_BLOCK) + bias + nm_ref[kb][None]).reshape(grows, KEY_BLOCK)
            x_ref[kb] = x
            m_ref[...] = jnp.maximum(m_ref[...], jnp.maximum(x[:, :half], x[:, half:]))
            return carry

        lax.fori_loop(0, nkb, max_body, 0)
        m_ref[...] = jnp.broadcast_to(jnp.max(m_ref[...], axis=1, keepdims=True), (grows, LANES))

        def pv_body(kb, carry):
            m = m_ref[...]
            p = jnp.exp2(x_ref[kb] - jnp.concatenate([m, m], axis=1))
            l_ref[...] += p[:, :half] + p[:, half:]
            vb = v_ref[pl.ds(pl.multiple_of(kb * KEY_BLOCK, KEY_BLOCK), KEY_BLOCK), :]
            acc_ref[...] += jnp.dot(p.astype(BF16), vb, preferred_element_type=F32)
            return carry

        lax.fori_loop(0, nkb, pv_body, 0)

        out = acc_ref[...] / jnp.sum(l_ref[...], axis=1, keepdims=True)
        for h in range(DSA_HEAD_GROUP):
            o_ref[:, (h0 + h) * A_HEAD_DIM:(h0 + h + 1) * A_HEAD_DIM] = out[h * tq:(h + 1) * tq].astype(o_ref.dtype)


def _dsa(proj, kbd, kt, tiles, bsz, seq, side_weight, layer, side_cols, side_weight2):
    tq = DSA_ROWS
    nb = seq // tq
    nkb = seq // KEY_BLOCK
    topk = min(TOPK_MAX, seq // 4)
    grows = DSA_HEAD_GROUP * tq
    av_blk = _NEW_OFF["a_v"] // A_HEAD_DIM
    kern = functools.partial(_dsa_kernel, nkb_max=nkb, topk=topk)
    expand = jnp.asarray(np.arange(IDX_HEADS)[:, None] == np.arange(IDX_HEADS * LANES)[None, :] // LANES, BF16)
    wd, wn = side_weight.shape[1], side_weight.shape[2]
    assert wd % (bsz * nb) == 0 and (wd // (bsz * nb)) % 16 == 0 and side_cols >= wn
    wr = wd // (bsz * nb)
    wd2, wn2 = side_weight2.shape[1], side_weight2.shape[2]
    assert wd2 % (bsz * nb) == 0 and (wd2 // (bsz * nb)) % 16 == 0
    wr2 = wd2 // (bsz * nb)
    once = pl.Buffered(1)
    return pl.pallas_call(
        kern,
        grid=(bsz, nb),
        in_specs=[
            pl.BlockSpec((tq, A_Q), lambda b, q: (b * nb + q, _NEW_OFF["a_q"] // A_Q)),
            pl.BlockSpec((tq, IDX_Q), lambda b, q: (b * nb + q, _NEW_OFF["i_q"] // IDX_Q)),
            pl.BlockSpec((tq, LANES), lambda b, q: (b * nb + q, SMALL_OFF // LANES)),
            pl.BlockSpec((None, nkb, 2 * IDX_DIM, 2 * KEY_BLOCK), lambda b, q: (b, 0, 0, 0)),
            pl.BlockSpec((None, nkb, A_HEAD_DIM, KEY_BLOCK), lambda b, q: (b, 0, 0, 0)),
            pl.BlockSpec((seq, A_HEAD_DIM), lambda b, q: (b, av_blk)),
            pl.BlockSpec((3, A_HEADS, Q_BLOCK, Q_BLOCK), lambda b, q: (0, 0, 0, 0), pipeline_mode=once),
            pl.BlockSpec((IDX_HEADS, IDX_HEADS * LANES), lambda b, q: (0, 0), pipeline_mode=once),
            pl.BlockSpec((None, wr, wn), lambda b, q: (layer, b * nb + q, 0)),
            pl.BlockSpec((None, wr2, wn2), lambda b, q: (layer, b * nb + q, 0)),
        ],
        out_specs=[pl.BlockSpec((tq, A_Q), lambda b, q: (b * nb + q, 0)),
                   pl.BlockSpec((wr, side_cols), lambda b, q: (b * nb + q, 0)),
                   pl.BlockSpec((wr2, wn2), lambda b, q: (b * nb + q, 0))],
        out_shape=[jax.ShapeDtypeStruct((bsz * seq, A_Q), BF16),
                   jax.ShapeDtypeStruct((wd, side_cols), BF16),
                   jax.ShapeDtypeStruct((wd2, wn2), BF16)],
        scratch_shapes=[
            pltpu.VMEM((nkb, tq, KEY_BLOCK), I32),
            pltpu.VMEM((nkb, tq, KEY_BLOCK), F32),
            pltpu.VMEM((IDX_HEADS, tq, LANES), F32),
            pltpu.VMEM((tq, LANES), I32),
            pltpu.VMEM((A_HEADS * tq, A_HEAD_DIM), BF16),
            pltpu.VMEM((A_HEADS * tq, LANES), BF16),
            pltpu.VMEM((grows, LANES), F32),
            pltpu.VMEM((grows, LANES), F32),
            pltpu.VMEM((grows, A_HEAD_DIM), F32),
            pltpu.VMEM((nkb, grows, KEY_BLOCK), F32),
        ],
        compiler_params=_cparams(("arbitrary", "arbitrary"), VMEM_LIMIT_FFN),
        name="dsa",
    )(proj, proj, proj, kbd, kt, proj, tiles, expand, side_weight, side_weight2)


def _gla_kernel(q_ref, k_ref, v_ref, r_ref, sm_ref, wg_ref, bg_ref, nrm_ref, tri_ref, sel_ref, wd_ref,
                o_ref, wd_o_ref, st_ref, oacc_ref, *, nchunk, side_rows):
    step = pl.program_id(0) * pl.num_programs(1) + pl.program_id(1)
    _side_cast(wd_ref, wd_o_ref, step * wd_ref.shape[0], side_rows)
    @pl.when(pl.program_id(1) == 0)
    def _():
        st_ref[...] = jnp.zeros(st_ref.shape, F32)

    ts = nchunk * CHUNK
    ri = lax.broadcasted_iota(I32, (ts, ts), 0)
    ci = lax.broadcasted_iota(I32, (ts, ts), 1)
    causal = jnp.logical_and(lax.shift_right_logical(ri, 6) == lax.shift_right_logical(ci, 6), ci <= ri)
    tn = (((0,), (0,)), ((), ()))
    nt = (((1,), (1,)), ((), ()))
    tri = tri_ref[...]
    sel = sel_ref[...]
    glr = sm_ref[:, GLR_LO:GLR_LO + GATE_RANK]

    heads = range(B_HEADS)
    ksl = [slice(h * B_HEAD_K, (h + 1) * B_HEAD_K) for h in heads]
    vsl = [slice(h * B_HEAD_V, (h + 1) * B_HEAD_V) for h in heads]
    qe, ke, kd, dcol = [], [], [], []

    def gates(h):
        ks = ksl[h]
        z = jnp.dot(glr, wg_ref[:, ks], preferred_element_type=F32) + bg_ref[:, ks]
        g = (jnp.minimum(z, 0.0) - jnp.log(1.0 + jnp.exp(-jnp.abs(z)))) * (1.0 / GATE_TAU)
        g_hi = g.astype(BF16)
        g_lo = (g - g_hi.astype(F32)).astype(BF16)
        b = jnp.dot(tri, g_hi, preferred_element_type=F32) + jnp.dot(tri, g_lo, preferred_element_type=F32)
        bl = jnp.concatenate(
            [jnp.broadcast_to(b[(c + 1) * CHUNK - 1:(c + 1) * CHUNK, :], (CHUNK, B_HEAD_K)) for c in range(nchunk)],
            axis=0)
        qc = q_ref[:, ks].astype(F32) * (B_HEAD_K ** -0.5)
        kc = k_ref[:, ks].astype(F32)
        qe.append((qc * jnp.exp(b)).astype(BF16))
        ke.append((kc * jnp.exp(-b)).astype(BF16))
        kd.append((kc * jnp.exp(bl - b)).astype(BF16))
        dcol.append(jnp.exp(lax.dot_general(g_hi, sel, tn, preferred_element_type=F32)
                            + lax.dot_general(g_lo, sel, tn, preferred_element_type=F32)))

    def intra(h):
        a = lax.dot_general(qe[h], ke[h], nt, preferred_element_type=F32)
        a = jnp.where(causal, a, 0.0).astype(BF16)
        oacc_ref[:, vsl[h]] = jnp.dot(a, v_ref[:, vsl[h]], preferred_element_type=F32)

    gates(0)
    for h in heads:
        if h + 1 < B_HEADS:
            gates(h + 1)
        intra(h)

    st = [st_ref[h] for h in heads]
    for c in range(nchunk):
        rs = slice(c * CHUNK, (c + 1) * CHUNK)
        for h in heads:
            oacc_ref[rs, vsl[h]] += jnp.dot(qe[h][rs], st[h].astype(BF16), preferred_element_type=F32)
            dec = jnp.broadcast_to(dcol[h][:, c:c + 1], (B_HEAD_K, B_HEAD_V))
            st[h] = st[h] * dec + lax.dot_general(kd[h][rs], v_ref[rs, vsl[h]], tn, preferred_element_type=F32)
    for h in heads:
        st_ref[h] = st[h]
        o = oacc_ref[:, vsl[h]]
        ms = jnp.mean(o * o, axis=-1, keepdims=True)
        on = o * lax.rsqrt(ms + EPS) * nrm_ref[...]
        rr = r_ref[:, vsl[h]].astype(F32)
        o_ref[:, vsl[h]] = (on * (rr * jax.nn.sigmoid(rr))).astype(o_ref.dtype)


def _gla(proj, w_g2, b_g2, gla_norm, bsz, seq, side_weight, layer, side_rows_padded):
    q_blk = _NEW_OFF["g_q"] // B_QK
    k_blk = _NEW_OFF["g_k"] // B_QK
    v_blk = _NEW_OFF["g_v"] // B_V
    r_blk = _NEW_OFF["g_r"] // B_V
    ts = min(seq, GLA_SEQ_TILE)
    nt = seq // ts
    nchunk = ts // CHUNK
    src_rows, wdim = side_weight.shape[1], side_weight.shape[2]
    nsteps = bsz * nt
    assert side_rows_padded % nsteps == 0 and (side_rows_padded // nsteps) % 16 == 0
    sr = side_rows_padded // nsteps
    last_src = (src_rows - 1) // sr
    kern = functools.partial(_gla_kernel, nchunk=nchunk, side_rows=src_rows)
    rc = np.arange(ts)[:, None] // CHUNK
    cc = np.arange(ts)[None, :] // CHUNK
    tri = jnp.asarray((rc == cc) & (np.arange(ts)[None, :] <= np.arange(ts)[:, None]), BF16)
    sel = jnp.asarray(rc == np.arange(LANES)[None, :], BF16)
    const = lambda shape: pl.BlockSpec(shape, lambda b, t: (0, 0))
    return pl.pallas_call(
        kern,
        grid=(bsz, nt),
        in_specs=[
            pl.BlockSpec((ts, B_QK), lambda b, t: (b * nt + t, q_blk)),
            pl.BlockSpec((ts, B_QK), lambda b, t: (b * nt + t, k_blk)),
            pl.BlockSpec((ts, B_V), lambda b, t: (b * nt + t, v_blk)),
            pl.BlockSpec((ts, B_V), lambda b, t: (b * nt + t, r_blk)),
            pl.BlockSpec((ts, LANES), lambda b, t: (b * nt + t, SMALL_OFF // LANES)),
            pl.BlockSpec((GATE_RANK, B_QK), lambda b, t: (0, 0)),
            pl.BlockSpec((1, B_QK), lambda b, t: (0, 0)),
            pl.BlockSpec((1, B_HEAD_V), lambda b, t: (0, 0)),
            const((ts, ts)), const((ts, LANES)),
            pl.BlockSpec((None, sr, wdim), lambda b, t: (layer, jnp.minimum(b * nt + t, last_src), 0)),
        ],
        out_specs=[pl.BlockSpec((ts, B_V), lambda b, t: (b * nt + t, 0)),
                   pl.BlockSpec((sr, wdim), lambda b, t: (b * nt + t, 0))],
        out_shape=[jax.ShapeDtypeStruct((bsz * seq, B_V), BF16),
                   jax.ShapeDtypeStruct((side_rows_padded, wdim), BF16)],
        scratch_shapes=[pltpu.VMEM((B_HEADS, B_HEAD_K, B_HEAD_V), F32),
                        pltpu.VMEM((ts, B_V), F32)],
        compiler_params=_cparams(("arbitrary", "arbitrary")),
        name="gla",
    )(proj, proj, proj, proj, proj, w_g2.astype(BF16), b_g2.reshape(1, -1), gla_norm.reshape(1, -1),
      tri, sel, side_weight)


def _outproj_kernel(oa_ref, ob_ref, w_ref, x_ref, modj_ref, mod_ref, g_ref, b_ref, x1_ref, h2_ref,
                    *, alpha, nj, tn):
    j = pl.program_id(1)
    a = jnp.concatenate([oa_ref[...], ob_ref[...]], axis=1)
    y = jnp.dot(a, w_ref[...], preferred_element_type=F32)
    z = alpha * x_ref[...] + modj_ref[2:3, :] * y
    for jj in range(nj):
        @pl.when(j == jj)
        def _(jj=jj):
            x1_ref[:, jj * tn:(jj + 1) * tn] = z

    @pl.when(j == nj - 1)
    def _():
        def rows(rs):
            x1 = _layer_norm(x1_ref[rs, :]) * g_ref[...] + b_ref[...]
            x1_ref[rs, :] = x1
            h2_ref[rs, :] = (_layer_norm(x1) * (1.0 + mod_ref[4:5, :]) + mod_ref[3:4, :]).astype(h2_ref.dtype)
        _for_row_chunks(x1_ref.shape[0], ROW_CHUNK, rows)


def _outproj(o_a, o_b, w, x2, mod3, ln_g, ln_b, seq, alpha):
    m, d = x2.shape
    tm, tn = 512, 1024
    nj = d // tn
    rows_per_seq = seq // tm
    kern = functools.partial(_outproj_kernel, alpha=alpha, nj=nj, tn=tn)
    return pl.pallas_call(
        kern,
        grid=(m // tm, nj),
        in_specs=[
            pl.BlockSpec((tm, A_Q), lambda i, j: (i, 0)),
            pl.BlockSpec((tm, B_V), lambda i, j: (i, 0)),
            pl.BlockSpec((A_Q + B_V, tn), lambda i, j: (0, j)),
            pl.BlockSpec((tm, tn), lambda i, j: (i, j)),
            pl.BlockSpec((None, 6, tn), lambda i, j: (i // rows_per_seq, 0, j)),
            pl.BlockSpec((None, 6, d), lambda i, j: (i // rows_per_seq, 0, 0)),
            pl.BlockSpec((1, d), lambda i, j: (0, 0)),
            pl.BlockSpec((1, d), lambda i, j: (0, 0)),
        ],
        out_specs=[pl.BlockSpec((tm, d), lambda i, j: (i, 0)),
                   pl.BlockSpec((tm, d), lambda i, j: (i, 0))],
        out_shape=[jax.ShapeDtypeStruct((m, d), F32), jax.ShapeDtypeStruct((m, d), BF16)],
        compiler_params=_cparams(("arbitrary", "arbitrary"), VMEM_LIMIT_FFN),
        name="outproj",
    )(o_a, o_b, w, x2, mod3, mod3, ln_g.reshape(1, d), ln_b.reshape(1, d))


def _ffn_kernel(h_ref, wu_ref, wg_ref, cw_ref, cb_ref, wd_ref, x1_hbm, mod_ref, g_ref, b_ref, o_ref,
                carry_ref, act_ref, xbuf_ref, xsem, *, alpha, nj, rows_per_seq):
    i = pl.program_id(0)
    j = pl.program_id(1)
    tm = o_ref.shape[0]
    nchunks = tm // ROW_CHUNK

    def x1_copy(c, slot):
        row0 = pl.multiple_of(i * tm + c * ROW_CHUNK, ROW_CHUNK)
        return pltpu.make_async_copy(x1_hbm.at[pl.ds(row0, ROW_CHUNK), :], xbuf_ref.at[slot], xsem.at[slot])

    def up_gate():
        h = h_ref[...]
        tm = h.shape[0]
        u = jnp.dot(h, wu_ref[...], preferred_element_type=F32)
        gt = jnp.dot(h, wg_ref[...], preferred_element_type=F32)
        first = (i % rows_per_seq) == 0
        prev = jnp.where(first, 0.0, carry_ref[j])
        carry_ref[j] = u[tm - 8:tm, :]
        rid = lax.broadcasted_iota(I32, u.shape, 0)
        p1 = prev[7:8, :]
        p2 = prev[6:7, :]
        u1 = jnp.where(rid == 0, p1, pltpu.roll(u, 1, axis=0))
        u2 = jnp.where(rid == 0, p2, jnp.where(rid == 1, p1, pltpu.roll(u, 2, axis=0)))
        cv = cw_ref[0:1, :] * u2 + cw_ref[1:2, :] * u1 + cw_ref[2:3, :] * u + cb_ref[...]
        cdf = 0.5 * (1.0 + jnp.tanh(math.sqrt(2.0 / math.pi) * (cv + 0.044715 * (cv ** 3))))
        return (cv * cdf * gt).astype(BF16)

    def down(act):
        for c0 in range(0, o_ref.shape[1], FFN_DOWN_CHUNK):
            cs = slice(c0, c0 + FFN_DOWN_CHUNK)
            o_ref[:, cs] += jnp.dot(act, wd_ref[:, cs], preferred_element_type=F32)

    @pl.when(j == 0)
    def _():
        o_ref[...] = jnp.zeros(o_ref.shape, F32)
        act_ref[...] = up_gate()

    @pl.when(jnp.logical_and(j > 0, j < nj))
    def _():
        act_prev = act_ref[...]
        act_ref[...] = up_gate()
        down(act_prev)

    @pl.when(j == nj - 1)
    def _():
        x1_copy(0, 0).start()

    @pl.when(j == nj)
    def _():
        down(act_ref[...])

        def body(c, carry):
            slot = c % 2

            @pl.when(c + 1 < nchunks)
            def _():
                x1_copy(c + 1, 1 - slot).start()

            x1_copy(c, slot).wait()
            rs = pl.ds(pl.multiple_of(c * ROW_CHUNK, ROW_CHUNK), ROW_CHUNK)
            z = alpha * xbuf_ref[slot] + mod_ref[5:6, :] * o_ref[rs, :]
            o_ref[rs, :] = _layer_norm(z) * g_ref[...] + b_ref[...]
            return carry

        lax.fori_loop(0, nchunks, body, 0)


def _ffn(h2, w_up, w_gate, conv_w, conv_b, w_down, x1, mod3, ln_g, ln_b, seq, alpha):
    m, d = h2.shape
    tm, tf = 512, FFN_TILE
    f = w_up.shape[1]
    pad = f - conv_w.shape[1]
    conv_w = jnp.pad(conv_w, ((0, 0), (0, pad)))
    conv_b = jnp.pad(conv_b, (0, pad))
    nj = f // tf
    rows_per_seq = seq // tm
    kern = functools.partial(_ffn_kernel, alpha=alpha, nj=nj, rows_per_seq=rows_per_seq)
    up_tile = lambda i, j: (0, jnp.minimum(j, nj - 1))
    return pl.pallas_call(
        kern,
        grid=(m // tm, nj + 1),
        in_specs=[
            pl.BlockSpec((tm, d), lambda i, j: (i, 0)),
            pl.BlockSpec((d, tf), up_tile),
            pl.BlockSpec((d, tf), up_tile),
            pl.BlockSpec((CONV_W, tf), up_tile),
            pl.BlockSpec((1, tf), up_tile),
            pl.BlockSpec((tf, d), lambda i, j: (jnp.maximum(j - 1, 0), 0)),
            pl.BlockSpec(memory_space=pl.ANY),
            pl.BlockSpec((None, 6, d), lambda i, j: (i // rows_per_seq, 0, 0)),
            pl.BlockSpec((1, d), lambda i, j: (0, 0)),
            pl.BlockSpec((1, d), lambda i, j: (0, 0)),
        ],
        out_specs=pl.BlockSpec((tm, d), lambda i, j: (i, 0)),
        out_shape=jax.ShapeDtypeStruct((m, d), F32),
        scratch_shapes=[pltpu.VMEM((nj, 8, tf), F32),
                        pltpu.VMEM((tm, tf), BF16),
                        pltpu.VMEM((2, ROW_CHUNK, d), F32),
                        pltpu.SemaphoreType.DMA((2,))],
        compiler_params=_cparams(("arbitrary", "arbitrary"), VMEM_LIMIT_FFN),
        name="ffn",
    )(h2, w_up, w_gate, conv_w, conv_b.reshape(1, f), w_down, x1, mod3, ln_g.reshape(1, d), ln_b.reshape(1, d))


def _regroup_kernel(w_ref, o_ref):
    for n in PROJ_ORDER:
        o_ref[_NEW_OFF[n]:_NEW_OFF[n] + _WIDTH[n], :] = w_ref[_OLD_OFF[n]:_OLD_OFF[n] + _WIDTH[n], :].astype(BF16)
    o_ref[_PROJ_USED:, :] = jnp.zeros((PROJ_COLS - _PROJ_USED, o_ref.shape[1]), BF16)


def _regroup_w_in(w, l):
    wt = jnp.swapaxes(w, 1, 2)
    _, n, d = wt.shape
    tc = 256
    return pl.pallas_call(
        _regroup_kernel,
        grid=(d // tc,),
        in_specs=[pl.BlockSpec((None, n, tc), lambda i: (l, 0, i))],
        out_specs=pl.BlockSpec((PROJ_COLS, tc), lambda i: (0, i)),
        out_shape=jax.ShapeDtypeStruct((PROJ_COLS, d), BF16),
        compiler_params=_cparams(("arbitrary",)),
        name="regroup_w_in",
    )(wt)


def kernel(x, c, t5_table, w_ada, b_ada, w_in, w_g2, b_g2, gla_norm, w_out, ln1_g, ln1_b, w_up, w_gate,
           conv_w, conv_b, w_down, ln2_g, ln2_b):
    bsz, seq, d = x.shape
    depth = w_ada.shape[0]
    alpha = (2 * depth) ** 0.25
    nkb = seq // KEY_BLOCK
    x2 = x.reshape(bsz * seq, d)
    tiles = _bias_tiles(t5_table)
    for l in range(depth):
        mod3 = _ada(c, w_ada[l], b_ada[l]).reshape(bsz, 6, d)
        f_pad = -(-w_up.shape[2] // FFN_TILE) * FFN_TILE
        proj, w_gate_b = _inproj(x2, mod3, _regroup_w_in(w_in, l), seq, w_gate, l, f_pad)

        p3 = proj.reshape(bsz, nkb, KEY_BLOCK, PROJ_COLS)
        ko, io = _NEW_OFF["a_k"], _NEW_OFF["i_k"]
        kt = jnp.swapaxes(p3[..., ko:ko + A_HEAD_DIM], 2, 3)
        kit = jnp.swapaxes(p3[..., io:io + IDX_DIM], 2, 3)
        zz = jnp.zeros_like(kit)
        kbd = jnp.concatenate([jnp.concatenate([kit, zz], axis=3),
                               jnp.concatenate([zz, kit], axis=3)], axis=2)

        o_a, w_up_b, w_out_b = _dsa(proj, kbd, kt, tiles, bsz, seq, w_up, l, f_pad, w_out)
        o_b, w_down_b = _gla(proj, w_g2[l], b_g2[l], gla_norm[l], bsz, seq, w_down, l, f_pad)
        x1, h2 = _outproj(o_a, o_b, w_out_b, x2, mod3, ln1_g[l], ln1_b[l], seq, alpha)
        x2 = _ffn(h2, w_up_b, w_gate_b, conv_w[l], conv_b[l], w_down_b, x1, mod3, ln2_g[l], ln2_b[l],
                  seq, alpha)
    return x2.reshape(bsz, seq, d)
```

```python
import functools
import math

import numpy as np
import jax
import jax.numpy as jnp
from jax import lax
from jax.experimental import pallas as pl
from jax.experimental.pallas import tpu as pltpu

F32 = jnp.float32
BF16 = jnp.bfloat16
I32 = jnp.int32

CHUNK = 64
Q_BLOCK = 128
A_HEADS = 16
A_HEAD_DIM = 128
IDX_HEADS = 32
IDX_DIM = 64
TOPK_MAX = 256
T5_BUCKETS = 32
T5_MAX_DIST = 128
B_HEADS = 4
B_HEAD_V = 512
B_HEAD_K = 256
GATE_RANK = 16
GATE_TAU = 16.0
CONV_W = 3
EPS = 1e-6

A_Q = A_HEADS * A_HEAD_DIM
IDX_Q = IDX_HEADS * IDX_DIM
B_QK = B_HEADS * B_HEAD_K
B_V = B_HEADS * B_HEAD_V
IN_SPLITS = (A_Q, A_HEAD_DIM, A_HEAD_DIM, IDX_Q, IDX_DIM, IDX_HEADS, B_QK, B_QK, B_V, GATE_RANK, B_V)
IN_NAMES = ("a_q", "a_k", "a_v", "i_q", "i_k", "i_w", "g_q", "g_k", "g_v", "g_lr", "g_r")
PROJ_ORDER = ("a_q", "i_q", "g_v", "g_r", "g_q", "g_k", "a_k", "a_v", "i_k", "i_w", "g_lr")
PROJ_TILE = 1536

LANES = 128
VMEM_LIMIT = 56 * 1024 * 1024
VMEM_LIMIT_FFN = 60 * 1024 * 1024

KEY_BLOCK = 2 * Q_BLOCK
DSA_ROWS = KEY_BLOCK
DSA_HEAD_GROUP = 8
NEG_MASK = -1e30
LOG2E = math.log2(math.e)
INT_MIN = -(2 ** 31)


def _proj_layout():
    old_off = dict(zip(IN_NAMES, np.concatenate([[0], np.cumsum(IN_SPLITS)[:-1]]).tolist()))
    width = dict(zip(IN_NAMES, IN_SPLITS))
    new_off, pos = {}, 0
    for name in PROJ_ORDER:
        new_off[name] = pos
        pos += width[name]
    total = -(-pos // PROJ_TILE) * PROJ_TILE
    return old_off, width, new_off, pos, total


_OLD_OFF, _WIDTH, _NEW_OFF, _PROJ_USED, PROJ_COLS = _proj_layout()
SMALL_OFF = _NEW_OFF["i_k"]
IK_LO = 0
IW_LO = _NEW_OFF["i_w"] - SMALL_OFF
GLR_LO = _NEW_OFF["g_lr"] - SMALL_OFF


def _layer_norm(x):
    mu = jnp.mean(x, axis=-1, keepdims=True)
    xc = x - mu
    var = jnp.mean(xc * xc, axis=-1, keepdims=True)
    return xc * lax.rsqrt(var + EPS)


def _for_row_chunks(nrows, chunk, fn):
    def body(r, carry):
        fn(pl.ds(pl.multiple_of(r * chunk, chunk), chunk))
        return carry
    lax.fori_loop(0, nrows // chunk, body, 0)


ROW_CHUNK = 64
FFN_TILE = 512
GLA_SEQ_TILE = 512
FFN_DOWN_CHUNK = 512


def _cparams(sem, vmem=VMEM_LIMIT):
    return pltpu.CompilerParams(dimension_semantics=sem, vmem_limit_bytes=vmem)


def _ada_kernel(c_ref, w_ref, b_ref, o_ref):
    c = c_ref[...]
    ca = (c * jax.nn.sigmoid(c)).astype(BF16)
    o_ref[...] = jnp.dot(ca, w_ref[...].astype(BF16), preferred_element_type=F32) + b_ref[...]


def _ada(c, w, b):
    bsz, d = c.shape
    n = w.shape[1]
    tn = 512
    return pl.pallas_call(
        _ada_kernel,
        grid=(n // tn,),
        in_specs=[pl.BlockSpec((bsz, d), lambda j: (0, 0)),
                  pl.BlockSpec((d, tn), lambda j: (0, j)),
                  pl.BlockSpec((1, tn), lambda j: (0, j))],
        out_specs=pl.BlockSpec((bsz, tn), lambda j: (0, j)),
        out_shape=jax.ShapeDtypeStruct((bsz, n), F32),
        compiler_params=_cparams(("arbitrary",)),
        name="ada",
    )(c, w, b.reshape(1, n))


def _inproj_kernel(x_ref, mod_ref, w_ref, ws_ref, o_ref, ws_o_ref, h_ref):
    _side_cast(ws_ref, ws_o_ref)

    @pl.when(pl.program_id(1) == 0)
    def _():
        def rows(rs):
            xn = _layer_norm(x_ref[rs, :])
            h_ref[rs, :] = (xn * (1.0 + mod_ref[1:2, :]) + mod_ref[0:1, :]).astype(BF16)
        _for_row_chunks(x_ref.shape[0], ROW_CHUNK, rows)

    o_ref[...] = lax.dot_general(h_ref[...], w_ref[...], (((1,), (1,)), ((), ())),
                                 preferred_element_type=F32).astype(o_ref.dtype)


def _side_block_rows(total, nsteps):
    return next(r for r in range(16, total + 1, 16) if total % r == 0 and total // r <= nsteps)


def _inproj(x2, mod3, wt, seq, side_weight, layer, side_cols):
    m, d = x2.shape
    n = wt.shape[0]
    tm, tn = 512, PROJ_TILE
    nj = n // tn
    rows_per_seq = seq // tm
    wd, wn = side_weight.shape[1], side_weight.shape[2]
    wr = _side_block_rows(wd, (m // tm) * nj)
    blk = lambda i, j: jnp.minimum(i * nj + j, wd // wr - 1)
    return pl.pallas_call(
        _inproj_kernel,
        grid=(m // tm, nj),
        in_specs=[pl.BlockSpec((tm, d), lambda i, j: (i, 0)),
                  pl.BlockSpec((None, 6, d), lambda i, j: (i // rows_per_seq, 0, 0)),
                  pl.BlockSpec((tn, d), lambda i, j: (j, 0)),
                  pl.BlockSpec((None, wr, wn), lambda i, j: (layer, blk(i, j), 0))],
        out_specs=[pl.BlockSpec((tm, tn), lambda i, j: (i, j)),
                   pl.BlockSpec((wr, side_cols), lambda i, j: (blk(i, j), 0))],
        out_shape=[jax.ShapeDtypeStruct((m, n), BF16), jax.ShapeDtypeStruct((wd, side_cols), BF16)],
        scratch_shapes=[pltpu.VMEM((tm, d), BF16)],
        compiler_params=_cparams(("arbitrary", "arbitrary"), VMEM_LIMIT_FFN),
        name="inproj",
    )(x2, mod3, wt, side_weight)


def _t5_bucket(rel):
    half = T5_BUCKETS // 2
    max_exact = half // 2
    ret = jnp.where(rel > 0, half, 0)
    n = jnp.abs(rel)
    nf = jnp.maximum(n, 1).astype(jnp.float32)
    large = max_exact + (jnp.log(nf / max_exact) / math.log(T5_MAX_DIST / max_exact)
                         * (half - max_exact)).astype(jnp.int32)
    large = jnp.minimum(large, half - 1)
    return ret + jnp.where(n < max_exact, n, large)


def _bias_kernel(tab_ref, bkt_ref, o_ref):
    for u in range(3):
        bk = bkt_ref[u]
        for h in range(A_HEADS):
            acc = jnp.zeros((Q_BLOCK, Q_BLOCK), F32)
            for b in range(T5_BUCKETS):
                acc = jnp.where(bk == b, tab_ref[b, h], acc)
            o_ref[u, h] = acc * LOG2E


def _bias_tiles(t5_table):
    i = jnp.arange(Q_BLOCK, dtype=I32)[:, None]
    j = jnp.arange(Q_BLOCK, dtype=I32)[None, :]
    rel = jnp.stack([j - i - 2 * Q_BLOCK, j - i - Q_BLOCK, j - i])
    bkt = _t5_bucket(rel).astype(I32)
    return pl.pallas_call(
        _bias_kernel,
        in_specs=[pl.BlockSpec(memory_space=pltpu.SMEM),
                  pl.BlockSpec((3, Q_BLOCK, Q_BLOCK), lambda: (0, 0, 0))],
        out_specs=pl.BlockSpec((3, A_HEADS, Q_BLOCK, Q_BLOCK), lambda: (0, 0, 0, 0)),
        out_shape=jax.ShapeDtypeStruct((3, A_HEADS, Q_BLOCK, Q_BLOCK), F32),
        name="t5_bias",
    )(t5_table, bkt)


def _side_cast(src_ref, dst_ref, row0=None, rows_valid=None):
    r, n = src_ref.shape
    w = src_ref[...]
    if rows_valid is not None:
        rid = lax.broadcasted_iota(I32, (r, n), 0) + row0
        w = jnp.where(rid < rows_valid, w, 0.0)
    dst_ref[:, :n] = w.astype(BF16)
    if dst_ref.shape[1] > n:
        dst_ref[:, n:] = jnp.zeros((r, dst_ref.shape[1] - n), BF16)


def _dsa_kernel(aq_ref, iq_ref, sm_ref, kbd_ref, kt_ref, v_ref, tiles_ref, exp_ref, wa_ref, wc_ref,
                o_ref, wa_o_ref, wc_o_ref,
                key_ref, nm_ref, wb_ref, t_ref, q_ref, lhs_ref, m_ref, l_ref, acc_ref, x_ref,
                *, nkb_max, topk):
    _side_cast(wa_ref, wa_o_ref)
    _side_cast(wc_ref, wc_o_ref)
    pb = pl.program_id(1)
    nkb = pb + 1
    tq = DSA_ROWS
    grows = DSA_HEAD_GROUP * tq
    half = KEY_BLOCK // 2
    pairs_per_dot = 8

    iw = jnp.dot(sm_ref[:, IW_LO:IW_LO + IDX_HEADS], exp_ref[...], preferred_element_type=F32)
    for hh in range(IDX_HEADS):
        wb_ref[hh] = iw[:, hh * LANES:(hh + 1) * LANES] * (IDX_DIM ** -0.5 * IDX_HEADS ** -0.5)
    for p in range(IDX_HEADS // 2):
        lhs_ref[p * tq:(p + 1) * tq, :] = iq_ref[:, p * LANES:(p + 1) * LANES]
    for h in range(A_HEADS):
        q_ref[h * tq:(h + 1) * tq, :] = aq_ref[:, h * LANES:(h + 1) * LANES]

    row_chunk = lax.shift_right_logical(lax.broadcasted_iota(I32, (tq, KEY_BLOCK), 0) + pb * tq, 6)
    col_iota = lax.broadcasted_iota(I32, (tq, KEY_BLOCK), 1)

    def idx_body(kb, carry):
        acc = jnp.zeros((tq, KEY_BLOCK), F32)
        for p0 in range(0, IDX_HEADS // 2, pairs_per_dot):
            res = jnp.dot(lhs_ref[p0 * tq:(p0 + pairs_per_dot) * tq, :], kbd_ref[kb],
                          preferred_element_type=F32)
            for p in range(pairs_per_dot):
                r = res[p * tq:(p + 1) * tq]
                we = wb_ref[2 * (p0 + p)]
                wo = wb_ref[2 * (p0 + p) + 1]
                acc = acc + jnp.maximum(r[:, :KEY_BLOCK], 0.0) * jnp.concatenate([we, we], axis=1)
                acc = acc + jnp.maximum(r[:, KEY_BLOCK:], 0.0) * jnp.concatenate([wo, wo], axis=1)
        bits = pltpu.bitcast(acc, I32)
        skey = bits ^ (lax.shift_right_arithmetic(bits, 31) & 0x7FFFFFFF)
        skey = jnp.where(acc == 0.0, 0, skey)
        adm = lax.shift_right_logical(col_iota + kb * KEY_BLOCK, 6) <= row_chunk
        key_ref[kb] = jnp.where(adm, skey, INT_MIN)
        return carry

    lax.fori_loop(0, nkb, idx_body, 0)

    def search(n):
        hr = tq // 2

        def partial_counts(cand, r0):
            c = jnp.zeros((hr, LANES), F32)
            for kb in range(n):
                k = key_ref[kb, r0:r0 + hr, :]
                c = c + jnp.where(k[:, :half] >= cand, 1.0, 0.0) + jnp.where(k[:, half:] >= cand, 1.0, 0.0)
            return c

        def enough(c):
            return jnp.broadcast_to(jnp.sum(c, axis=1, keepdims=True), (hr, LANES)) >= topk

        def bit(s):
            return lax.shift_left(jnp.int32(1), 31 - s)

        t_init = jnp.full((hr, LANES), INT_MIN, I32)
        cand_b = t_init + bit(0)
        c_b = partial_counts(cand_b, hr)

        def body(s, carry):
            t_a, t_b, cand_b, c_b = carry
            ok_b = enough(c_b)
            cand_a = t_a + bit(s)
            ok_a = enough(partial_counts(cand_a, 0))
            t_b = jnp.where(ok_b, cand_b, t_b)
            cand_b = t_b + bit(s + 1)
            c_b = partial_counts(cand_b, hr)
            t_a = jnp.where(ok_a, cand_a, t_a)
            return t_a, t_b, cand_b, c_b

        t_a, t_b, cand_b, c_b = lax.fori_loop(0, 31, body, (t_init, t_init, cand_b, c_b))
        cand_a = t_a + bit(31)
        t_a = jnp.where(enough(partial_counts(cand_a, 0)), cand_a, t_a)
        t_b = jnp.where(enough(c_b), cand_b, t_b)
        t_ref[0:hr, :] = jnp.maximum(t_a, INT_MIN + 1)
        t_ref[hr:tq, :] = jnp.maximum(t_b, INT_MIN + 1)

    all_selected = (pb + 1) * tq <= topk

    @pl.when(all_selected)
    def _():
        t_ref[...] = jnp.full((tq, LANES), INT_MIN + 1, I32)

    for n in range(1, nkb_max + 1):
        pl.when(jnp.logical_and(nkb == n, jnp.logical_not(all_selected)))(functools.partial(search, n))

    thr = t_ref[...]
    thr2 = jnp.concatenate([thr, thr], axis=1)

    def cnt_body(kb, c):
        g = jnp.where(key_ref[kb] >= thr2, 1.0, 0.0)
        return c + g[:, :half] + g[:, half:]

    n_ge = jnp.sum(lax.fori_loop(0, nkb, cnt_body, jnp.zeros((tq, LANES), F32)),
                   axis=1, keepdims=True)
    has_ties = jnp.max(n_ge) > topk

    @pl.when(jnp.logical_not(has_ties))
    def _():
        def body(kb, carry):
            nm_ref[kb] = jnp.where(key_ref[kb] >= thr2, 0.0, NEG_MASK)
            return carry
        lax.fori_loop(0, nkb, body, 0)

    @pl.when(has_ties)
    def _():
        def gt_body(kb, c):
            g = jnp.where(key_ref[kb] > thr2, 1.0, 0.0)
            return c + g[:, :half] + g[:, half:]
        n_gt = jnp.sum(lax.fori_loop(0, nkb, gt_body, jnp.zeros((tq, LANES), F32)),
                       axis=1, keepdims=True)
        need = topk - n_gt
        tri = (lax.broadcasted_iota(I32, (half, half), 0)
               <= lax.broadcasted_iota(I32, (half, half), 1)).astype(BF16)

        def body(kb, seen):
            k = key_ref[kb]
            parts = []
            for s in range(2):
                ks = k[:, s * half:(s + 1) * half]
                eq = ks == thr
                eqf = jnp.where(eq, 1.0, 0.0)
                rank = jnp.dot(eqf.astype(BF16), tri, preferred_element_type=F32) + seen
                keep = jnp.logical_or(ks > thr, jnp.logical_and(eq, rank <= need))
                parts.append(jnp.where(keep, 0.0, NEG_MASK))
                seen = seen + jnp.sum(eqf, axis=1, keepdims=True)
            nm_ref[kb] = jnp.concatenate(parts, axis=1)
            return seen
        lax.fori_loop(0, nkb, body, jnp.zeros((tq, 1), F32))

    scale = A_HEAD_DIM ** -0.5 * LOG2E

    def tile_of(g, qblk):
        return jnp.where(g == qblk, 2, jnp.where(g == qblk - 1, 1, 0))

    for h0 in range(0, A_HEADS, DSA_HEAD_GROUP):
        hs = slice(h0, h0 + DSA_HEAD_GROUP)
        qrows = slice(h0 * tq, (h0 + DSA_HEAD_GROUP) * tq)
        m_ref[...] = jnp.full((grows, LANES), NEG_MASK, F32)
        l_ref[...] = jnp.zeros((grows, LANES), F32)
        acc_ref[...] = jnp.zeros((grows, A_HEAD_DIM), F32)

        def max_body(kb, carry, hs=hs, qrows=qrows):
            x = jnp.dot(q_ref[qrows, :], kt_ref[kb], preferred_element_type=F32) * scale
            per_qblk = []
            for r in range(tq // Q_BLOCK):
                qblk = (tq // Q_BLOCK) * pb + r
                per_qblk.append(jnp.concatenate([tiles_ref[tile_of(2 * kb, qblk), hs],
                                                 tiles_ref[tile_of(2 * kb + 1, qblk), hs]], axis=2))
            bias = jnp.concatenate(per_qblk, axis=1)
            x = (x.reshape(DSA_HEAD_GROUP, tq, KEY_BLOCK) + bias + nm_ref[kb][None]).reshape(grows, KEY_BLOCK)
            x_ref[kb] = x
            m_ref[...] = jnp.maximum(m_ref[...], jnp.maximum(x[:, :half], x[:, half:]))
            return carry

        lax.fori_loop(0, nkb, max_body, 0)
        m_ref[...] = jnp.broadcast_to(jnp.max(m_ref[...], axis=1, keepdims=True), (grows, LANES))

        def pv_body(kb, carry):
            m = m_ref[...]
            p = jnp.exp2(x_ref[kb] - jnp.concatenate([m, m], axis=1))
            l_ref[...] += p[:, :half] + p[:, half:]
            vb = v_ref[pl.ds(pl.multiple_of(kb * KEY_BLOCK, KEY_BLOCK), KEY_BLOCK), :]
            acc_ref[...] += jnp.dot(p.astype(BF16), vb, preferred_element_type=F32)
            return carry

        lax.fori_loop(0, nkb, pv_body, 0)

        out = acc_ref[...] / jnp.sum(l_ref[...], axis=1, keepdims=True)
        for h in range(DSA_HEAD_GROUP):
            o_ref[:, (h0 + h) * A_HEAD_DIM:(h0 + h + 1) * A_HEAD_DIM] = out[h * tq:(h + 1) * tq].astype(o_ref.dtype)


def _dsa(proj, kbd, kt, tiles, bsz, seq, side_weight, layer, side_cols, side_weight2):
    tq = DSA_ROWS
    nb = seq // tq
    nkb = seq // KEY_BLOCK
    topk = min(TOPK_MAX, seq // 4)
    grows = DSA_HEAD_GROUP * tq
    av_blk = _NEW_OFF["a_v"] // A_HEAD_DIM
    kern = functools.partial(_dsa_kernel, nkb_max=nkb, topk=topk)
    expand = jnp.asarray(np.arange(IDX_HEADS)[:, None] == np.arange(IDX_HEADS * LANES)[None, :] // LANES, BF16)
    wd, wn = side_weight.shape[1], side_weight.shape[2]
    assert wd % (bsz * nb) == 0 and (wd // (bsz * nb)) % 16 == 0 and side_cols >= wn
    wr = wd // (bsz * nb)
    wd2, wn2 = side_weight2.shape[1], side_weight2.shape[2]
    assert wd2 % (bsz * nb) == 0 and (wd2 // (bsz * nb)) % 16 == 0
    wr2 = wd2 // (bsz * nb)
    once = pl.Buffered(1)
    return pl.pallas_call(
        kern,
        grid=(bsz, nb),
        in_specs=[
            pl.BlockSpec((tq, A_Q), lambda b, q: (b * nb + q, _NEW_OFF["a_q"] // A_Q)),
            pl.BlockSpec((tq, IDX_Q), lambda b, q: (b * nb + q, _NEW_OFF["i_q"] // IDX_Q)),
            pl.BlockSpec((tq, LANES), lambda b, q: (b * nb + q, SMALL_OFF // LANES)),
            pl.BlockSpec((None, nkb, 2 * IDX_DIM, 2 * KEY_BLOCK), lambda b, q: (b, 0, 0, 0)),
            pl.BlockSpec((None, nkb, A_HEAD_DIM, KEY_BLOCK), lambda b, q: (b, 0, 0, 0)),
            pl.BlockSpec((seq, A_HEAD_DIM), lambda b, q: (b, av_blk)),
            pl.BlockSpec((3, A_HEADS, Q_BLOCK, Q_BLOCK), lambda b, q: (0, 0, 0, 0), pipeline_mode=once),
            pl.BlockSpec((IDX_HEADS, IDX_HEADS * LANES), lambda b, q: (0, 0), pipeline_mode=once),
            pl.BlockSpec((None, wr, wn), lambda b, q: (layer, b * nb + q, 0)),
            pl.BlockSpec((None, wr2, wn2), lambda b, q: (layer, b * nb + q, 0)),
        ],
        out_specs=[pl.BlockSpec((tq, A_Q), lambda b, q: (b * nb + q, 0)),
                   pl.BlockSpec((wr, side_cols), lambda b, q: (b * nb + q, 0)),
                   pl.BlockSpec((wr2, wn2), lambda b, q: (b * nb + q, 0))],
        out_shape=[jax.ShapeDtypeStruct((bsz * seq, A_Q), BF16),
                   jax.ShapeDtypeStruct((wd, side_cols), BF16),
                   jax.ShapeDtypeStruct((wd2, wn2), BF16)],
        scratch_shapes=[
            pltpu.VMEM((nkb, tq, KEY_BLOCK), I32),
            pltpu.VMEM((nkb, tq, KEY_BLOCK), F32),
            pltpu.VMEM((IDX_HEADS, tq, LANES), F32),
            pltpu.VMEM((tq, LANES), I32),
            pltpu.VMEM((A_HEADS * tq, A_HEAD_DIM), BF16),
            pltpu.VMEM((A_HEADS * tq, LANES), BF16),
            pltpu.VMEM((grows, LANES), F32),
            pltpu.VMEM((grows, LANES), F32),
            pltpu.VMEM((grows, A_HEAD_DIM), F32),
            pltpu.VMEM((nkb, grows, KEY_BLOCK), F32),
        ],
        compiler_params=_cparams(("arbitrary", "arbitrary"), VMEM_LIMIT_FFN),
        name="dsa",
    )(proj, proj, proj, kbd, kt, proj, tiles, expand, side_weight, side_weight2)


def _gla_kernel(q_ref, k_ref, v_ref, r_ref, sm_ref, wg_ref, bg_ref, nrm_ref, tri_ref, sel_ref, wd_ref,
                o_ref, wd_o_ref, st_ref, oacc_ref, *, nchunk, side_rows):
    step = pl.program_id(0) * pl.num_programs(1) + pl.program_id(1)
    _side_cast(wd_ref, wd_o_ref, step * wd_ref.shape[0], side_rows)
    @pl.when(pl.program_id(1) == 0)
    def _():
        st_ref[...] = jnp.zeros(st_ref.shape, F32)

    ts = nchunk * CHUNK
    ri = lax.broadcasted_iota(I32, (ts, ts), 0)
    ci = lax.broadcasted_iota(I32, (ts, ts), 1)
    causal = jnp.logical_and(lax.shift_right_logical(ri, 6) == lax.shift_right_logical(ci, 6), ci <= ri)
    tn = (((0,), (0,)), ((), ()))
    nt = (((1,), (1,)), ((), ()))
    tri = tri_ref[...]
    sel = sel_ref[...]
    glr = sm_ref[:, GLR_LO:GLR_LO + GATE_RANK]

    heads = range(B_HEADS)
    ksl = [slice(h * B_HEAD_K, (h + 1) * B_HEAD_K) for h in heads]
    vsl = [slice(h * B_HEAD_V, (h + 1) * B_HEAD_V) for h in heads]
    qe, ke, kd, dcol = [], [], [], []

    def gates(h):
        ks = ksl[h]
        z = jnp.dot(glr, wg_ref[:, ks], preferred_element_type=F32) + bg_ref[:, ks]
        g = (jnp.minimum(z, 0.0) - jnp.log(1.0 + jnp.exp(-jnp.abs(z)))) * (1.0 / GATE_TAU)
        g_hi = g.astype(BF16)
        g_lo = (g - g_hi.astype(F32)).astype(BF16)
        b = jnp.dot(tri, g_hi, preferred_element_type=F32) + jnp.dot(tri, g_lo, preferred_element_type=F32)
        bl = jnp.concatenate(
            [jnp.broadcast_to(b[(c + 1) * CHUNK - 1:(c + 1) * CHUNK, :], (CHUNK, B_HEAD_K)) for c in range(nchunk)],
            axis=0)
        qc = q_ref[:, ks].astype(F32) * (B_HEAD_K ** -0.5)
        kc = k_ref[:, ks].astype(F32)
        qe.append((qc * jnp.exp(b)).astype(BF16))
        ke.append((kc * jnp.exp(-b)).astype(BF16))
        kd.append((kc * jnp.exp(bl - b)).astype(BF16))
        dcol.append(jnp.exp(lax.dot_general(g_hi, sel, tn, preferred_element_type=F32)
                            + lax.dot_general(g_lo, sel, tn, preferred_element_type=F32)))

    def intra(h):
        a = lax.dot_general(qe[h], ke[h], nt, preferred_element_type=F32)
        a = jnp.where(causal, a, 0.0).astype(BF16)
        oacc_ref[:, vsl[h]] = jnp.dot(a, v_ref[:, vsl[h]], preferred_element_type=F32)

    gates(0)
    for h in heads:
        if h + 1 < B_HEADS:
            gates(h + 1)
        intra(h)

    st = [st_ref[h] for h in heads]
    for c in range(nchunk):
        rs = slice(c * CHUNK, (c + 1) * CHUNK)
        for h in heads:
            oacc_ref[rs, vsl[h]] += jnp.dot(qe[h][rs], st[h].astype(BF16), preferred_element_type=F32)
            dec = jnp.broadcast_to(dcol[h][:, c:c + 1], (B_HEAD_K, B_HEAD_V))
            st[h] = st[h] * dec + lax.dot_general(kd[h][rs], v_ref[rs, vsl[h]], tn, preferred_element_type=F32)
    for h in heads:
        st_ref[h] = st[h]
        o = oacc_ref[:, vsl[h]]
        ms = jnp.mean(o * o, axis=-1, keepdims=True)
        on = o * lax.rsqrt(ms + EPS) * nrm_ref[...]
        rr = r_ref[:, vsl[h]].astype(F32)
        o_ref[:, vsl[h]] = (on * (rr * jax.nn.sigmoid(rr))).astype(o_ref.dtype)


def _gla(proj, w_g2, b_g2, gla_norm, bsz, seq, side_weight, layer, side_rows_padded):
    q_blk = _NEW_OFF["g_q"] // B_QK
    k_blk = _NEW_OFF["g_k"] // B_QK
    v_blk = _NEW_OFF["g_v"] // B_V
    r_blk = _NEW_OFF["g_r"] // B_V
    ts = min(seq, GLA_SEQ_TILE)
    nt = seq // ts
    nchunk = ts // CHUNK
    src_rows, wdim = side_weight.shape[1], side_weight.shape[2]
    nsteps = bsz * nt
    assert side_rows_padded % nsteps == 0 and (side_rows_padded // nsteps) % 16 == 0
    sr = side_rows_padded // nsteps
    last_src = (src_rows - 1) // sr
    kern = functools.partial(_gla_kernel, nchunk=nchunk, side_rows=src_rows)
    rc = np.arange(ts)[:, None] // CHUNK
    cc = np.arange(ts)[None, :] // CHUNK
    tri = jnp.asarray((rc == cc) & (np.arange(ts)[None, :] <= np.arange(ts)[:, None]), BF16)
    sel = jnp.asarray(rc == np.arange(LANES)[None, :], BF16)
    const = lambda shape: pl.BlockSpec(shape, lambda b, t: (0, 0))
    return pl.pallas_call(
        kern,
        grid=(bsz, nt),
        in_specs=[
            pl.BlockSpec((ts, B_QK), lambda b, t: (b * nt + t, q_blk)),
            pl.BlockSpec((ts, B_QK), lambda b, t: (b * nt + t, k_blk)),
            pl.BlockSpec((ts, B_V), lambda b, t: (b * nt + t, v_blk)),
            pl.BlockSpec((ts, B_V), lambda b, t: (b * nt + t, r_blk)),
            pl.BlockSpec((ts, LANES), lambda b, t: (b * nt + t, SMALL_OFF // LANES)),
            pl.BlockSpec((GATE_RANK, B_QK), lambda b, t: (0, 0)),
            pl.BlockSpec((1, B_QK), lambda b, t: (0, 0)),
            pl.BlockSpec((1, B_HEAD_V), lambda b, t: (0, 0)),
            const((ts, ts)), const((ts, LANES)),
            pl.BlockSpec((None, sr, wdim), lambda b, t: (layer, jnp.minimum(b * nt + t, last_src), 0)),
        ],
        out_specs=[pl.BlockSpec((ts, B_V), lambda b, t: (b * nt + t, 0)),
                   pl.BlockSpec((sr, wdim), lambda b, t: (b * nt + t, 0))],
        out_shape=[jax.ShapeDtypeStruct((bsz * seq, B_V), BF16),
                   jax.ShapeDtypeStruct((side_rows_padded, wdim), BF16)],
        scratch_shapes=[pltpu.VMEM((B_HEADS, B_HEAD_K, B_HEAD_V), F32),
                        pltpu.VMEM((ts, B_V), F32)],
        compiler_params=_cparams(("arbitrary", "arbitrary")),
        name="gla",
    )(proj, proj, proj, proj, proj, w_g2.astype(BF16), b_g2.reshape(1, -1), gla_norm.reshape(1, -1),
      tri, sel, side_weight)


def _outproj_mm_kernel(oa_ref, ob_ref, w_ref, x_ref, modj_ref, z_ref, *, alpha):
    a = jnp.concatenate([oa_ref[...], ob_ref[...]], axis=1)
    y = jnp.dot(a, w_ref[...], preferred_element_type=F32)
    z_ref[...] = alpha * x_ref[...] + modj_ref[2:3, :] * y


def _ln_mod_kernel(z_ref, mod_ref, g_ref, b_ref, x1_ref, h2_ref):
    def rows(rs):
        x1 = _layer_norm(z_ref[rs, :]) * g_ref[...] + b_ref[...]
        x1_ref[rs, :] = x1
        h2_ref[rs, :] = (_layer_norm(x1) * (1.0 + mod_ref[4:5, :]) + mod_ref[3:4, :]).astype(h2_ref.dtype)
    _for_row_chunks(z_ref.shape[0], ROW_CHUNK, rows)


def _outproj_split(o_a, o_b, w, x2, mod3, ln_g, ln_b, seq, alpha):
    m, d = x2.shape
    tm = min(1024, seq)
    tn = 1024
    rows_per_seq = seq // tm
    z = pl.pallas_call(
        functools.partial(_outproj_mm_kernel, alpha=alpha),
        grid=(m // tm, d // tn),
        in_specs=[
            pl.BlockSpec((tm, A_Q), lambda i, j: (i, 0)),
            pl.BlockSpec((tm, B_V), lambda i, j: (i, 0)),
            pl.BlockSpec((A_Q + B_V, tn), lambda i, j: (0, j)),
            pl.BlockSpec((tm, tn), lambda i, j: (i, j)),
            pl.BlockSpec((None, 6, tn), lambda i, j: (i // rows_per_seq, 0, j)),
        ],
        out_specs=pl.BlockSpec((tm, tn), lambda i, j: (i, j)),
        out_shape=jax.ShapeDtypeStruct((m, d), F32),
        compiler_params=_cparams(("arbitrary", "arbitrary"), VMEM_LIMIT_FFN),
        name="outproj_mm",
    )(o_a, o_b, w, x2, mod3)
    tr = 256
    per_seq = seq // tr
    return pl.pallas_call(
        _ln_mod_kernel,
        grid=(m // tr,),
        in_specs=[
            pl.BlockSpec((tr, d), lambda i: (i, 0)),
            pl.BlockSpec((None, 6, d), lambda i: (i // per_seq, 0, 0)),
            pl.BlockSpec((1, d), lambda i: (0, 0)),
            pl.BlockSpec((1, d), lambda i: (0, 0)),
        ],
        out_specs=[pl.BlockSpec((tr, d), lambda i: (i, 0)),
                   pl.BlockSpec((tr, d), lambda i: (i, 0))],
        out_shape=[jax.ShapeDtypeStruct((m, d), F32), jax.ShapeDtypeStruct((m, d), BF16)],
        compiler_params=_cparams(("arbitrary",)),
        name="ln1_mod",
    )(z, mod3, ln_g.reshape(1, d), ln_b.reshape(1, d))


def _ffn_kernel(h_ref, wu_ref, wg_ref, cw_ref, cb_ref, wd_ref, x1_hbm, mod_ref, g_ref, b_ref, o_ref,
                carry_ref, act_ref, xbuf_ref, xsem, *, alpha, nj, rows_per_seq):
    i = pl.program_id(0)
    j = pl.program_id(1)
    tm = o_ref.shape[0]
    nchunks = tm // ROW_CHUNK

    def x1_copy(c, slot):
        row0 = pl.multiple_of(i * tm + c * ROW_CHUNK, ROW_CHUNK)
        return pltpu.make_async_copy(x1_hbm.at[pl.ds(row0, ROW_CHUNK), :], xbuf_ref.at[slot], xsem.at[slot])

    def up_gate():
        h = h_ref[...]
        tm = h.shape[0]
        u = jnp.dot(h, wu_ref[...], preferred_element_type=F32)
        gt = jnp.dot(h, wg_ref[...], preferred_element_type=F32)
        first = (i % rows_per_seq) == 0
        prev = jnp.where(first, 0.0, carry_ref[j])
        carry_ref[j] = u[tm - 8:tm, :]
        rid = lax.broadcasted_iota(I32, u.shape, 0)
        p1 = prev[7:8, :]
        p2 = prev[6:7, :]
        u1 = jnp.where(rid == 0, p1, pltpu.roll(u, 1, axis=0))
        u2 = jnp.where(rid == 0, p2, jnp.where(rid == 1, p1, pltpu.roll(u, 2, axis=0)))
        cv = cw_ref[0:1, :] * u2 + cw_ref[1:2, :] * u1 + cw_ref[2:3, :] * u + cb_ref[...]
        cdf = 0.5 * (1.0 + jnp.tanh(math.sqrt(2.0 / math.pi) * (cv + 0.044715 * (cv ** 3))))
        return (cv * cdf * gt).astype(BF16)

    def down(act):
        for c0 in range(0, o_ref.shape[1], FFN_DOWN_CHUNK):
            cs = slice(c0, c0 + FFN_DOWN_CHUNK)
            o_ref[:, cs] += jnp.dot(act, wd_ref[:, cs], preferred_element_type=F32)

    @pl.when(j == 0)
    def _():
        o_ref[...] = jnp.zeros(o_ref.shape, F32)
        act_ref[...] = up_gate()

    @pl.when(jnp.logical_and(j > 0, j < nj))
    def _():
        act_prev = act_ref[...]
        act_ref[...] = up_gate()
        down(act_prev)

    @pl.when(j == nj - 1)
    def _():
        x1_copy(0, 0).start()

    @pl.when(j == nj)
    def _():
        down(act_ref[...])

        def body(c, carry):
            slot = c % 2

            @pl.when(c + 1 < nchunks)
            def _():
                x1_copy(c + 1, 1 - slot).start()

            x1_copy(c, slot).wait()
            rs = pl.ds(pl.multiple_of(c * ROW_CHUNK, ROW_CHUNK), ROW_CHUNK)
            z = alpha * xbuf_ref[slot] + mod_ref[5:6, :] * o_ref[rs, :]
            o_ref[rs, :] = _layer_norm(z) * g_ref[...] + b_ref[...]
            return carry

        lax.fori_loop(0, nchunks, body, 0)


def _ffn(h2, w_up, w_gate, conv_w, conv_b, w_down, x1, mod3, ln_g, ln_b, seq, alpha):
    m, d = h2.shape
    tm, tf = 512, FFN_TILE
    f = w_up.shape[1]
    pad = f - conv_w.shape[1]
    conv_w = jnp.pad(conv_w, ((0, 0), (0, pad)))
    conv_b = jnp.pad(conv_b, (0, pad))
    nj = f // tf
    rows_per_seq = seq // tm
    kern = functools.partial(_ffn_kernel, alpha=alpha, nj=nj, rows_per_seq=rows_per_seq)
    up_tile = lambda i, j: (0, jnp.minimum(j, nj - 1))
    return pl.pallas_call(
        kern,
        grid=(m // tm, nj + 1),
        in_specs=[
            pl.BlockSpec((tm, d), lambda i, j: (i, 0)),
            pl.BlockSpec((d, tf), up_tile),
            pl.BlockSpec((d, tf), up_tile),
            pl.BlockSpec((CONV_W, tf), up_tile),
            pl.BlockSpec((1, tf), up_tile),
            pl.BlockSpec((tf, d), lambda i, j: (jnp.maximum(j - 1, 0), 0)),
            pl.BlockSpec(memory_space=pl.ANY),
            pl.BlockSpec((None, 6, d), lambda i, j: (i // rows_per_seq, 0, 0)),
            pl.BlockSpec((1, d), lambda i, j: (0, 0)),
            pl.BlockSpec((1, d), lambda i, j: (0, 0)),
        ],
        out_specs=pl.BlockSpec((tm, d), lambda i, j: (i, 0)),
        out_shape=jax.ShapeDtypeStruct((m, d), F32),
        scratch_shapes=[pltpu.VMEM((nj, 8, tf), F32),
                        pltpu.VMEM((tm, tf), BF16),
                        pltpu.VMEM((2, ROW_CHUNK, d), F32),
                        pltpu.SemaphoreType.DMA((2,))],
        compiler_params=_cparams(("arbitrary", "arbitrary"), VMEM_LIMIT_FFN),
        name="ffn",
    )(h2, w_up, w_gate, conv_w, conv_b.reshape(1, f), w_down, x1, mod3, ln_g.reshape(1, d), ln_b.reshape(1, d))


def _regroup_kernel(w_ref, o_ref):
    for n in PROJ_ORDER:
        o_ref[_NEW_OFF[n]:_NEW_OFF[n] + _WIDTH[n], :] = w_ref[_OLD_OFF[n]:_OLD_OFF[n] + _WIDTH[n], :].astype(BF16)
    o_ref[_PROJ_USED:, :] = jnp.zeros((PROJ_COLS - _PROJ_USED, o_ref.shape[1]), BF16)


def _regroup_w_in(w, l):
    wt = jnp.swapaxes(w, 1, 2)
    _, n, d = wt.shape
    tc = 256
    return pl.pallas_call(
        _regroup_kernel,
        grid=(d // tc,),
        in_specs=[pl.BlockSpec((None, n, tc), lambda i: (l, 0, i))],
        out_specs=pl.BlockSpec((PROJ_COLS, tc), lambda i: (0, i)),
        out_shape=jax.ShapeDtypeStruct((PROJ_COLS, d), BF16),
        compiler_params=_cparams(("arbitrary",)),
        name="regroup_w_in",
    )(wt)


def kernel(x, c, t5_table, w_ada, b_ada, w_in, w_g2, b_g2, gla_norm, w_out, ln1_g, ln1_b, w_up, w_gate,
           conv_w, conv_b, w_down, ln2_g, ln2_b):
    bsz, seq, d = x.shape
    depth = w_ada.shape[0]
    alpha = (2 * depth) ** 0.25
    nkb = seq // KEY_BLOCK
    x2 = x.reshape(bsz * seq, d)
    tiles = _bias_tiles(t5_table)
    for l in range(depth):
        mod3 = _ada(c, w_ada[l], b_ada[l]).reshape(bsz, 6, d)
        f_pad = -(-w_up.shape[2] // FFN_TILE) * FFN_TILE
        proj, w_gate_b = _inproj(x2, mod3, _regroup_w_in(w_in, l), seq, w_gate, l, f_pad)

        p3 = proj.reshape(bsz, nkb, KEY_BLOCK, PROJ_COLS)
        ko, io = _NEW_OFF["a_k"], _NEW_OFF["i_k"]
        kt = jnp.swapaxes(p3[..., ko:ko + A_HEAD_DIM], 2, 3)
        kit = jnp.swapaxes(p3[..., io:io + IDX_DIM], 2, 3)
        zz = jnp.zeros_like(kit)
        kbd = jnp.concatenate([jnp.concatenate([kit, zz], axis=3),
                               jnp.concatenate([zz, kit], axis=3)], axis=2)

        o_a, w_up_b, w_out_b = _dsa(proj, kbd, kt, tiles, bsz, seq, w_up, l, f_pad, w_out)
        o_b, w_down_b = _gla(proj, w_g2[l], b_g2[l], gla_norm[l], bsz, seq, w_down, l, f_pad)
        x1, h2 = _outproj_split(o_a, o_b, w_out_b, x2, mod3, ln1_g[l], ln1_b[l], seq, alpha)
        x2 = _ffn(h2, w_up_b, w_gate_b, conv_w[l], conv_b[l], w_down_b, x1, mod3, ln2_g[l], ln2_b[l],
                  seq, alpha)
    return x2.reshape(bsz, seq, d)
```
